```python
import jax, jax.numpy as jnp
from jax import lax
import numpy as np

D_MODEL = 4096
BATCH = 1
SEQ = 16384
DEPTH = 4

CTX_LEN = 256
GRID_W = 64
N_HEADS = 16
N_KV_HEADS = 4
HEAD_DIM = 128
GROUP = N_HEADS // N_KV_HEADS
Q_WIDTH = N_HEADS * HEAD_DIM
KV_WIDTH = N_KV_HEADS * HEAD_DIM
WINDOW = 128
BLOCK = 128
N_FGROUPS = 4
F_DIM = 512
F_WIDTH = N_FGROUPS * F_DIM
IN_WIDTH = Q_WIDTH + 2 * KV_WIDTH + F_WIDTH
MIX_WIDTH = Q_WIDTH + F_WIDTH
D_FF = 4 * D_MODEL
ADA_RANK = 512
N_MOD = 6
ROPE_THETA = 10000.0
EPS = 1e-6
NEG_INF = -1e30

kernel_name = "hymba_style_window_gqa_fnet_dit_trunk"


def rmsnorm(x, g):
    xf = x.astype(jnp.float32)
    y = xf * lax.rsqrt(jnp.mean(xf * xf, axis=-1, keepdims=True) + EPS)
    return (y * g.astype(jnp.float32)).astype(x.dtype)


def adaln(cvec, a, b, bias):
    m = (jax.nn.silu(cvec) @ a) @ b + bias
    return jnp.split(m, N_MOD, axis=-1)


def modulate(h, shift, scale):
    return h * (1.0 + scale[:, None, :]) + shift[:, None, :]


def axial_rope(x, rows, cols):
    half = HEAD_DIM // 2
    quarter = half // 2
    inv = ROPE_THETA ** (-jnp.arange(quarter, dtype=jnp.float32) / quarter)

    def rot(xs, pos):
        ang = pos.astype(jnp.float32)[:, None] * inv[None, :]
        cos = jnp.cos(ang)[None, :, None, :]
        sin = jnp.sin(ang)[None, :, None, :]
        x1 = xs[..., :quarter].astype(jnp.float32)
        x2 = xs[..., quarter:].astype(jnp.float32)
        return jnp.concatenate([x1 * cos - x2 * sin, x2 * cos + x1 * sin], axis=-1)

    out = jnp.concatenate([rot(x[..., :half], rows), rot(x[..., half:], cols)], axis=-1)
    return out.astype(x.dtype)


def latent_attention(q, k, v, k_ctx, v_ctx, sink):
    B, L = q.shape[0], q.shape[1]
    C = k_ctx.shape[1]
    nb = L // BLOCK
    scale = HEAD_DIM ** -0.5
    qb = q.reshape(B, nb, BLOCK, N_KV_HEADS, GROUP, HEAD_DIM)

    def band(t):
        tp = jnp.pad(t, ((0, 0), (BLOCK, BLOCK), (0, 0), (0, 0)))
        tp = tp.reshape(B, nb + 2, BLOCK, N_KV_HEADS, HEAD_DIM)
        return jnp.concatenate([tp[:, :-2], tp[:, 1:-1], tp[:, 2:]], axis=2)

    kw, vw = band(k), band(v)
    s_win = jnp.einsum('bnqhgd,bnkhd->bnhgqk', qb, kw).astype(jnp.float32) * scale
    s_ctx = jnp.einsum('bnqhgd,bchd->bnhgqc', qb, k_ctx).astype(jnp.float32) * scale
    q_off = jnp.arange(BLOCK)[:, None]
    k_off = jnp.arange(3 * BLOCK)[None, :] - BLOCK
    rel_ok = jnp.abs(k_off - q_off) <= WINDOW
    abs_k = jnp.arange(nb)[:, None] * BLOCK + k_off
    in_range = (abs_k >= 0) & (abs_k < L)
    mask = rel_ok[None, :, :] & in_range[:, None, :]
    s_win = jnp.where(mask[None, :, None, None, :, :], s_win, NEG_INF)
    sink_col = jnp.broadcast_to(
        sink.reshape(N_KV_HEADS, GROUP).astype(jnp.float32)[None, None, :, :, None, None],
        s_win.shape[:-1] + (1,))
    p = jax.nn.softmax(jnp.concatenate([sink_col, s_ctx, s_win], axis=-1), axis=-1)
    p_ctx = p[..., 1:1 + C].astype(v.dtype)
    p_win = p[..., 1 + C:].astype(v.dtype)
    o = (jnp.einsum('bnhgqc,bchd->bnqhgd', p_ctx, v_ctx)
         + jnp.einsum('bnhgqk,bnkhd->bnqhgd', p_win, vw))
    return o.reshape(B, L, Q_WIDTH)


def context_attention(q, k, v, sink):
    B, C = q.shape[0], q.shape[1]
    scale = HEAD_DIM ** -0.5
    qg = q.reshape(B, C, N_KV_HEADS, GROUP, HEAD_DIM)
    s = jnp.einsum('bqhgd,bkhd->bhgqk', qg, k).astype(jnp.float32) * scale
    sink_col = jnp.broadcast_to(
        sink.reshape(N_KV_HEADS, GROUP).astype(jnp.float32)[None, :, :, None, None],
        s.shape[:-1] + (1,))
    p = jax.nn.softmax(jnp.concatenate([sink_col, s], axis=-1), axis=-1)[..., 1:]
    o = jnp.einsum('bhgqk,bkhd->bqhgd', p.astype(v.dtype), v)
    return o.reshape(B, C, Q_WIDTH)


def fourier_mix(u, w_f):
    B, L = u.shape[0], u.shape[1]
    ug = u.reshape(B, L, N_FGROUPS, F_DIM).astype(jnp.float32)
    f = jnp.real(jnp.fft.fft2(ug, axes=(1, 3), norm='ortho')).astype(u.dtype)
    return jnp.einsum('blgc,gcd->blgd', f, w_f).reshape(B, L, F_WIDTH)


def merge_groups(attn_out, four_out, g_a, g_f, w_o):
    return jnp.concatenate([rmsnorm(attn_out, g_a), rmsnorm(four_out, g_f)], axis=-1) @ w_o


def sq_relu_mlp(h, w1, w2):
    return jnp.square(jax.nn.relu(h @ w1)) @ w2


def setup_inputs(seed: int = 0) -> dict:
    key = jax.random.key(seed)
    ks = jax.random.split(key, 18)

    def nrm(k, shape, s):
        return jax.random.normal(k, shape, jnp.float32) * s

    def gain(k, shape):
        return 1.0 + 0.02 * jax.random.normal(k, shape, jnp.float32)

    return {
        "x": nrm(ks[0], (BATCH, SEQ, D_MODEL), 1.0),
        "c": nrm(ks[1], (BATCH, D_MODEL), 1.0),
        "ctx": nrm(ks[2], (BATCH, CTX_LEN, D_MODEL), 1.0),
        "c_ctx": nrm(ks[3], (D_MODEL,), 1.0),
        "ada_a": nrm(ks[4], (DEPTH, D_MODEL, ADA_RANK), D_MODEL ** -0.5),
        "ada_b": nrm(ks[5], (DEPTH, ADA_RANK, N_MOD * D_MODEL), 0.2 * ADA_RANK ** -0.5),
        "ada_bias": nrm(ks[6], (DEPTH, N_MOD * D_MODEL), 0.01),
        "g_mix": gain(ks[7], (DEPTH, D_MODEL)),
        "w_in": nrm(ks[8], (DEPTH, D_MODEL, IN_WIDTH), D_MODEL ** -0.5),
        "sink": nrm(ks[9], (DEPTH, N_HEADS), 0.5),
        "w_f": nrm(ks[10], (DEPTH, N_FGROUPS, F_DIM, F_DIM), F_DIM ** -0.5),
        "g_attn_out": gain(ks[11], (DEPTH, Q_WIDTH)),
        "g_four_out": gain(ks[12], (DEPTH, F_WIDTH)),
        "w_out": nrm(ks[13], (DEPTH, MIX_WIDTH, D_MODEL), MIX_WIDTH ** -0.5),
        "g_mlp": gain(ks[14], (DEPTH, D_MODEL)),
        "w1": nrm(ks[15], (DEPTH, D_MODEL, D_FF), D_MODEL ** -0.5),
        "w2": nrm(ks[16], (DEPTH, D_FF, D_MODEL), D_FF ** -0.5),
        "g_final": gain(ks[17], (D_MODEL,)),
    }


def reference(x, c, ctx, c_ctx, ada_a, ada_b, ada_bias, g_mix, w_in, sink, w_f,
              g_attn_out, g_four_out, w_out, g_mlp, w1, w2, g_final):
    B, L = x.shape[0], x.shape[1]
    C = ctx.shape[1]
    ROWS = L // GRID_W
    rows = jnp.broadcast_to(jnp.arange(ROWS)[:, None], (ROWS, GRID_W)).reshape(L)
    cols = jnp.broadcast_to(jnp.arange(GRID_W)[None, :], (ROWS, GRID_W)).reshape(L)
    cc = c_ctx[None, :]
    k_lo, v_lo, f_lo = Q_WIDTH, Q_WIDTH + KV_WIDTH, Q_WIDTH + 2 * KV_WIDTH

    for l in range(DEPTH):
        last = l == DEPTH - 1
        w = w_in[l]
        sh_a, sc_a, ga_a, sh_m, sc_m, ga_m = adaln(c, ada_a[l], ada_b[l], ada_bias[l])
        csh_a, csc_a, cga_a, csh_m, csc_m, cga_m = adaln(cc, ada_a[l], ada_b[l], ada_bias[l])

        hc = modulate(rmsnorm(ctx, g_mix[l]), csh_a, csc_a)
        kc = (hc @ w[:, k_lo:v_lo]).reshape(B, C, N_KV_HEADS, HEAD_DIM)
        vc = (hc @ w[:, v_lo:f_lo]).reshape(B, C, N_KV_HEADS, HEAD_DIM)

        hx = modulate(rmsnorm(x, g_mix[l]), sh_a, sc_a)
        px = hx @ w
        qx = axial_rope(px[..., :k_lo].reshape(B, L, N_HEADS, HEAD_DIM), rows, cols)
        kx = axial_rope(px[..., k_lo:v_lo].reshape(B, L, N_KV_HEADS, HEAD_DIM), rows, cols)
        vx = px[..., v_lo:f_lo].reshape(B, L, N_KV_HEADS, HEAD_DIM)
        fx = px[..., f_lo:]
        ax = latent_attention(qx, kx, vx, kc, vc, sink[l])
        fo = fourier_mix(fx, w_f[l])
        x_new = x + ga_a[:, None, :] * merge_groups(ax, fo, g_attn_out[l], g_four_out[l], w_out[l])
        hm = modulate(rmsnorm(x_new, g_mlp[l]), sh_m, sc_m)
        x_new = x_new + ga_m[:, None, :] * sq_relu_mlp(hm, w1[l], w2[l])

        if not last:
            qc = (hc @ w[:, :k_lo]).reshape(B, C, N_HEADS, HEAD_DIM)
            fc = hc @ w[:, f_lo:]
            acx = context_attention(qc, kc, vc, sink[l])
            fco = fourier_mix(fc, w_f[l])
            ctx_new = ctx + cga_a[:, None, :] * merge_groups(acx, fco, g_attn_out[l], g_four_out[l], w_out[l])
            hcm = modulate(rmsnorm(ctx_new, g_mlp[l]), csh_m, csc_m)
            ctx = ctx_new + cga_m[:, None, :] * sq_relu_mlp(hcm, w1[l], w2[l])
        x = x_new

    return rmsnorm(x, g_final)
```

```python
import functools
import math

import numpy as np
import jax
import jax.numpy as jnp
from jax import lax
from jax.experimental import pallas as pl
from jax.experimental.pallas import tpu as pltpu

F32 = jnp.float32
BF16 = jnp.bfloat16

N_HEADS = 16
N_KV_HEADS = 4
HEAD_DIM = 128
GROUP = N_HEADS // N_KV_HEADS
Q_WIDTH = N_HEADS * HEAD_DIM
KV_WIDTH = N_KV_HEADS * HEAD_DIM
BLOCK = 128
GRID_W = 64
N_FGROUPS = 4
F_DIM = 512
F_WIDTH = N_FGROUPS * F_DIM
N_MOD = 6
ROPE_THETA = 10000.0
EPS = 1e-6
NEG_INF = -1e30

V7X_VMEM_BYTES = 64 * 1024 * 1024
LANES = 128
ADA_ROWS = 8
MLP_OUT_CHUNK = 512
ROW_CHUNK = 64


def _params(semantics, vmem_bytes):
    limit = min(int(vmem_bytes * 1.15) + (4 << 20), V7X_VMEM_BYTES - (6 << 20))
    return pltpu.CompilerParams(dimension_semantics=semantics, vmem_limit_bytes=limit)


def _dot(a, b):
    return jnp.dot(a, b, preferred_element_type=F32)


def _dot_nt(a, b):
    return lax.dot_general(a, b, (((1,), (1,)), ((), ())), preferred_element_type=F32)


def _rms(x, g):
    return x * lax.rsqrt(jnp.mean(x * x, axis=-1, keepdims=True) + EPS) * g


def _for_row_chunks(n_rows, body):
    rc = min(ROW_CHUNK, n_rows)

    def step(r, carry):
        body(pl.ds(pl.multiple_of(r * rc, rc), rc))
        return carry

    lax.fori_loop(0, n_rows // rc, step, 0)


def _normmod_rows(x_ref, gss_ref, h_ref, copy_ref=None):
    def body(rows):
        x = x_ref[rows, :]
        y = _rms(x, gss_ref[0:1, :])
        h_ref[rows, :] = (y * (1.0 + gss_ref[2:3, :]) + gss_ref[1:2, :]).astype(h_ref.dtype)
        if copy_ref is not None:
            copy_ref[rows, :] = x

    _for_row_chunks(x_ref.shape[0], body)


def _ada_kernel(cv_ref, a_ref, b_ref, bias_ref, o_ref, h_ref):
    @pl.when(pl.program_id(1) == 0)
    def _():
        cv = cv_ref[...]
        s = cv * jax.nn.sigmoid(cv)
        h_ref[...] = _dot(s.astype(BF16), a_ref[0].astype(BF16))

    o_ref[0] = _dot(h_ref[...].astype(BF16), b_ref[0].astype(BF16)) + bias_ref[0]


def _ada(cv, ada_a, ada_b, ada_bias):
    depth, d, rank = ada_a.shape
    n = ada_b.shape[2]
    tn = d
    vmem = 2 * (d * rank * 4 + rank * tn * 4) + 4 * ADA_ROWS * (d + tn) * 4
    return pl.pallas_call(
        _ada_kernel,
        grid=(depth, n // tn),
        in_specs=[
            pl.BlockSpec((ADA_ROWS, d), lambda l, j: (0, 0)),
            pl.BlockSpec((1, d, rank), lambda l, j: (l, 0, 0)),
            pl.BlockSpec((1, rank, tn), lambda l, j: (l, 0, j)),
            pl.BlockSpec((1, 1, tn), lambda l, j: (l, 0, j)),
        ],
        out_specs=pl.BlockSpec((1, ADA_ROWS, tn), lambda l, j: (l, 0, j)),
        out_shape=jax.ShapeDtypeStruct((depth, ADA_ROWS, n), F32),
        scratch_shapes=[pltpu.VMEM((ADA_ROWS, rank), F32)],
        compiler_params=_params(("parallel", "arbitrary"), vmem),
        name="ada",
    )(cv, ada_a, ada_b, ada_bias.reshape(depth, 1, n))


def _rope_chunk(a, cos, sin, lane):
    up = pltpu.roll(a, HEAD_DIM - 32, axis=1)
    down = pltpu.roll(a, 32, axis=1)
    partner = jnp.where((lane % 64) < 32, up, down)
    return a * cos + partner * sin


def _inproj_kernel(x_ref, gss_ref, w_ref, *rest, rope_cols, tn):
    if rope_cols:
        cos_ref, sin_ref, o_ref, h_ref = rest
    else:
        o_ref, h_ref = rest
    j = pl.program_id(1)

    @pl.when(j == 0)
    def _():
        _normmod_rows(x_ref, gss_ref, h_ref)

    acc = _dot(h_ref[...], w_ref[...])
    n_chunks = tn // HEAD_DIM

    def store(n_rope):
        if n_rope:
            cos = cos_ref[...]
            sin = sin_ref[...]
            lane = lax.broadcasted_iota(jnp.int32, cos.shape, 1)
        for c in range(n_chunks):
            sl = slice(c * HEAD_DIM, (c + 1) * HEAD_DIM)
            a = acc[:, sl]
            if c < n_rope:
                a = _rope_chunk(a, cos, sin, lane)
            o_ref[:, sl] = a.astype(o_ref.dtype)

    full_tiles, part = divmod(rope_cols, tn)
    if rope_cols == 0:
        store(0)
    else:
        pl.when(j < full_tiles)(functools.partial(store, n_chunks))
        if part:
            pl.when(j == full_tiles)(functools.partial(store, part // HEAD_DIM))
            pl.when(j > full_tiles)(functools.partial(store, 0))
        else:
            pl.when(j >= full_tiles)(functools.partial(store, 0))


def _inproj(x, gss, w, rope_tables, tm, tn):
    n_rows, d = x.shape
    n_out = w.shape[1]
    rope_cols = Q_WIDTH + KV_WIDTH if rope_tables is not None else 0
    in_specs = [
        pl.BlockSpec((tm, d), lambda i, j: (i, 0)),
        pl.BlockSpec((8, d), lambda i, j: (0, 0)),
        pl.BlockSpec((d, tn), lambda i, j: (0, j)),
    ]
    args = [x, gss, w]
    if rope_tables is not None:
        in_specs += [pl.BlockSpec((tm, HEAD_DIM), lambda i, j: (i, 0))] * 2
        args += list(rope_tables)
    vmem = 2 * (tm * d * 4 + d * tn * 2 + tm * tn * 2 + 2 * tm * HEAD_DIM * 4) + tm * d * 2 + 2 * tm * tn * 4
    return pl.pallas_call(
        functools.partial(_inproj_kernel, rope_cols=rope_cols, tn=tn),
        grid=(n_rows // tm, n_out // tn),
        in_specs=in_specs,
        out_specs=pl.BlockSpec((tm, tn), lambda i, j: (i, j)),
        out_shape=jax.ShapeDtypeStruct((n_rows, n_out), BF16),
        scratch_shapes=[pltpu.VMEM((tm, d), BF16)],
        compiler_params=_params(("parallel", "arbitrary"), vmem),
        name="inproj_rope" if rope_cols else "inproj_ctx",
    )(*args)


def _attend(qh, k_all, v_all, sink_col, mask):
    s = _dot_nt(qh, k_all) * (HEAD_DIM ** -0.5)
    if mask is not None:
        s = jnp.where(mask, s, NEG_INF)
    m = jnp.maximum(jnp.max(s, axis=-1, keepdims=True), sink_col)
    e = jnp.exp(s - m)
    denom = jnp.sum(e, axis=-1, keepdims=True) + jnp.exp(sink_col - m)
    p = e * (1.0 / denom)
    return _dot(p.astype(BF16), v_all)


def _stack_heads(q, h):
    return jnp.concatenate(
        [q[:, (h * GROUP + g) * HEAD_DIM:(h * GROUP + g + 1) * HEAD_DIM] for g in range(GROUP)], axis=0)


def _sink_col(sink_ref, h, rows):
    return jnp.concatenate(
        [jnp.full((rows, 1), sink_ref[h * GROUP + g], F32) for g in range(GROUP)], axis=0)


def _latent_attn_kernel(sink_ref, q_ref, kp_ref, kc_ref, kn_ref, vp_ref, vc_ref, vn_ref,
                        kx_ref, vx_ref, g_ref, o_ref, acc_ref, *, n_ctx):
    n = pl.program_id(0)
    nb = pl.num_programs(0)
    rows = GROUP * BLOCK
    cols = n_ctx + 3 * BLOCK
    qpos = lax.broadcasted_iota(jnp.int32, (rows, cols), 0) % BLOCK
    col = lax.broadcasted_iota(jnp.int32, (rows, cols), 1)
    kk = col - n_ctx
    lo = jnp.where(n == 0, BLOCK, 0)
    hi = jnp.where(n == nb - 1, 2 * BLOCK, 3 * BLOCK)
    mask = (col < n_ctx) | ((kk >= qpos) & (kk <= qpos + 2 * BLOCK) & (kk >= lo) & (kk < hi))
    q = q_ref[...]
    for h in range(N_KV_HEADS):
        hs = slice(h * HEAD_DIM, (h + 1) * HEAD_DIM)
        k_all = jnp.concatenate([kx_ref[:, hs], kp_ref[:, hs], kc_ref[:, hs], kn_ref[:, hs]], axis=0)
        v_all = jnp.concatenate([vx_ref[:, hs], vp_ref[:, hs], vc_ref[:, hs], vn_ref[:, hs]], axis=0)
        o = _attend(_stack_heads(q, h), k_all, v_all, _sink_col(sink_ref, h, BLOCK), mask)
        for g in range(GROUP):
            c0 = (h * GROUP + g) * HEAD_DIM
            acc_ref[:, c0:c0 + HEAD_DIM] = o[g * BLOCK:(g + 1) * BLOCK, :]
    o_ref[...] = _rms(acc_ref[...], g_ref[...]).astype(o_ref.dtype)


def _latent_attn(px, pc, sink, g_attn):
    n_rows = px.shape[0]
    n_ctx = pc.shape[0]
    nb = n_rows // BLOCK
    kcol = Q_WIDTH // KV_WIDTH
    vcol = kcol + 1

    def kv_spec(colblk, shift):
        return pl.BlockSpec(
            (BLOCK, KV_WIDTH), lambda n: (jnp.clip(n + shift, 0, nb - 1), colblk))

    in_specs = [
        pl.BlockSpec(memory_space=pltpu.SMEM),
        pl.BlockSpec((BLOCK, Q_WIDTH), lambda n: (n, 0)),
        kv_spec(kcol, -1), kv_spec(kcol, 0), kv_spec(kcol, 1),
        kv_spec(vcol, -1), kv_spec(vcol, 0), kv_spec(vcol, 1),
        pl.BlockSpec((n_ctx, KV_WIDTH), lambda n: (0, kcol)),
        pl.BlockSpec((n_ctx, KV_WIDTH), lambda n: (0, vcol)),
        pl.BlockSpec((1, Q_WIDTH), lambda n: (0, 0)),
    ]
    vmem = 16 << 20
    return pl.pallas_call(
        functools.partial(_latent_attn_kernel, n_ctx=n_ctx),
        grid=(nb,),
        in_specs=in_specs,
        out_specs=pl.BlockSpec((BLOCK, Q_WIDTH), lambda n: (n, 0)),
        out_shape=jax.ShapeDtypeStruct((n_rows, Q_WIDTH), BF16),
        scratch_shapes=[pltpu.VMEM((BLOCK, Q_WIDTH), F32)],
        compiler_params=_params(("parallel",), vmem),
        name="latent_attn",
    )(sink, px, px, px, px, px, px, px, pc, pc, g_attn.reshape(1, Q_WIDTH))


def _ctx_attn_kernel(sink_ref, q_ref, k_ref, v_ref, g_ref, o_ref, acc_ref):
    n_ctx = q_ref.shape[0]
    q = q_ref[...]
    for h in range(N_KV_HEADS):
        hs = slice(h * HEAD_DIM, (h + 1) * HEAD_DIM)
        o = _attend(_stack_heads(q, h), k_ref[:, hs], v_ref[:, hs], _sink_col(sink_ref, h, n_ctx), None)
        for g in range(GROUP):
            c0 = (h * GROUP + g) * HEAD_DIM
            acc_ref[:, c0:c0 + HEAD_DIM] = o[g * n_ctx:(g + 1) * n_ctx, :]
    o_ref[...] = _rms(acc_ref[...], g_ref[...]).astype(o_ref.dtype)


def _ctx_attn(pc, sink, g_attn):
    n_ctx = pc.shape[0]
    kcol = Q_WIDTH // KV_WIDTH
    return pl.pallas_call(
        _ctx_attn_kernel,
        grid=(1,),
        in_specs=[
            pl.BlockSpec(memory_space=pltpu.SMEM),
            pl.BlockSpec((n_ctx, Q_WIDTH), lambda i: (0, 0)),
            pl.BlockSpec((n_ctx, KV_WIDTH), lambda i: (0, kcol)),
            pl.BlockSpec((n_ctx, KV_WIDTH), lambda i: (0, kcol + 1)),
            pl.BlockSpec((1, Q_WIDTH), lambda i: (0, 0)),
        ],
        out_specs=pl.BlockSpec((n_ctx, Q_WIDTH), lambda i: (0, 0)),
        out_shape=jax.ShapeDtypeStruct((n_ctx, Q_WIDTH), BF16),
        scratch_shapes=[pltpu.VMEM((n_ctx, Q_WIDTH), F32)],
        compiler_params=_params(("arbitrary",), 16 << 20),
        name="ctx_attn",
    )(sink, pc, pc, pc, g_attn.reshape(1, Q_WIDTH))


def _dft_tables(n):
    ang = 2.0 * np.pi * (np.outer(np.arange(n), np.arange(n)) % n) / n
    return np.cos(ang), np.sin(ang)


def _fourier_head_kernel(g_ref, z_ref, o_ref):
    o_ref[...] = _dot(g_ref[...], z_ref[...]).astype(o_ref.dtype)


def _fourier_head(zt, gmat):
    nb, na, c = zt.shape
    return pl.pallas_call(
        _fourier_head_kernel,
        grid=(nb,),
        in_specs=[
            pl.BlockSpec((None, 2 * na, na), lambda b: (b, 0, 0)),
            pl.BlockSpec((None, na, c), lambda b: (b, 0, 0)),
        ],
        out_specs=pl.BlockSpec((None, 2 * na, c), lambda b: (b, 0, 0)),
        out_shape=jax.ShapeDtypeStruct((nb, 2 * na, c), BF16),
        compiler_params=_params(("parallel",), 8 << 20),
        name="fourier_head",
    )(gmat, zt)


def _fourier_tail_kernel(m_ref, d_ref, cs_ref, wf_ref, g_ref, o_ref, acc_ref, *, pos_scale):
    p = o_ref.shape[0]
    x = (_dot(m_ref[...], d_ref[...]) * pos_scale).astype(BF16)
    for g in range(N_FGROUPS):
        gs = slice(g * F_DIM, (g + 1) * F_DIM)
        lhs = jnp.concatenate([x[:p, gs], x[p:, gs]], axis=1)
        f = _dot(lhs, cs_ref[...]) * (F_DIM ** -0.5)
        acc_ref[:, gs] = _dot(f.astype(BF16), wf_ref[g])
    o_ref[...] = _rms(acc_ref[...], g_ref[...]).astype(o_ref.dtype)


def _fourier_tail(stage_mat, data, cs, wf, g_four, pos_scale):
    nblk, k_in, c = data.shape
    p = stage_mat.shape[0] // 2
    return pl.pallas_call(
        functools.partial(_fourier_tail_kernel, pos_scale=pos_scale),
        grid=(nblk,),
        in_specs=[
            pl.BlockSpec((2 * p, k_in), lambda i: (0, 0)),
            pl.BlockSpec((None, k_in, c), lambda i: (i, 0, 0)),
            pl.BlockSpec((2 * F_DIM, F_DIM), lambda i: (0, 0)),
            pl.BlockSpec((N_FGROUPS, F_DIM, F_DIM), lambda i: (0, 0, 0)),
            pl.BlockSpec((1, c), lambda i: (0, 0)),
        ],
        out_specs=pl.BlockSpec((None, p, c), lambda i: (i, 0, 0)),
        out_shape=jax.ShapeDtypeStruct((nblk, p, c), BF16),
        scratch_shapes=[pltpu.VMEM((p, c), F32)],
        compiler_params=_params(("parallel",), 24 << 20),
        name="fourier_tail",
    )(stage_mat, data, cs, wf, g_four.reshape(1, c))


class _FourierConsts:
    def __init__(self, n_lat, n_ctx):
        a = b = int(round(math.sqrt(n_lat)))
        assert a * b == n_lat
        self.a, self.b = a, b
        k_lo = np.arange(a)[None, :, None]
        n = (np.arange(a)[None, None, :] * b + np.arange(b)[:, None, None])
        ang = 2.0 * np.pi * ((k_lo * n) % n_lat) / n_lat
        self.head = jnp.asarray(np.concatenate([np.cos(ang), -np.sin(ang)], axis=1), BF16)
        cb, sb = _dft_tables(b)
        self.tail = jnp.asarray(np.block([[cb, sb], [-sb, cb]]), BF16)
        cc, sc = _dft_tables(n_ctx)
        self.ctx = jnp.asarray(np.concatenate([cc, -sc], axis=0), BF16)
        cf, sf = _dft_tables(F_DIM)
        self.chan = jnp.asarray(np.concatenate([cf, sf], axis=0), BF16)


def _latent_fourier(px, fc, wf, g_four):
    n_rows = px.shape[0]
    a, b = fc.a, fc.b
    f_lo = Q_WIDTH + 2 * KV_WIDTH
    z = px[:, f_lo:].reshape(a, b, F_WIDTH)
    zt = jnp.transpose(z, (1, 0, 2))
    t = _fourier_head(zt, fc.head)
    t = jnp.transpose(t.reshape(b, 2, a, F_WIDTH), (2, 1, 0, 3)).reshape(a, 2 * b, F_WIDTH)
    o = _fourier_tail(fc.tail, t, fc.chan, wf, g_four, 1.0 / math.sqrt(n_rows))
    return jnp.transpose(o, (1, 0, 2)).reshape(n_rows, F_WIDTH)


def _ctx_fourier(pc, fc, wf, g_four):
    n_ctx = pc.shape[0]
    f_lo = Q_WIDTH + 2 * KV_WIDTH
    d = pc[:, f_lo:].reshape(1, n_ctx, F_WIDTH)
    return _fourier_tail(fc.ctx, d, fc.chan, wf, g_four, 1.0 / math.sqrt(n_ctx)).reshape(n_ctx, F_WIDTH)


def _outproj_kernel(na_ref, nf_ref, w_ref, x_ref, gate_ref, o_ref):
    ka = na_ref.shape[1]
    acc = _dot(na_ref[...], w_ref[:ka, :]) + _dot(nf_ref[...], w_ref[ka:, :])
    o_ref[...] = x_ref[...] + gate_ref[...] * acc


def _outproj(na, nf, w, x, gate, tm, tn):
    n_rows, d = x.shape
    ka, kf = na.shape[1], nf.shape[1]
    vmem = 2 * (tm * (ka + kf) * 2 + (ka + kf) * tn * 2 + 2 * tm * tn * 4) + tm * tn * 4
    return pl.pallas_call(
        _outproj_kernel,
        grid=(n_rows // tm, d // tn),
        in_specs=[
            pl.BlockSpec((tm, ka), lambda i, j: (i, 0)),
            pl.BlockSpec((tm, kf), lambda i, j: (i, 0)),
            pl.BlockSpec((ka + kf, tn), lambda i, j: (0, j)),
            pl.BlockSpec((tm, tn), lambda i, j: (i, j)),
            pl.BlockSpec((1, tn), lambda i, j: (0, j)),
        ],
        out_specs=pl.BlockSpec((tm, tn), lambda i, j: (i, j)),
        out_shape=jax.ShapeDtypeStruct((n_rows, d), F32),
        compiler_params=_params(("parallel", "parallel"), vmem),
        name="outproj",
    )(na, nf, w, x, gate.reshape(1, d))


def _mlp_kernel(x_ref, gss_ref, w1_ref, w2_ref, o_ref, h_ref, *, final):
    f = pl.program_id(1)

    @pl.when(f == 0)
    def _():
        _normmod_rows(x_ref, gss_ref, h_ref, copy_ref=o_ref)

    u =jnp.maximum(_dot(h_ref[...], w1_ref[...]), 0.0)
    u = (u * u).astype(BF16)
    tn = MLP_OUT_CHUNK
    for c in range(o_ref.shape[1] // tn):
        cs = slice(c * tn, (c + 1) * tn)
        o_ref[:, cs] += gss_ref[3:4, cs] * _dot(u, w2_ref[:, cs])

    if final:
        @pl.when(f == pl.num_programs(1) - 1)
        def _():
            def body(rows):
                o_ref[rows, :] = _rms(o_ref[rows, :], gss_ref[4:5, :])

            _for_row_chunks(o_ref.shape[0], body)


def _mlp(x, gss, w1, w2, tm, tf, final):
    n_rows, d = x.shape
    d_ff = w1.shape[1]
    vmem = 2 * (2 * tm * d * 4 + 2 * d * tf * 2) + tm * d * 2 + tm * tf * 6 + 2 * tm * MLP_OUT_CHUNK * 4
    return pl.pallas_call(
        functools.partial(_mlp_kernel, final=final),
        grid=(n_rows // tm, d_ff // tf),
        in_specs=[
            pl.BlockSpec((tm, d), lambda i, f: (i, 0)),
            pl.BlockSpec((8, d), lambda i, f: (0, 0)),
            pl.BlockSpec((d, tf), lambda i, f: (0, f)),
            pl.BlockSpec((tf, d), lambda i, f: (f, 0)),
        ],
        out_specs=pl.BlockSpec((tm, d), lambda i, f: (i, 0)),
        out_shape=jax.ShapeDtypeStruct((n_rows, d), F32),
        scratch_shapes=[pltpu.VMEM((tm, d), BF16)],
        compiler_params=_params(("parallel", "arbitrary"), vmem),
        name="mlp_final" if final else "mlp",
    )(x, gss, w1, w2)


def _rope_tables(n_rows):
    quarter = HEAD_DIM // 4
    inv = ROPE_THETA ** (-jnp.arange(quarter, dtype=F32) / quarter)
    pos = jnp.arange(n_rows)
    rows = (pos // GRID_W).astype(F32)
    cols = (pos % GRID_W).astype(F32)
    ang = jnp.concatenate([rows[:, None] * inv[None, :]] * 2 + [cols[:, None] * inv[None, :]] * 2, axis=1)
    sign = jnp.tile(jnp.concatenate([-jnp.ones(quarter, F32), jnp.ones(quarter, F32)]), 2)
    return jnp.cos(ang), jnp.sin(ang) * sign[None, :]


def _rows8(*vecs):
    d = vecs[0].shape[0]
    pad = [jnp.zeros((d,), F32)] * (8 - len(vecs))
    return jnp.stack(list(vecs) + pad, axis=0)


def _row_tile(n_rows, want):
    return min(want, n_rows)


def kernel(x, c, ctx, c_ctx, ada_a, ada_b, ada_bias, g_mix, w_in, sink, w_f, g_attn_out, g_four_out,
           w_out, g_mlp, w1, w2, g_final):
    assert x.shape[0] == 1 and ctx.shape[0] == 1
    depth = w_in.shape[0]
    d = x.shape[2]
    xs = x[0]
    cs = ctx[0]
    n_lat, n_ctx = xs.shape[0], cs.shape[0]

    cv = jnp.zeros((ADA_ROWS, d), F32).at[0].set(c[0]).at[1].set(c_ctx)
    mods = _ada(cv, ada_a, ada_b, ada_bias)
    rope = _rope_tables(n_lat)
    fconst = _FourierConsts(n_lat, n_ctx)

    for l in range(depth):
        last = l == depth - 1
        m_lat = [mods[l, 0, i * d:(i + 1) * d] for i in range(N_MOD)]
        m_ctx = [mods[l, 1, i * d:(i + 1) * d] for i in range(N_MOD)]
        w_in_l = w_in[l].astype(BF16)
        w_out_l = w_out[l].astype(BF16)
        w1_l = w1[l].astype(BF16)
        w2_l = w2[l].astype(BF16)
        wf_l = w_f[l].astype(BF16)

        pc = _inproj(cs, _rows8(g_mix[l], m_ctx[0], m_ctx[1]), w_in_l, None, _row_tile(n_ctx, 512), 1024)
        px = _inproj(xs, _rows8(g_mix[l], m_lat[0], m_lat[1]), w_in_l, rope, _row_tile(n_lat, 512), 1024)

        na = _latent_attn(px, pc, sink[l], g_attn_out[l])
        nf = _latent_fourier(px, fconst, wf_l, g_four_out[l])
        x_mid = _outproj(na, nf, w_out_l, xs, m_lat[2], _row_tile(n_lat, 1024), 1024)
        gss = _rows8(g_mlp[l], m_lat[3], m_lat[4], m_lat[5], g_final)
        xs = _mlp(x_mid, gss, w1_l, w2_l, _row_tile(n_lat, 512), 512, final=last)

        if not last:
            nac = _ctx_attn(pc, sink[l], g_attn_out[l])
            nfc = _ctx_fourier(pc, fconst, wf_l, g_four_out[l])
            c_mid = _outproj(nac, nfc, w_out_l, cs, m_ctx[2], _row_tile(n_ctx, 1024), 1024)
            gss_c = _rows8(g_mlp[l], m_ctx[3], m_ctx[4], m_ctx[5])
            cs = _mlp(c_mid, gss_c, w1_l, w2_l, _row_tile(n_ctx, 512), 512, final=False)

    return xs[None]
```

```python
import functools
import math

import numpy as np
import jax
import jax.numpy as jnp
from jax import lax
from jax.experimental import pallas as pl
from jax.experimental.pallas import tpu as pltpu

F32 = jnp.float32
BF16 = jnp.bfloat16

N_HEADS = 16
N_KV_HEADS = 4
HEAD_DIM = 128
GROUP = N_HEADS // N_KV_HEADS
Q_WIDTH = N_HEADS * HEAD_DIM
KV_WIDTH = N_KV_HEADS * HEAD_DIM
BLOCK = 128
GRID_W = 64
N_FGROUPS = 4
F_DIM = 512
F_WIDTH = N_FGROUPS * F_DIM
N_MOD = 6
ROPE_THETA = 10000.0
EPS = 1e-6
NEG_INF = -1e30
LOG2E = 1.4426950408889634

V7X_VMEM_BYTES = 64 * 1024 * 1024
ADA_ROWS = 8
MLP_OUT_CHUNK = 512
ROW_CHUNK = 16
ROW_UNROLL = 2
DOT_COLS = 256
SOFTMAX_ROWS = 32
HEAD_RESIDUES = 8
TAIL_BLOCKS = 2


def _params(semantics, vmem_bytes):
    limit = min(int(vmem_bytes * 1.15) + (4 << 20), V7X_VMEM_BYTES - (6 << 20))
    return pltpu.CompilerParams(dimension_semantics=semantics, vmem_limit_bytes=limit)


def _dot(a, b):
    return jnp.dot(a, b, preferred_element_type=F32)


def _dot_nt(a, b):
    return lax.dot_general(a, b, (((1,), (1,)), ((), ())), preferred_element_type=F32)


def _rms(x, g):
    return x * lax.rsqrt(jnp.mean(x * x, axis=-1, keepdims=True) + EPS) * g


def _for_row_chunks(n_rows, body):
    rc = min(ROW_CHUNK, n_rows)

    def step(r, carry):
        body(pl.ds(pl.multiple_of(r * rc, rc), rc))
        return carry

    lax.fori_loop(0, n_rows // rc, step, 0, unroll=ROW_UNROLL)


def _normmod_rows(x_ref, gss_ref, ops_ref, h_ref, copy_ref=None):
    ops_ref[0:1, :] = 1.0 + gss_ref[2:3, :]

    def body(rows):
        x = x_ref[rows, :]
        y = _rms(x, gss_ref[0:1, :])
        h_ref[rows, :] = (y * ops_ref[0:1, :] + gss_ref[1:2, :]).astype(h_ref.dtype)
        if copy_ref is not None:
            copy_ref[rows, :] = x

    _for_row_chunks(x_ref.shape[0], body)


def _rms_rows(src_ref, g_ref, dst_ref):
    def body(rows):
        dst_ref[rows, :] = _rms(src_ref[rows, :], g_ref[...]).astype(dst_ref.dtype)

    _for_row_chunks(src_ref.shape[0], body)


def _ada_kernel(cv_ref, a_ref, b_ref, bias_ref, o_ref, h_ref):
    @pl.when(pl.program_id(1) == 0)
    def _():
        cv = cv_ref[...]
        s = cv * jax.nn.sigmoid(cv)
        h_ref[...] = _dot(s.astype(BF16), a_ref[0].astype(BF16))

    o_ref[0] = _dot(h_ref[...].astype(BF16), b_ref[0].astype(BF16)) + bias_ref[0]


def _ada(cv, ada_a, ada_b, ada_bias):
    depth, d, rank = ada_a.shape
    n = ada_b.shape[2]
    tn = d
    vmem = 2 * (d * rank * 4 + rank * tn * 4) + 4 * ADA_ROWS * (d + tn) * 4
    return pl.pallas_call(
        _ada_kernel,
        grid=(depth, n // tn),
        in_specs=[
            pl.BlockSpec((ADA_ROWS, d), lambda l, j: (0, 0)),
            pl.BlockSpec((1, d, rank), lambda l, j: (l, 0, 0)),
            pl.BlockSpec((1, rank, tn), lambda l, j: (l, 0, j)),
            pl.BlockSpec((1, 1, tn), lambda l, j: (l, 0, j)),
        ],
        out_specs=pl.BlockSpec((1, ADA_ROWS, tn), lambda l, j: (l, 0, j)),
        out_shape=jax.ShapeDtypeStruct((depth, ADA_ROWS, n), F32),
        scratch_shapes=[pltpu.VMEM((ADA_ROWS, rank), F32)],
        compiler_params=_params(("parallel", "arbitrary"), vmem),
        name="ada",
    )(cv, ada_a, ada_b, ada_bias.reshape(depth, 1, n))


def _rope_chunk(a, cos, sin, lane):
    up = pltpu.roll(a, HEAD_DIM - 32, axis=1)
    down = pltpu.roll(a, 32, axis=1)
    partner = jnp.where((lane % 64) < 32, up, down)
    return a * cos + partner * sin


def _inproj_kernel(x_ref, gss_ref, w_ref, *rest, rope_cols, tn):
    if rope_cols:
        cos_ref, sin_ref, o_ref, h_ref, ops_ref = rest
    else:
        o_ref, h_ref, ops_ref = rest
    j = pl.program_id(1)

    @pl.when(j == 0)
    def _():
        _normmod_rows(x_ref, gss_ref, ops_ref, h_ref)

    heads_per_dot = DOT_COLS // HEAD_DIM

    def tile(n_rope):
        if n_rope:
            lane = lax.broadcasted_iota(jnp.int32, (x_ref.shape[0], HEAD_DIM), 1)
        for c in range(tn // DOT_COLS):
            acc = _dot(h_ref[...], w_ref[:, c * DOT_COLS:(c + 1) * DOT_COLS])
            for k in range(heads_per_dot):
                head = c * heads_per_dot + k
                a = acc[:, k * HEAD_DIM:(k + 1) * HEAD_DIM]
                if head < n_rope:
                    a = _rope_chunk(a, cos_ref[...], sin_ref[...], lane)
                o_ref[:, head * HEAD_DIM:(head + 1) * HEAD_DIM] = a.astype(o_ref.dtype)

    full_tiles, part = divmod(rope_cols, tn)
    if rope_cols == 0:
        tile(0)
    else:
        pl.when(j < full_tiles)(functools.partial(tile, tn // HEAD_DIM))
        if part:
            pl.when(j == full_tiles)(functools.partial(tile, part // HEAD_DIM))
            pl.when(j > full_tiles)(functools.partial(tile, 0))
        else:
            pl.when(j >= full_tiles)(functools.partial(tile, 0))


def _inproj(x, gss, w_all, layer, rope_tables, tm, tn):
    n_rows, d = x.shape
    n_out = w_all.shape[2]
    rope_cols = Q_WIDTH + KV_WIDTH if rope_tables is not None else 0
    in_specs = [
        pl.BlockSpec((tm, d), lambda i, j: (i, 0)),
        pl.BlockSpec((8, d), lambda i, j: (0, 0)),
        pl.BlockSpec((None, d, tn), lambda i, j: (layer, 0, j)),
    ]
    args = [x, gss, w_all]
    if rope_tables is not None:
        in_specs += [pl.BlockSpec((tm, HEAD_DIM), lambda i, j: (i, 0))] * 2
        args += list(rope_tables)
    vmem = (2 * (tm * d * 4 + d * tn * 2 + tm * tn * 2 + 2 * tm * HEAD_DIM * 4)
            + tm * d * 2 + 4 * tm * DOT_COLS * 4)
    return pl.pallas_call(
        functools.partial(_inproj_kernel, rope_cols=rope_cols, tn=tn),
        grid=(n_rows // tm, n_out // tn),
        in_specs=in_specs,
        out_specs=pl.BlockSpec((tm, tn), lambda i, j: (i, j)),
        out_shape=jax.ShapeDtypeStruct((n_rows, n_out), BF16),
        scratch_shapes=[pltpu.VMEM((tm, d), BF16), pltpu.VMEM((8, d), F32)],
        compiler_params=_params(("parallel", "arbitrary"), vmem),
        name="inproj_rope" if rope_cols else "inproj_ctx",
    )(*args)


def _stack_heads(q_ref, h):
    return jnp.concatenate(
        [q_ref[:, (h * GROUP + g) * HEAD_DIM:(h * GROUP + g + 1) * HEAD_DIM] for g in range(GROUP)], axis=0)


def _softmax_chunk(s_ref, p_ref, rows, sink, masks):
    n_tiles = s_ref.shape[1] // BLOCK
    tiles = []
    for t in range(n_tiles):
        v = s_ref[rows, t * BLOCK:(t + 1) * BLOCK]
        if masks.get(t) is not None:
            v = jnp.where(masks[t], v, NEG_INF)
        tiles.append(v)
    m_raw = jnp.max(functools.reduce(jnp.maximum, tiles), axis=-1, keepdims=True)
    m = jnp.maximum(m_raw * (HEAD_DIM ** -0.5), sink)
    mb = m * LOG2E
    es = [jnp.exp2(v * (HEAD_DIM ** -0.5 * LOG2E) - mb) for v in tiles]
    denom = jnp.sum(functools.reduce(jnp.add, es), axis=-1, keepdims=True) + jnp.exp2(sink * LOG2E - mb)
    inv = 1.0 / denom
    for t in range(n_tiles):
        p_ref[rows, t * BLOCK:(t + 1) * BLOCK] = (es[t] * inv).astype(p_ref.dtype)


def _attn_heads(sink_ref, q_ref, keys, values, acc_ref, s_ref, p_ref, mask_fn):
    n_q = q_ref.shape[0]
    chunks_per_head = n_q // SOFTMAX_ROWS

    def scores(h):
        s_ref[h % 2] = _dot_nt(_stack_heads(q_ref, h), keys(h))

    scores(0)
    for h in range(N_KV_HEADS):
        if h + 1 < N_KV_HEADS:
            scores(h + 1)
        for r in range(GROUP * chunks_per_head):
            rows = slice(r * SOFTMAX_ROWS, (r + 1) * SOFTMAX_ROWS)
            sink = sink_ref[h * GROUP + r // chunks_per_head]
            q0 = (r % chunks_per_head) * SOFTMAX_ROWS
            _softmax_chunk(s_ref.at[h % 2], p_ref.at[h % 2], rows, sink, mask_fn(q0))
        o = _dot(p_ref[h % 2], values(h))
        for g in range(GROUP):
            c0 = (h * GROUP + g) * HEAD_DIM
            acc_ref[:, c0:c0 + HEAD_DIM] = o[g * n_q:(g + 1) * n_q, :]


def _latent_attn_kernel(sink_ref, q_ref, kp_ref, kc_ref, kn_ref, vp_ref, vc_ref, vn_ref,
                        kx_ref, vx_ref, g_ref, o_ref, acc_ref, s_ref, p_ref, *, n_ctx):
    n = pl.program_id(0)
    nb = pl.num_programs(0)
    ctx_tiles = n_ctx // BLOCK
    prev_lo = jnp.where(n == 0, BLOCK, 0)
    next_hi = jnp.where(n == nb - 1, 0, BLOCK)

    def mask_fn(q0):
        qi = lax.broadcasted_iota(jnp.int32, (SOFTMAX_ROWS, BLOCK), 0) + q0
        kj = lax.broadcasted_iota(jnp.int32, (SOFTMAX_ROWS, BLOCK), 1)
        return {ctx_tiles: (kj >= qi) & (kj >= prev_lo), ctx_tiles + 2: (kj <= qi) & (kj < next_hi)}

    def keys(h):
        hs = slice(h * HEAD_DIM, (h + 1) * HEAD_DIM)
        return jnp.concatenate([kx_ref[:, hs], kp_ref[:, hs], kc_ref[:, hs], kn_ref[:, hs]], axis=0)

    def values(h):
        hs = slice(h * HEAD_DIM, (h + 1) * HEAD_DIM)
        return jnp.concatenate([vx_ref[:, hs], vp_ref[:, hs], vc_ref[:, hs], vn_ref[:, hs]], axis=0)

    _attn_heads(sink_ref, q_ref, keys, values, acc_ref, s_ref, p_ref, mask_fn)
    _rms_rows(acc_ref, g_ref, o_ref)


def _latent_attn(px, pc, sink, g_attn):
    n_rows = px.shape[0]
    n_ctx = pc.shape[0]
    nb = n_rows // BLOCK
    kcol = Q_WIDTH // KV_WIDTH
    vcol = kcol + 1
    n_keys = n_ctx + 3 * BLOCK

    def kv_spec(colblk, shift):
        return pl.BlockSpec(
            (BLOCK, KV_WIDTH), lambda n: (jnp.clip(n + shift, 0, nb - 1), colblk))

    in_specs = [
        pl.BlockSpec(memory_space=pltpu.SMEM),
        pl.BlockSpec((BLOCK, Q_WIDTH), lambda n: (n, 0)),
        kv_spec(kcol, -1), kv_spec(kcol, 0), kv_spec(kcol, 1),
        kv_spec(vcol, -1), kv_spec(vcol, 0), kv_spec(vcol, 1),
        pl.BlockSpec((n_ctx, KV_WIDTH), lambda n: (0, kcol)),
        pl.BlockSpec((n_ctx, KV_WIDTH), lambda n: (0, vcol)),
        pl.BlockSpec((1, Q_WIDTH), lambda n: (0, 0)),
    ]
    return pl.pallas_call(
        functools.partial(_latent_attn_kernel, n_ctx=n_ctx),
        grid=(nb,),
        in_specs=in_specs,
        out_specs=pl.BlockSpec((BLOCK, Q_WIDTH), lambda n: (n, 0)),
        out_shape=jax.ShapeDtypeStruct((n_rows, Q_WIDTH), BF16),
        scratch_shapes=[
            pltpu.VMEM((BLOCK, Q_WIDTH), F32),
            pltpu.VMEM((2, GROUP * BLOCK, n_keys), F32),
            pltpu.VMEM((2, GROUP * BLOCK, n_keys), BF16),
        ],
        compiler_params=_params(("parallel",), 16 << 20),
        name="latent_attn",
    )(sink, px, px, px, px, px, px, px, pc, pc, g_attn.reshape(1, Q_WIDTH))


def _ctx_attn_kernel(sink_ref, q_ref, k_ref, v_ref, g_ref, o_ref, acc_ref, s_ref, p_ref):
    def keys(h):
        return k_ref[:, h * HEAD_DIM:(h + 1) * HEAD_DIM]

    def values(h):
        return v_ref[:, h * HEAD_DIM:(h + 1) * HEAD_DIM]

    _attn_heads(sink_ref, q_ref, keys, values, acc_ref, s_ref, p_ref, lambda q0: {})
    _rms_rows(acc_ref, g_ref, o_ref)


def _ctx_attn(pc, sink, g_attn):
    n_ctx = pc.shape[0]
    kcol = Q_WIDTH // KV_WIDTH
    return pl.pallas_call(
        _ctx_attn_kernel,
        grid=(1,),
        in_specs=[
            pl.BlockSpec(memory_space=pltpu.SMEM),
            pl.BlockSpec((n_ctx, Q_WIDTH), lambda i: (0, 0)),
            pl.BlockSpec((n_ctx, KV_WIDTH), lambda i: (0, kcol)),
            pl.BlockSpec((n_ctx, KV_WIDTH), lambda i: (0, kcol + 1)),
            pl.BlockSpec((1, Q_WIDTH), lambda i: (0, 0)),
        ],
        out_specs=pl.BlockSpec((n_ctx, Q_WIDTH), lambda i: (0, 0)),
        out_shape=jax.ShapeDtypeStruct((n_ctx, Q_WIDTH), BF16),
        scratch_shapes=[
            pltpu.VMEM((n_ctx, Q_WIDTH), F32),
            pltpu.VMEM((2, GROUP * n_ctx, n_ctx), F32),
            pltpu.VMEM((2, GROUP * n_ctx, n_ctx), BF16),
        ],
        compiler_params=_params(("arbitrary",), 16 << 20),
        name="ctx_attn",
    )(sink, pc, pc, pc, g_attn.reshape(1, Q_WIDTH))


def _dft_tables(n):
    ang = 2.0 * np.pi * (np.outer(np.arange(n), np.arange(n)) % n) / n
    return np.cos(ang), np.sin(ang)


def _fourier_head_kernel(g_ref, z_ref, o_ref):
    for r in range(z_ref.shape[0]):
        o_ref[r] = _dot(g_ref[r], z_ref[r]).astype(o_ref.dtype)


def _fourier_head(zt, gmat):
    nb, na, c = zt.shape
    rb = min(HEAD_RESIDUES, nb)
    return pl.pallas_call(
        _fourier_head_kernel,
        grid=(nb // rb,),
        in_specs=[
            pl.BlockSpec((rb, 2 * na, na), lambda b: (b, 0, 0)),
            pl.BlockSpec((rb, na, c), lambda b: (b, 0, 0)),
        ],
        out_specs=pl.BlockSpec((rb, 2 * na, c), lambda b: (b, 0, 0)),
        out_shape=jax.ShapeDtypeStruct((nb, 2 * na, c), BF16),
        compiler_params=_params(("parallel",), 6 * rb * na * c * 2),
        name="fourier_head",
    )(gmat, zt)


def _fourier_tail_kernel(m_ref, d_ref, cs_ref, wf_ref, g_ref, o_ref, acc_ref, *, pos_scale):
    p = o_ref.shape[1]
    for blk in range(o_ref.shape[0]):
        x = (_dot(m_ref[...], d_ref[blk]) * pos_scale).astype(BF16)
        for g in range(N_FGROUPS):
            gs = slice(g * F_DIM, (g + 1) * F_DIM)
            lhs = jnp.concatenate([x[:p, gs], x[p:, gs]], axis=1)
            f = _dot(lhs, cs_ref[...]) * (F_DIM ** -0.5)
            acc_ref[blk, :, gs] = _dot(f.astype(BF16), wf_ref[g])
    for blk in range(o_ref.shape[0]):
        _rms_rows(acc_ref.at[blk], g_ref, o_ref.at[blk])


def _fourier_tail(stage_mat, data, cs, wf_all, layer, g_four, pos_scale):
    nblk, k_in, c = data.shape
    p = stage_mat.shape[0] // 2
    kb = min(TAIL_BLOCKS, nblk)
    vmem = 2 * kb * (k_in + p) * c * 2 + kb * p * c * 4 + 2 * (2 * p * k_in + 6 * F_DIM * F_DIM) * 2 + (8 << 20)
    return pl.pallas_call(
        functools.partial(_fourier_tail_kernel, pos_scale=pos_scale),
        grid=(nblk // kb,),
        in_specs=[
            pl.BlockSpec((2 * p, k_in), lambda i: (0, 0)),
            pl.BlockSpec((kb, k_in, c), lambda i: (i, 0, 0)),
            pl.BlockSpec((2 * F_DIM, F_DIM), lambda i: (0, 0)),
            pl.BlockSpec((None, N_FGROUPS, F_DIM, F_DIM), lambda i: (layer, 0, 0, 0)),
            pl.BlockSpec((1, c), lambda i: (0, 0)),
        ],
        out_specs=pl.BlockSpec((kb, p, c), lambda i: (i, 0, 0)),
        out_shape=jax.ShapeDtypeStruct((nblk, p, c), BF16),
        scratch_shapes=[pltpu.VMEM((kb, p, c), F32)],
        compiler_params=_params(("parallel",), vmem),
        name="fourier_tail",
    )(stage_mat, data, cs, wf_all, g_four.reshape(1, c))


class _FourierConsts:
    def __init__(self, n_lat, n_ctx):
        a = b = int(round(math.sqrt(n_lat)))
        assert a * b == n_lat
        self.a, self.b = a, b
        k_lo = np.arange(a)[None, :, None]
        n = (np.arange(a)[None, None, :] * b + np.arange(b)[:, None, None])
        ang = 2.0 * np.pi * ((k_lo * n) % n_lat) / n_lat
        self.head = jnp.asarray(np.concatenate([np.cos(ang), -np.sin(ang)], axis=1), BF16)
        cb, sb = _dft_tables(b)
        self.tail = jnp.asarray(np.block([[cb, sb], [-sb, cb]]), BF16)
        cc, sc = _dft_tables(n_ctx)
        self.ctx = jnp.asarray(np.concatenate([cc, -sc], axis=0), BF16)
        cf, sf = _dft_tables(F_DIM)
        self.chan = jnp.asarray(np.concatenate([cf, sf], axis=0), BF16)


def _latent_fourier(px, fc, wf_all, layer, g_four):
    n_rows = px.shape[0]
    a, b = fc.a, fc.b
    f_lo = Q_WIDTH + 2 * KV_WIDTH
    z = px[:, f_lo:].reshape(a, b, F_WIDTH)
    zt = jnp.transpose(z, (1, 0, 2))
    t = _fourier_head(zt, fc.head)
    t = jnp.transpose(t.reshape(b, 2, a, F_WIDTH), (2, 1, 0, 3)).reshape(a, 2 * b, F_WIDTH)
    o = _fourier_tail(fc.tail, t, fc.chan, wf_all, layer, g_four, 1.0 / math.sqrt(n_rows))
    return jnp.transpose(o, (1, 0, 2)).reshape(n_rows, F_WIDTH)


def _ctx_fourier(pc, fc, wf_all, layer, g_four):
    n_ctx = pc.shape[0]
    f_lo = Q_WIDTH + 2 * KV_WIDTH
    d = pc[:, f_lo:].reshape(1, n_ctx, F_WIDTH)
    o = _fourier_tail(fc.ctx, d, fc.chan, wf_all, layer, g_four, 1.0 / math.sqrt(n_ctx))
    return o.reshape(n_ctx, F_WIDTH)


def _outproj_kernel(na_ref, nf_ref, w_ref, x_ref, gate_ref, o_ref):
    ka = na_ref.shape[1]
    acc = _dot(na_ref[...], w_ref[:ka, :]) + _dot(nf_ref[...], w_ref[ka:, :])
    o_ref[...] = x_ref[...] + gate_ref[...] * acc


def _outproj(na, nf, w_all, layer, x, gate, tm, tn):
    n_rows, d = x.shape
    ka, kf = na.shape[1], nf.shape[1]
    vmem = 2 * (tm * (ka + kf) * 2 + (ka + kf) * tn * 2 + 2 * tm * tn * 4) + tm * tn * 4
    return pl.pallas_call(
        _outproj_kernel,
        grid=(n_rows // tm, d // tn),
        in_specs=[
            pl.BlockSpec((tm, ka), lambda i, j: (i, 0)),
            pl.BlockSpec((tm, kf), lambda i, j: (i, 0)),
            pl.BlockSpec((None, ka + kf, tn), lambda i, j: (layer, 0, j)),
            pl.BlockSpec((tm, tn), lambda i, j: (i, j)),
            pl.BlockSpec((1, tn), lambda i, j: (0, j)),
        ],
        out_specs=pl.BlockSpec((tm, tn), lambda i, j: (i, j)),
        out_shape=jax.ShapeDtypeStruct((n_rows, d), F32),
        compiler_params=_params(("parallel", "parallel"), vmem),
        name="outproj",
    )(na, nf, w_all, x, gate.reshape(1, d))


def _mlp_kernel(x_ref, gss_ref, w1_ref, w2_ref, o_ref, h_ref, ops_ref, *, final):
    f = pl.program_id(1)

    @pl.when(f == 0)
    def _():
        _normmod_rows(x_ref, gss_ref, ops_ref, h_ref, copy_ref=o_ref)

    u = jnp.maximum(_dot(h_ref[...], w1_ref[...]), 0.0)
    u = (u * u).astype(BF16)
    tn = MLP_OUT_CHUNK
    for c in range(o_ref.shape[1] // tn):
        cs = slice(c * tn, (c + 1) * tn)
        o_ref[:, cs] += gss_ref[3:4, cs] * _dot(u, w2_ref[:, cs])

    if final:
        @pl.when(f == pl.num_programs(1) - 1)
        def _():
            _rms_rows(o_ref, gss_ref.at[4:5, :], o_ref)


def _mlp(x, gss, w1_all, w2_all, layer, tm, tf, final):
    n_rows, d = x.shape
    d_ff = w1_all.shape[2]
    vmem = 2 * (2 * tm * d * 4 + 2 * d * tf * 2) + tm * d * 2 + tm * tf * 6 + 2 * tm * MLP_OUT_CHUNK * 4
    return pl.pallas_call(
        functools.partial(_mlp_kernel, final=final),
        grid=(n_rows // tm, d_ff // tf),
        in_specs=[
            pl.BlockSpec((tm, d), lambda i, f: (i, 0)),
            pl.BlockSpec((8, d), lambda i, f: (0, 0)),
            pl.BlockSpec((None, d, tf), lambda i, f: (layer, 0, f)),
            pl.BlockSpec((None, tf, d), lambda i, f: (layer, f, 0)),
        ],
        out_specs=pl.BlockSpec((tm, d), lambda i, f: (i, 0)),
        out_shape=jax.ShapeDtypeStruct((n_rows, d), F32),
        scratch_shapes=[pltpu.VMEM((tm, d), BF16), pltpu.VMEM((8, d), F32)],
        compiler_params=_params(("parallel", "arbitrary"), vmem),
        name="mlp_final" if final else "mlp",
    )(x, gss, w1_all, w2_all)


def _rope_tables(n_rows):
    quarter = HEAD_DIM // 4
    inv = ROPE_THETA ** (-jnp.arange(quarter, dtype=F32) / quarter)
    pos = jnp.arange(n_rows)
    rows = (pos // GRID_W).astype(F32)
    cols = (pos % GRID_W).astype(F32)
    ang = jnp.concatenate([rows[:, None] * inv[None, :]] * 2 + [cols[:, None] * inv[None, :]] * 2, axis=1)
    sign = jnp.tile(jnp.concatenate([-jnp.ones(quarter, F32), jnp.ones(quarter, F32)]), 2)
    return jnp.cos(ang), jnp.sin(ang) * sign[None, :]


def _rows8(*vecs):
    d = vecs[0].shape[0]
    pad = [jnp.zeros((d,), F32)] * (8 - len(vecs))
    return jnp.stack(list(vecs) + pad, axis=0)


def _row_tile(n_rows, want):
    return min(want, n_rows)


def kernel(x, c, ctx, c_ctx, ada_a, ada_b, ada_bias, g_mix, w_in, sink, w_f, g_attn_out, g_four_out,
           w_out, g_mlp, w1, w2, g_final):
    assert x.shape[0] == 1 and ctx.shape[0] == 1
    depth = w_in.shape[0]
    d = x.shape[2]
    xs = x[0]
    cs = ctx[0]
    n_lat, n_ctx = xs.shape[0], cs.shape[0]

    cv = jnp.zeros((ADA_ROWS, d), F32).at[0].set(c[0]).at[1].set(c_ctx)
    mods = _ada(cv, ada_a, ada_b, ada_bias)
    rope = _rope_tables(n_lat)
    fconst = _FourierConsts(n_lat, n_ctx)
    w_in_b = w_in.astype(BF16)
    w_out_b = w_out.astype(BF16)
    w1_b = w1.astype(BF16)
    w2_b = w2.astype(BF16)
    wf_b = w_f.astype(BF16)

    for l in range(depth):
        last = l == depth - 1
        m_lat = [mods[l, 0, i * d:(i + 1) * d] for i in range(N_MOD)]
        m_ctx = [mods[l, 1, i * d:(i + 1) * d] for i in range(N_MOD)]

        pc = _inproj(cs, _rows8(g_mix[l], m_ctx[0], m_ctx[1]), w_in_b, l, None, _row_tile(n_ctx, 512), 1024)
        px = _inproj(xs, _rows8(g_mix[l], m_lat[0], m_lat[1]), w_in_b, l, rope, _row_tile(n_lat, 512), 1024)

        na = _latent_attn(px, pc, sink[l], g_attn_out[l])
        nf = _latent_fourier(px, fconst, wf_b, l, g_four_out[l])
        x_mid = _outproj(na, nf, w_out_b, l, xs, m_lat[2], _row_tile(n_lat, 1024), 1024)
        gss = _rows8(g_mlp[l], m_lat[3], m_lat[4], m_lat[5], g_final)
        xs = _mlp(x_mid, gss, w1_b, w2_b, l, _row_tile(n_lat, 512), 512, final=last)

        if not last:
            nac = _ctx_attn(pc, sink[l], g_attn_out[l])
            nfc = _ctx_fourier(pc, fconst, wf_b, l, g_four_out[l])
            c_mid = _outproj(nac, nfc, w_out_b, l, cs, m_ctx[2], _row_tile(n_ctx, 1024), 1024)
            gss_c = _rows8(g_mlp[l], m_ctx[3], m_ctx[4], m_ctx[5])
            cs = _mlp(c_mid, gss_c, w1_b, w2_b, l, _row_tile(n_ctx, 512), 512, final=False)

    return xs[None]
```

```python
import functools
import math

import numpy as np
import jax
import jax.numpy as jnp
from jax import lax
from jax.experimental import pallas as pl
from jax.experimental.pallas import tpu as pltpu

F32 = jnp.float32
BF16 = jnp.bfloat16

N_HEADS = 16
N_KV_HEADS = 4
HEAD_DIM = 128
GROUP = N_HEADS // N_KV_HEADS
Q_WIDTH = N_HEADS * HEAD_DIM
KV_WIDTH = N_KV_HEADS * HEAD_DIM
BLOCK = 128
GRID_W = 64
N_FGROUPS = 4
F_DIM = 512
F_WIDTH = N_FGROUPS * F_DIM
N_MOD = 6
ROPE_THETA = 10000.0
EPS = 1e-6
NEG_INF = -1e30
LOG2E = 1.4426950408889634

V7X_VMEM_BYTES = 64 * 1024 * 1024
ADA_ROWS = 8
MLP_OUT_CHUNK = 512
ROW_CHUNK = 16
ROW_UNROLL = 2
DOT_COLS = 256
SOFTMAX_ROWS = 32
ATTN_Q_BLOCKS = 2
HEAD_RESIDUES = 8
TAIL_BLOCKS = 4


def _params(semantics, vmem_bytes):
    limit = min(int(vmem_bytes * 1.15) + (4 << 20), V7X_VMEM_BYTES - (6 << 20))
    return pltpu.CompilerParams(dimension_semantics=semantics, vmem_limit_bytes=limit)


def _dot(a, b):
    return jnp.dot(a, b, preferred_element_type=F32)


def _dot_nt(a, b):
    return lax.dot_general(a, b, (((1,), (1,)), ((), ())), preferred_element_type=F32)


def _rms(x, g):
    return x * lax.rsqrt(jnp.mean(x * x, axis=-1, keepdims=True) + EPS) * g


def _for_row_chunks(n_rows, body):
    rc = min(ROW_CHUNK, n_rows)

    def step(r, carry):
        body(pl.ds(pl.multiple_of(r * rc, rc), rc))
        return carry

    lax.fori_loop(0, n_rows // rc, step, 0, unroll=ROW_UNROLL)


def _normmod_rows(x_ref, gss_ref, ops_ref, h_ref, copy_ref=None):
    ops_ref[0:1, :] = 1.0 + gss_ref[2:3, :]

    def body(rows):
        x = x_ref[rows, :]
        y = _rms(x, gss_ref[0:1, :])
        h_ref[rows, :] = (y * ops_ref[0:1, :] + gss_ref[1:2, :]).astype(h_ref.dtype)
        if copy_ref is not None:
            copy_ref[rows, :] = x

    _for_row_chunks(x_ref.shape[0], body)


def _rms_rows(src_ref, g_ref, dst_ref):
    def body(rows):
        dst_ref[rows, :] = _rms(src_ref[rows, :], g_ref[...]).astype(dst_ref.dtype)

    _for_row_chunks(src_ref.shape[0], body)


def _ada_kernel(cv_ref, a_ref, b_ref, bias_ref, o_ref, h_ref):
    @pl.when(pl.program_id(1) == 0)
    def _():
        cv = cv_ref[...]
        s = cv * jax.nn.sigmoid(cv)
        h_ref[...] = _dot(s.astype(BF16), a_ref[0].astype(BF16))

    o_ref[0] = _dot(h_ref[...].astype(BF16), b_ref[0].astype(BF16)) + bias_ref[0]


def _ada(cv, ada_a, ada_b, ada_bias):
    depth, d, rank = ada_a.shape
    n = ada_b.shape[2]
    tn = d
    vmem = 2 * (d * rank * 4 + rank * tn * 4) + 4 * ADA_ROWS * (d + tn) * 4
    return pl.pallas_call(
        _ada_kernel,
        grid=(depth, n // tn),
        in_specs=[
            pl.BlockSpec((ADA_ROWS, d), lambda l, j: (0, 0)),
            pl.BlockSpec((1, d, rank), lambda l, j: (l, 0, 0)),
            pl.BlockSpec((1, rank, tn), lambda l, j: (l, 0, j)),
            pl.BlockSpec((1, 1, tn), lambda l, j: (l, 0, j)),
        ],
        out_specs=pl.BlockSpec((1, ADA_ROWS, tn), lambda l, j: (l, 0, j)),
        out_shape=jax.ShapeDtypeStruct((depth, ADA_ROWS, n), F32),
        scratch_shapes=[pltpu.VMEM((ADA_ROWS, rank), F32)],
        compiler_params=_params(("parallel", "arbitrary"), vmem),
        name="ada",
    )(cv, ada_a, ada_b, ada_bias.reshape(depth, 1, n))


def _rope_chunk(a, cos, sin, lane):
    up = pltpu.roll(a, HEAD_DIM - 32, axis=1)
    down = pltpu.roll(a, 32, axis=1)
    partner = jnp.where((lane % 64) < 32, up, down)
    return a * cos + partner * sin


def _inproj_kernel(x_ref, gss_ref, w_ref, *rest, rope_cols, tn):
    if rope_cols:
        cos_ref, sin_ref, qkv_ref, f_ref, h_ref, ops_ref = rest
    else:
        qkv_ref, f_ref, h_ref, ops_ref = rest
    j = pl.program_id(1)

    @pl.when(j == 0)
    def _():
        _normmod_rows(x_ref, gss_ref, ops_ref, h_ref)

    heads_per_dot = DOT_COLS // HEAD_DIM

    def tile(n_rope, o_ref):
        if n_rope:
            lane = lax.broadcasted_iota(jnp.int32, (x_ref.shape[0], HEAD_DIM), 1)
        for c in range(tn // DOT_COLS):
            acc = _dot(h_ref[...], w_ref[:, c * DOT_COLS:(c + 1) * DOT_COLS])
            for k in range(heads_per_dot):
                head = c * heads_per_dot + k
                a = acc[:, k * HEAD_DIM:(k + 1) * HEAD_DIM]
                if head < n_rope:
                    a = _rope_chunk(a, cos_ref[...], sin_ref[...], lane)
                o_ref[:, head * HEAD_DIM:(head + 1) * HEAD_DIM] = a.astype(o_ref.dtype)

    qkv_tiles = (Q_WIDTH + 2 * KV_WIDTH) // tn
    full_tiles, part = divmod(rope_cols, tn)
    if full_tiles:
        pl.when(j < full_tiles)(functools.partial(tile, tn // HEAD_DIM, qkv_ref))
    if part:
        pl.when(j == full_tiles)(functools.partial(tile, part // HEAD_DIM, qkv_ref))
        full_tiles += 1
    if full_tiles < qkv_tiles:
        pl.when((j >= full_tiles) & (j < qkv_tiles))(functools.partial(tile, 0, qkv_ref))
    pl.when(j >= qkv_tiles)(functools.partial(tile, 0, f_ref))


def _inproj(x, gss, w_all, layer, rope_tables, tm, tn):
    n_rows, d = x.shape
    n_out = w_all.shape[2]
    qkv_cols = Q_WIDTH + 2 * KV_WIDTH
    qkv_tiles = qkv_cols // tn
    rope_cols = Q_WIDTH + KV_WIDTH if rope_tables is not None else 0
    in_specs = [
        pl.BlockSpec((tm, d), lambda i, j: (i, 0)),
        pl.BlockSpec((8, d), lambda i, j: (0, 0)),
        pl.BlockSpec((None, d, tn), lambda i, j: (layer, 0, j)),
    ]
    args = [x, gss, w_all]
    if rope_tables is not None:
        in_specs += [pl.BlockSpec((tm, HEAD_DIM), lambda i, j: (i, 0))] * 2
        args += list(rope_tables)
    vmem = (2 * (tm * d * 4 + d * tn * 2 + 2 * tm * tn * 2 + 2 * tm * HEAD_DIM * 4)
            + tm * d * 2 + 4 * tm * DOT_COLS * 4)
    cost = pl.CostEstimate(flops=2 * n_rows * d * n_out, transcendentals=n_rows,
                           bytes_accessed=n_rows * d * 4 + (n_rows // tm) * d * n_out * 2 + n_rows * n_out * 2)
    return pl.pallas_call(
        functools.partial(_inproj_kernel, rope_cols=rope_cols, tn=tn),
        grid=(n_rows // tm, n_out // tn),
        in_specs=in_specs,
        out_specs=[
            pl.BlockSpec((tm, tn), lambda i, j: (i, jnp.minimum(j, qkv_tiles - 1))),
            pl.BlockSpec((tm, tn), lambda i, j: (i, jnp.maximum(j - qkv_tiles, 0))),
        ],
        out_shape=[
            jax.ShapeDtypeStruct((n_rows, qkv_cols), BF16),
            jax.ShapeDtypeStruct((n_rows, n_out - qkv_cols), BF16),
        ],
        scratch_shapes=[pltpu.VMEM((tm, d), BF16), pltpu.VMEM((8, d), F32)],
        compiler_params=_params(("parallel", "arbitrary"), vmem),
        cost_estimate=cost,
        name="inproj_rope" if rope_cols else "inproj_ctx",
    )(*args)


def _stack_heads(q_ref, rows, h):
    return jnp.concatenate(
        [q_ref[rows, (h * GROUP + g) * HEAD_DIM:(h * GROUP + g + 1) * HEAD_DIM] for g in range(GROUP)], axis=0)


def _softmax_chunk(s_ref, p_ref, rows, sink, masks):
    n_tiles = s_ref.shape[1] // BLOCK
    tiles = []
    for t in range(n_tiles):
        v = s_ref[rows, t * BLOCK:(t + 1) * BLOCK]
        if masks.get(t) is not None:
            v = jnp.where(masks[t], v, NEG_INF)
        tiles.append(v)
    m_raw = jnp.max(functools.reduce(jnp.maximum, tiles), axis=-1, keepdims=True)
    m = jnp.maximum(m_raw * (HEAD_DIM ** -0.5), sink)
    mb = m * LOG2E
    es = [jnp.exp2(v * (HEAD_DIM ** -0.5 * LOG2E) - mb) for v in tiles]
    denom = jnp.sum(functools.reduce(jnp.add, es), axis=-1, keepdims=True) + jnp.exp2(sink * LOG2E - mb)
    inv = 1.0 / denom
    for t in range(n_tiles):
        p_ref[rows, t * BLOCK:(t + 1) * BLOCK] = (es[t] * inv).astype(p_ref.dtype)


def _attn_units(sink_ref, q_ref, units, acc_ref, s_ref, p_ref):
    def scores(u):
        q_rows, h, keys, _, _ = units[u]
        s_ref[u % 2] = _dot_nt(_stack_heads(q_ref, q_rows, h), keys())

    scores(0)
    for u, (q_rows, h, _, values, mask_fn) in enumerate(units):
        if u + 1 < len(units):
            scores(u + 1)
        n_q = q_rows.stop - q_rows.start
        chunks_per_head = n_q // SOFTMAX_ROWS
        for r in range(GROUP * chunks_per_head):
            rows = slice(r * SOFTMAX_ROWS, (r + 1) * SOFTMAX_ROWS)
            sink = sink_ref[h * GROUP + r // chunks_per_head]
            q0 = (r % chunks_per_head) * SOFTMAX_ROWS
            _softmax_chunk(s_ref.at[u % 2], p_ref.at[u % 2], rows, sink, mask_fn(q0))
        o = _dot(p_ref[u % 2], values())
        for g in range(GROUP):
            c0 = (h * GROUP + g) * HEAD_DIM
            acc_ref[q_rows, c0:c0 + HEAD_DIM] = o[g * n_q:(g + 1) * n_q, :]


def _latent_attn_kernel(sink_ref, q_ref, kp_ref, kc_ref, kn_ref, vp_ref, vc_ref, vn_ref,
                        kx_ref, vx_ref, g_ref, o_ref, acc_ref, s_ref, p_ref, *, n_ctx, q_blocks):
    n = pl.program_id(0)
    nb = pl.num_programs(0)
    ctx_tiles = n_ctx // BLOCK
    prev_lo = jnp.where(n == 0, BLOCK, 0)
    next_hi = jnp.where(n == nb - 1, 0, BLOCK)

    def block_of(refs, i):
        prev_ref, cur_ref, next_ref = refs
        if i == 0:
            return prev_ref, slice(0, BLOCK)
        if i == q_blocks + 1:
            return next_ref, slice(0, BLOCK)
        return cur_ref, slice((i - 1) * BLOCK, i * BLOCK)

    def operand(ctx_ref, refs, qb, h):
        hs = slice(h * HEAD_DIM, (h + 1) * HEAD_DIM)
        parts = [ctx_ref[:, hs]]
        for i in range(qb, qb + 3):
            ref, rows = block_of(refs, i)
            parts.append(ref[rows, hs])
        return jnp.concatenate(parts, axis=0)

    def mask_fn(qb, q0):
        qi = lax.broadcasted_iota(jnp.int32, (SOFTMAX_ROWS, BLOCK), 0) + q0
        kj = lax.broadcasted_iota(jnp.int32, (SOFTMAX_ROWS, BLOCK), 1)
        lo = kj >= qi
        hi = kj <= qi
        if qb == 0:
            lo = lo & (kj >= prev_lo)
        if qb == q_blocks - 1:
            hi = hi & (kj < next_hi)
        return {ctx_tiles: lo, ctx_tiles + 2: hi}

    units = []
    for qb in range(q_blocks):
        for h in range(N_KV_HEADS):
            units.append((
                slice(qb * BLOCK, (qb + 1) * BLOCK), h,
                functools.partial(operand, kx_ref, (kp_ref, kc_ref, kn_ref), qb, h),
                functools.partial(operand, vx_ref, (vp_ref, vc_ref, vn_ref), qb, h),
                functools.partial(mask_fn, qb),
            ))
    _attn_units(sink_ref, q_ref, units, acc_ref, s_ref, p_ref)
    _rms_rows(acc_ref, g_ref, o_ref)


def _latent_attn(px, pc, sink, g_attn):
    n_rows = px.shape[0]
    n_ctx = pc.shape[0]
    nb = n_rows // BLOCK
    qb = ATTN_Q_BLOCKS if nb % ATTN_Q_BLOCKS == 0 else 1
    kcol = Q_WIDTH // KV_WIDTH
    vcol = kcol + 1
    n_keys = n_ctx + 3 * BLOCK

    def kv_specs(colblk):
        return [
            pl.BlockSpec((BLOCK, KV_WIDTH), lambda n: (jnp.maximum(n * qb - 1, 0), colblk)),
            pl.BlockSpec((qb * BLOCK, KV_WIDTH), lambda n: (n, colblk)),
            pl.BlockSpec((BLOCK, KV_WIDTH), lambda n: (jnp.minimum(n * qb + qb, nb - 1), colblk)),
        ]

    in_specs = [
        pl.BlockSpec(memory_space=pltpu.SMEM),
        pl.BlockSpec((qb * BLOCK, Q_WIDTH), lambda n: (n, 0)),
        *kv_specs(kcol), *kv_specs(vcol),
        pl.BlockSpec((n_ctx, KV_WIDTH), lambda n: (0, kcol)),
        pl.BlockSpec((n_ctx, KV_WIDTH), lambda n: (0, vcol)),
        pl.BlockSpec((1, Q_WIDTH), lambda n: (0, 0)),
    ]
    return pl.pallas_call(
        functools.partial(_latent_attn_kernel, n_ctx=n_ctx, q_blocks=qb),
        grid=(nb // qb,),
        in_specs=in_specs,
        out_specs=pl.BlockSpec((qb * BLOCK, Q_WIDTH), lambda n: (n, 0)),
        out_shape=jax.ShapeDtypeStruct((n_rows, Q_WIDTH), BF16),
        scratch_shapes=[
            pltpu.VMEM((qb * BLOCK, Q_WIDTH), F32),
            pltpu.VMEM((2, GROUP * BLOCK, n_keys), F32),
            pltpu.VMEM((2, GROUP * BLOCK, n_keys), BF16),
        ],
        compiler_params=_params(("parallel",), 24 << 20),
        cost_estimate=pl.CostEstimate(
            flops=4 * n_rows * N_HEADS * n_keys * HEAD_DIM, transcendentals=n_rows * N_HEADS * n_keys,
            bytes_accessed=n_rows * (2 * Q_WIDTH + 6 * KV_WIDTH) * 2),
        name="latent_attn",
    )(sink, px, px, px, px, px, px, px, pc, pc, g_attn.reshape(1, Q_WIDTH))


def _ctx_attn_kernel(sink_ref, q_ref, k_ref, v_ref, g_ref, o_ref, acc_ref, s_ref, p_ref):
    def operand(ref, h):
        return ref[:, h * HEAD_DIM:(h + 1) * HEAD_DIM]

    units = [(slice(0, q_ref.shape[0]), h, functools.partial(operand, k_ref, h),
              functools.partial(operand, v_ref, h), lambda q0: {}) for h in range(N_KV_HEADS)]
    _attn_units(sink_ref, q_ref, units, acc_ref, s_ref, p_ref)
    _rms_rows(acc_ref, g_ref, o_ref)


def _ctx_attn(pc, sink, g_attn):
    n_ctx = pc.shape[0]
    kcol = Q_WIDTH // KV_WIDTH
    return pl.pallas_call(
        _ctx_attn_kernel,
        grid=(1,),
        in_specs=[
            pl.BlockSpec(memory_space=pltpu.SMEM),
            pl.BlockSpec((n_ctx, Q_WIDTH), lambda i: (0, 0)),
            pl.BlockSpec((n_ctx, KV_WIDTH), lambda i: (0, kcol)),
            pl.BlockSpec((n_ctx, KV_WIDTH), lambda i: (0, kcol + 1)),
            pl.BlockSpec((1, Q_WIDTH), lambda i: (0, 0)),
        ],
        out_specs=pl.BlockSpec((n_ctx, Q_WIDTH), lambda i: (0, 0)),
        out_shape=jax.ShapeDtypeStruct((n_ctx, Q_WIDTH), BF16),
        scratch_shapes=[
            pltpu.VMEM((n_ctx, Q_WIDTH), F32),
            pltpu.VMEM((2, GROUP * n_ctx, n_ctx), F32),
            pltpu.VMEM((2, GROUP * n_ctx, n_ctx), BF16),
        ],
        compiler_params=_params(("arbitrary",), 16 << 20),
        name="ctx_attn",
    )(sink, pc, pc, pc, g_attn.reshape(1, Q_WIDTH))


def _dft_tables(n):
    ang = 2.0 * np.pi * (np.outer(np.arange(n), np.arange(n)) % n) / n
    return np.cos(ang), np.sin(ang)


def _fourier_head_kernel(g_ref, z_ref, o_ref):
    for r in range(z_ref.shape[0]):
        o_ref[r] = _dot(g_ref[r], z_ref[r]).astype(o_ref.dtype)


def _fourier_head(zt, gmat):
    nb, na, c = zt.shape
    rb = min(HEAD_RESIDUES, nb)
    return pl.pallas_call(
        _fourier_head_kernel,
        grid=(nb // rb,),
        in_specs=[
            pl.BlockSpec((rb, 2 * na, na), lambda b: (b, 0, 0)),
            pl.BlockSpec((rb, na, c), lambda b: (b, 0, 0)),
        ],
        out_specs=pl.BlockSpec((rb, 2 * na, c), lambda b: (b, 0, 0)),
        out_shape=jax.ShapeDtypeStruct((nb, 2 * na, c), BF16),
        compiler_params=_params(("parallel",), 6 * rb * na * c * 2),
        cost_estimate=pl.CostEstimate(
            flops=4 * nb * na * na * c, transcendentals=0,
            bytes_accessed=(3 * nb * na * c + 2 * nb * na * na) * 2),
        name="fourier_head",
    )(gmat, zt)


def _fourier_tail_kernel(m_ref, d_ref, cs_ref, wf_ref, g_ref, o_ref, acc_ref, *, pos_scale):
    kb, p = o_ref.shape[0], o_ref.shape[1]
    xs = [(_dot(m_ref[...], d_ref[blk]) * pos_scale).astype(BF16) for blk in range(kb)]
    lhs = jnp.concatenate(
        [jnp.concatenate([x[:p, g * F_DIM:(g + 1) * F_DIM], x[p:, g * F_DIM:(g + 1) * F_DIM]], axis=1)
         for g in range(N_FGROUPS) for x in xs], axis=0)
    f = (_dot(lhs, cs_ref[...]) * (F_DIM ** -0.5)).astype(BF16)
    for g in range(N_FGROUPS):
        og = _dot(f[g * kb * p:(g + 1) * kb * p, :], wf_ref[g])
        for blk in range(kb):
            acc_ref[blk, :, g * F_DIM:(g + 1) * F_DIM] = og[blk * p:(blk + 1) * p, :]
    for blk in range(kb):
        _rms_rows(acc_ref.at[blk], g_ref, o_ref.at[blk])


def _fourier_tail(stage_mat, data, cs, wf_all, layer, g_four, pos_scale):
    nblk, k_in, c = data.shape
    p = stage_mat.shape[0] // 2
    kb = min(TAIL_BLOCKS, nblk)
    temps = kb * p * c * (4 + 2 + 2 + 4 + 2)
    vmem = 2 * kb * (k_in + p) * c * 2 + kb * p * c * 4 + 2 * (2 * p * k_in + 6 * F_DIM * F_DIM) * 2 + temps
    return pl.pallas_call(
        functools.partial(_fourier_tail_kernel, pos_scale=pos_scale),
        grid=(nblk // kb,),
        in_specs=[
            pl.BlockSpec((2 * p, k_in), lambda i: (0, 0)),
            pl.BlockSpec((kb, k_in, c), lambda i: (i, 0, 0)),
            pl.BlockSpec((2 * F_DIM, F_DIM), lambda i: (0, 0)),
            pl.BlockSpec((None, N_FGROUPS, F_DIM, F_DIM), lambda i: (layer, 0, 0, 0)),
            pl.BlockSpec((1, c), lambda i: (0, 0)),
        ],
        out_specs=pl.BlockSpec((kb, p, c), lambda i: (i, 0, 0)),
        out_shape=jax.ShapeDtypeStruct((nblk, p, c), BF16),
        scratch_shapes=[pltpu.VMEM((kb, p, c), F32)],
        compiler_params=_params(("parallel",), vmem),
        cost_estimate=pl.CostEstimate(
            flops=2 * nblk * (2 * p * k_in * c + p * c * 2 * F_DIM + p * c * F_DIM), transcendentals=nblk * p,
            bytes_accessed=nblk * (k_in + p) * c * 2 + 2 * p * k_in * 2 + 6 * F_DIM * F_DIM * 2),
        name="fourier_tail",
    )(stage_mat, data, cs, wf_all, g_four.reshape(1, c))


class _FourierConsts:
    def __init__(self, n_lat, n_ctx):
        a = b = int(round(math.sqrt(n_lat)))
        assert a * b == n_lat
        self.a, self.b = a, b
        k_lo = np.arange(a)[None, :, None]
        n = (np.arange(a)[None, None, :] * b + np.arange(b)[:, None, None])
        ang = 2.0 * np.pi * ((k_lo * n) % n_lat) / n_lat
        self.head = jnp.asarray(np.concatenate([np.cos(ang), -np.sin(ang)], axis=1), BF16)
        cb, sb = _dft_tables(b)
        self.tail = jnp.asarray(np.block([[cb, sb], [-sb, cb]]), BF16)
        cc, sc = _dft_tables(n_ctx)
        self.ctx = jnp.asarray(np.concatenate([cc, -sc], axis=0), BF16)
        cf, sf = _dft_tables(F_DIM)
        self.chan = jnp.asarray(np.concatenate([cf, sf], axis=0), BF16)


def _latent_fourier(pf, fc, wf_all, layer, g_four):
    n_rows = pf.shape[0]
    a, b = fc.a, fc.b
    zt = jnp.transpose(pf.reshape(a, b, F_WIDTH), (1, 0, 2))
    t = _fourier_head(zt, fc.head)
    t = jnp.transpose(t.reshape(b, 2, a, F_WIDTH), (2, 1, 0, 3)).reshape(a, 2 * b, F_WIDTH)
    o = _fourier_tail(fc.tail, t, fc.chan, wf_all, layer, g_four, 1.0 / math.sqrt(n_rows))
    return jnp.transpose(o, (1, 0, 2)).reshape(n_rows, F_WIDTH)


def _ctx_fourier(pf, fc, wf_all, layer, g_four):
    n_ctx = pf.shape[0]
    d = pf.reshape(1, n_ctx, F_WIDTH)
    o = _fourier_tail(fc.ctx, d, fc.chan, wf_all, layer, g_four, 1.0 / math.sqrt(n_ctx))
    return o.reshape(n_ctx, F_WIDTH)


def _outproj_kernel(na_ref, nf_ref, w_ref, x_ref, gate_ref, o_ref):
    ka = na_ref.shape[1]
    acc = _dot(na_ref[...], w_ref[:ka, :]) + _dot(nf_ref[...], w_ref[ka:, :])
    o_ref[...] = x_ref[...] + gate_ref[...] * acc


def _outproj(na, nf, w_all, layer, x, gate, tm, tn):
    n_rows, d = x.shape
    ka, kf = na.shape[1], nf.shape[1]
    vmem = 2 * (tm * (ka + kf) * 2 + (ka + kf) * tn * 2 + 2 * tm * tn * 4) + tm * tn * 4
    return pl.pallas_call(
        _outproj_kernel,
        grid=(n_rows // tm, d // tn),
        in_specs=[
            pl.BlockSpec((tm, ka), lambda i, j: (i, 0)),
            pl.BlockSpec((tm, kf), lambda i, j: (i, 0)),
            pl.BlockSpec((None, ka + kf, tn), lambda i, j: (layer, 0, j)),
            pl.BlockSpec((tm, tn), lambda i, j: (i, j)),
            pl.BlockSpec((1, tn), lambda i, j: (0, j)),
        ],
        out_specs=pl.BlockSpec((tm, tn), lambda i, j: (i, j)),
        out_shape=jax.ShapeDtypeStruct((n_rows, d), F32),
        compiler_params=_params(("parallel", "parallel"), vmem),
        cost_estimate=pl.CostEstimate(
            flops=2 * n_rows * (ka + kf) * d, transcendentals=0,
            bytes_accessed=n_rows * (ka + kf) * 2 + (n_rows // tm) * (ka + kf) * d * 2 + 2 * n_rows * d * 4),
        name="outproj",
    )(na, nf, w_all, x, gate.reshape(1, d))


def _mlp_kernel(x_ref, gss_ref, w1_ref, w2_ref, o_ref, h_ref, ops_ref, *, final):
    f = pl.program_id(1)

    @pl.when(f == 0)
    def _():
        _normmod_rows(x_ref, gss_ref, ops_ref, h_ref, copy_ref=o_ref)

    u = jnp.maximum(_dot(h_ref[...], w1_ref[...]), 0.0)
    u = (u * u).astype(BF16)
    tn = MLP_OUT_CHUNK
    for c in range(o_ref.shape[1] // tn):
        cs = slice(c * tn, (c + 1) * tn)
        o_ref[:, cs] += gss_ref[3:4, cs] * _dot(u, w2_ref[:, cs])

    if final:
        @pl.when(f == pl.num_programs(1) - 1)
        def _():
            _rms_rows(o_ref, gss_ref.at[4:5, :], o_ref)


def _mlp(x, gss, w1_all, w2_all, layer, tm, tf, final):
    n_rows, d = x.shape
    d_ff = w1_all.shape[2]
    vmem = 2 * (2 * tm * d * 4 + 2 * d * tf * 2) + tm * d * 2 + tm * tf * 6 + 2 * tm * MLP_OUT_CHUNK * 4
    return pl.pallas_call(
        functools.partial(_mlp_kernel, final=final),
        grid=(n_rows // tm, d_ff // tf),
        in_specs=[
            pl.BlockSpec((tm, d), lambda i, f: (i, 0)),
            pl.BlockSpec((8, d), lambda i, f: (0, 0)),
            pl.BlockSpec((None, d, tf), lambda i, f: (layer, 0, f)),
            pl.BlockSpec((None, tf, d), lambda i, f: (layer, f, 0)),
        ],
        out_specs=pl.BlockSpec((tm, d), lambda i, f: (i, 0)),
        out_shape=jax.ShapeDtypeStruct((n_rows, d), F32),
        scratch_shapes=[pltpu.VMEM((tm, d), BF16), pltpu.VMEM((8, d), F32)],
        compiler_params=_params(("parallel", "arbitrary"), vmem),
        cost_estimate=pl.CostEstimate(
            flops=4 * n_rows * d * d_ff, transcendentals=n_rows,
            bytes_accessed=2 * n_rows * d * 4 + (n_rows // tm) * 2 * d * d_ff * 2),
        name="mlp_final" if final else "mlp",
    )(x, gss, w1_all, w2_all)


def _rope_tables(n_rows):
    quarter = HEAD_DIM // 4
    inv = ROPE_THETA ** (-jnp.arange(quarter, dtype=F32) / quarter)
    pos = jnp.arange(n_rows)
    rows = (pos // GRID_W).astype(F32)
    cols = (pos % GRID_W).astype(F32)
    ang = jnp.concatenate([rows[:, None] * inv[None, :]] * 2 + [cols[:, None] * inv[None, :]] * 2, axis=1)
    sign = jnp.tile(jnp.concatenate([-jnp.ones(quarter, F32), jnp.ones(quarter, F32)]), 2)
    return jnp.cos(ang), jnp.sin(ang) * sign[None, :]


def _rows8(*vecs):
    d = vecs[0].shape[0]
    pad = [jnp.zeros((d,), F32)] * (8 - len(vecs))
    return jnp.stack(list(vecs) + pad, axis=0)


def _row_tile(n_rows, want):
    return min(want, n_rows)


def kernel(x, c, ctx, c_ctx, ada_a, ada_b, ada_bias, g_mix, w_in, sink, w_f, g_attn_out, g_four_out,
           w_out, g_mlp, w1, w2, g_final):
    assert x.shape[0] == 1 and ctx.shape[0] == 1
    depth = w_in.shape[0]
    d = x.shape[2]
    xs = x[0]
    cs = ctx[0]
    n_lat, n_ctx = xs.shape[0], cs.shape[0]

    cv = jnp.zeros((ADA_ROWS, d), F32).at[0].set(c[0]).at[1].set(c_ctx)
    mods = _ada(cv, ada_a, ada_b, ada_bias)
    rope = _rope_tables(n_lat)
    fconst = _FourierConsts(n_lat, n_ctx)
    w_in_b = w_in.astype(BF16)
    w_out_b = w_out.astype(BF16)
    w1_b = w1.astype(BF16)
    w2_b = w2.astype(BF16)
    wf_b = w_f.astype(BF16)

    for l in range(depth):
        last = l == depth - 1
        m_lat = [mods[l, 0, i * d:(i + 1) * d] for i in range(N_MOD)]
        m_ctx = [mods[l, 1, i * d:(i + 1) * d] for i in range(N_MOD)]

        pc, pcf = _inproj(cs, _rows8(g_mix[l], m_ctx[0], m_ctx[1]), w_in_b, l, None, _row_tile(n_ctx, 512), 1024)
        px, pxf = _inproj(xs, _rows8(g_mix[l], m_lat[0], m_lat[1]), w_in_b, l, rope, _row_tile(n_lat, 512), 1024)

        na = _latent_attn(px, pc, sink[l], g_attn_out[l])
        nf = _latent_fourier(pxf, fconst, wf_b, l, g_four_out[l])
        x_mid = _outproj(na, nf, w_out_b, l, xs, m_lat[2], _row_tile(n_lat, 1024), 1024)
        gss = _rows8(g_mlp[l], m_lat[3], m_lat[4], m_lat[5], g_final)
        xs = _mlp(x_mid, gss, w1_b, w2_b, l, _row_tile(n_lat, 512), 512, final=last)

        if not last:
            nac = _ctx_attn(pc, sink[l], g_attn_out[l])
            nfc = _ctx_fourier(pcf, fconst, wf_b, l, g_four_out[l])
            c_mid = _outproj(nac, nfc, w_out_b, l, cs, m_ctx[2], _row_tile(n_ctx, 1024), 1024)
            gss_c = _rows8(g_mlp[l], m_ctx[3], m_ctx[4], m_ctx[5])
            cs = _mlp(c_mid, gss_c, w1_b, w2_b, l, _row_tile(n_ctx, 512), 512, final=False)

    return xs[None]
```

```python
import functools
import math

import numpy as np
import jax
import jax.numpy as jnp
from jax import lax
from jax.experimental import pallas as pl
from jax.experimental.pallas import tpu as pltpu

F32 = jnp.float32
BF16 = jnp.bfloat16

N_HEADS = 16
N_KV_HEADS = 4
HEAD_DIM = 128
GROUP = N_HEADS // N_KV_HEADS
Q_WIDTH = N_HEADS * HEAD_DIM
KV_WIDTH = N_KV_HEADS * HEAD_DIM
BLOCK = 128
GRID_W = 64
N_FGROUPS = 4
F_DIM = 512
F_WIDTH = N_FGROUPS * F_DIM
N_MOD = 6
ROPE_THETA = 10000.0
EPS = 1e-6
NEG_INF = -1e30
LOG2E = 1.4426950408889634

V7X_VMEM_BYTES = 64 * 1024 * 1024
ADA_ROWS = 8
MLP_OUT_CHUNK = 512
ROW_CHUNK = 16
ROW_UNROLL = 2
STREAM_UNROLL = 8
DOT_COLS = 256
SOFTMAX_ROWS = 32
ATTN_Q_BLOCKS = 2
HEAD_RESIDUES = 8
TAIL_BLOCKS = 4


def _params(semantics, vmem_bytes):
    limit = min(int(vmem_bytes * 1.15) + (4 << 20), V7X_VMEM_BYTES - (6 << 20))
    return pltpu.CompilerParams(dimension_semantics=semantics, vmem_limit_bytes=limit)


def _dot(a, b):
    return jnp.dot(a, b, preferred_element_type=F32)


def _dot_nt(a, b):
    return lax.dot_general(a, b, (((1,), (1,)), ((), ())), preferred_element_type=F32)


def _rms(x, g):
    return x * lax.rsqrt(jnp.mean(x * x, axis=-1, keepdims=True) + EPS) * g


def _for_row_chunks(n_rows, body, unroll=ROW_UNROLL):
    rc = min(ROW_CHUNK, n_rows)

    def step(r, carry):
        body(pl.ds(pl.multiple_of(r * rc, rc), rc))
        return carry

    lax.fori_loop(0, n_rows // rc, step, 0, unroll=min(unroll, n_rows // rc))


def _normmod_rows(x_ref, gss_ref, ops_ref, h_ref, copy_ref=None):
    ops_ref[0:1, :] = 1.0 + gss_ref[2:3, :]

    def body(rows):
        x = x_ref[rows, :]
        y = _rms(x, gss_ref[0:1, :])
        h_ref[rows, :] = (y * ops_ref[0:1, :] + gss_ref[1:2, :]).astype(h_ref.dtype)
        if copy_ref is not None:
            copy_ref[rows, :] = x

    _for_row_chunks(x_ref.shape[0], body)


def _rms_rows(src_ref, g_ref, dst_ref):
    def body(rows):
        dst_ref[rows, :] = _rms(src_ref[rows, :], g_ref[...]).astype(dst_ref.dtype)

    _for_row_chunks(src_ref.shape[0], body)


def _rms_rows_from_sumsq(src_ref, ss_ref, g_ref, dst_ref):
    width = src_ref.shape[1]

    def body(rows):
        inv = lax.rsqrt(ss_ref[rows, :] * (1.0 / width) + EPS)
        dst_ref[rows, :] = (src_ref[rows, :] * inv * g_ref[...]).astype(dst_ref.dtype)

    _for_row_chunks(src_ref.shape[0], body, unroll=STREAM_UNROLL)


def _sumsq(x):
    return jnp.sum(x * x, axis=-1, keepdims=True)


def _ada_kernel(cv_ref, a_ref, b_ref, bias_ref, o_ref, h_ref):
    @pl.when(pl.program_id(1) == 0)
    def _():
        cv = cv_ref[...]
        s = cv * jax.nn.sigmoid(cv)
        h_ref[...] = _dot(s.astype(BF16), a_ref[0].astype(BF16))

    o_ref[0] = _dot(h_ref[...].astype(BF16), b_ref[0].astype(BF16)) + bias_ref[0]


def _ada(cv, ada_a, ada_b, ada_bias):
    depth, d, rank = ada_a.shape
    n = ada_b.shape[2]
    tn = d
    vmem = 2 * (d * rank * 4 + rank * tn * 4) + 4 * ADA_ROWS * (d + tn) * 4
    return pl.pallas_call(
        _ada_kernel,
        grid=(depth, n // tn),
        in_specs=[
            pl.BlockSpec((ADA_ROWS, d), lambda l, j: (0, 0)),
            pl.BlockSpec((1, d, rank), lambda l, j: (l, 0, 0)),
            pl.BlockSpec((1, rank, tn), lambda l, j: (l, 0, j)),
            pl.BlockSpec((1, 1, tn), lambda l, j: (l, 0, j)),
        ],
        out_specs=pl.BlockSpec((1, ADA_ROWS, tn), lambda l, j: (l, 0, j)),
        out_shape=jax.ShapeDtypeStruct((depth, ADA_ROWS, n), F32),
        scratch_shapes=[pltpu.VMEM((ADA_ROWS, rank), F32)],
        compiler_params=_params(("parallel", "arbitrary"), vmem),
        name="ada",
    )(cv, ada_a, ada_b, ada_bias.reshape(depth, 1, n))


def _rope_chunk(a, cos, sin, lane):
    up = pltpu.roll(a, HEAD_DIM - 32, axis=1)
    down = pltpu.roll(a, 32, axis=1)
    partner = jnp.where((lane % 64) < 32, up, down)
    return a * cos + partner * sin


def _inproj_kernel(x_ref, gss_ref, w_ref, *rest, rope_cols, tn):
    if rope_cols:
        cos_ref, sin_ref, qkv_ref, f_ref, h_ref, ops_ref = rest
    else:
        qkv_ref, f_ref, h_ref, ops_ref = rest
    j = pl.program_id(1)

    @pl.when(j == 0)
    def _():
        _normmod_rows(x_ref, gss_ref, ops_ref, h_ref)

    heads_per_dot = DOT_COLS // HEAD_DIM

    def tile(n_rope, o_ref):
        if n_rope:
            lane = lax.broadcasted_iota(jnp.int32, (x_ref.shape[0], HEAD_DIM), 1)
        for c in range(tn // DOT_COLS):
            acc = _dot(h_ref[...], w_ref[:, c * DOT_COLS:(c + 1) * DOT_COLS])
            for k in range(heads_per_dot):
                head = c * heads_per_dot + k
                a = acc[:, k * HEAD_DIM:(k + 1) * HEAD_DIM]
                if head < n_rope:
                    a = _rope_chunk(a, cos_ref[...], sin_ref[...], lane)
                o_ref[:, head * HEAD_DIM:(head + 1) * HEAD_DIM] = a.astype(o_ref.dtype)

    qkv_tiles = (Q_WIDTH + 2 * KV_WIDTH) // tn
    full_tiles, part = divmod(rope_cols, tn)
    if full_tiles:
        pl.when(j < full_tiles)(functools.partial(tile, tn // HEAD_DIM, qkv_ref))
    if part:
        pl.when(j == full_tiles)(functools.partial(tile, part // HEAD_DIM, qkv_ref))
        full_tiles += 1
    if full_tiles < qkv_tiles:
        pl.when((j >= full_tiles) & (j < qkv_tiles))(functools.partial(tile, 0, qkv_ref))
    pl.when(j >= qkv_tiles)(functools.partial(tile, 0, f_ref))


def _inproj(x, gss, w_all, layer, rope_tables, tm, tn):
    n_rows, d = x.shape
    n_out = w_all.shape[2]
    qkv_cols = Q_WIDTH + 2 * KV_WIDTH
    qkv_tiles = qkv_cols // tn
    rope_cols = Q_WIDTH + KV_WIDTH if rope_tables is not None else 0
    in_specs = [
        pl.BlockSpec((tm, d), lambda i, j: (i, 0)),
        pl.BlockSpec((8, d), lambda i, j: (0, 0)),
        pl.BlockSpec((None, d, tn), lambda i, j: (layer, 0, j)),
    ]
    args = [x, gss, w_all]
    if rope_tables is not None:
        in_specs += [pl.BlockSpec((tm, HEAD_DIM), lambda i, j: (i, 0))] * 2
        args += list(rope_tables)
    vmem = (2 * (tm * d * 4 + d * tn * 2 + 2 * tm * tn * 2 + 2 * tm * HEAD_DIM * 4)
            + tm * d * 2 + 4 * tm * DOT_COLS * 4)
    cost = pl.CostEstimate(flops=2 * n_rows * d * n_out, transcendentals=n_rows,
                           bytes_accessed=n_rows * d * 4 + (n_rows // tm) * d * n_out * 2 + n_rows * n_out * 2)
    return pl.pallas_call(
        functools.partial(_inproj_kernel, rope_cols=rope_cols, tn=tn),
        grid=(n_rows // tm, n_out // tn),
        in_specs=in_specs,
        out_specs=[
            pl.BlockSpec((tm, tn), lambda i, j: (i, jnp.minimum(j, qkv_tiles - 1))),
            pl.BlockSpec((tm, tn), lambda i, j: (i, jnp.maximum(j - qkv_tiles, 0))),
        ],
        out_shape=[
            jax.ShapeDtypeStruct((n_rows, qkv_cols), BF16),
            jax.ShapeDtypeStruct((n_rows, n_out - qkv_cols), BF16),
        ],
        scratch_shapes=[pltpu.VMEM((tm, d), BF16), pltpu.VMEM((8, d), F32)],
        compiler_params=_params(("parallel", "arbitrary"), vmem),
        cost_estimate=cost,
        name="inproj_rope" if rope_cols else "inproj_ctx",
    )(*args)


def _stack_heads(q_ref, rows, h):
    return jnp.concatenate(
        [q_ref[rows, (h * GROUP + g) * HEAD_DIM:(h * GROUP + g + 1) * HEAD_DIM] for g in range(GROUP)], axis=0)


def _softmax_chunk(s_ref, p_ref, rows, sink, masks):
    n_tiles = s_ref.shape[1] // BLOCK
    tiles = []
    for t in range(n_tiles):
        v = s_ref[rows, t * BLOCK:(t + 1) * BLOCK]
        if masks.get(t) is not None:
            v = jnp.where(masks[t], v, NEG_INF)
        tiles.append(v)
    m_raw = jnp.max(functools.reduce(jnp.maximum, tiles), axis=-1, keepdims=True)
    m = jnp.maximum(m_raw * (HEAD_DIM ** -0.5), sink)
    mb = m * LOG2E
    es = [jnp.exp2(v * (HEAD_DIM ** -0.5 * LOG2E) - mb) for v in tiles]
    denom = jnp.sum(functools.reduce(jnp.add, es), axis=-1, keepdims=True) + jnp.exp2(sink * LOG2E - mb)
    inv = 1.0 / denom
    for t in range(n_tiles):
        p_ref[rows, t * BLOCK:(t + 1) * BLOCK] = (es[t] * inv).astype(p_ref.dtype)


def _attn_units(sink_ref, q_ref, units, acc_ref, ss_ref, s_ref, p_ref):
    def scores(u):
        q_rows, h, keys, _, _ = units[u]
        s_ref[u % 2] = _dot_nt(_stack_heads(q_ref, q_rows, h), keys())

    scores(0)
    for u, (q_rows, h, _, values, mask_fn) in enumerate(units):
        if u + 1 < len(units):
            scores(u + 1)
        n_q = q_rows.stop - q_rows.start
        chunks_per_head = n_q // SOFTMAX_ROWS
        for r in range(GROUP * chunks_per_head):
            rows = slice(r * SOFTMAX_ROWS, (r + 1) * SOFTMAX_ROWS)
            sink = sink_ref[h * GROUP + r // chunks_per_head]
            q0 = (r % chunks_per_head) * SOFTMAX_ROWS
            _softmax_chunk(s_ref.at[u % 2], p_ref.at[u % 2], rows, sink, mask_fn(q0))
        o = _dot(p_ref[u % 2], values())
        ss = None
        for g in range(GROUP):
            c0 = (h * GROUP + g) * HEAD_DIM
            og = o[g * n_q:(g + 1) * n_q, :]
            acc_ref[q_rows, c0:c0 + HEAD_DIM] = og
            ss = _sumsq(og) if ss is None else ss + _sumsq(og)
        ss_ref[q_rows, :] = ss if h == 0 else ss_ref[q_rows, :] + ss


def _latent_attn_kernel(sink_ref, q_ref, kp_ref, kc_ref, kn_ref, vp_ref, vc_ref, vn_ref,
                        kx_ref, vx_ref, g_ref, o_ref, acc_ref, ss_ref, s_ref, p_ref, *, n_ctx, q_blocks):
    n = pl.program_id(0)
    nb = pl.num_programs(0)
    ctx_tiles = n_ctx // BLOCK
    prev_lo = jnp.where(n == 0, BLOCK, 0)
    next_hi = jnp.where(n == nb - 1, 0, BLOCK)

    def block_of(refs, i):
        prev_ref, cur_ref, next_ref = refs
        if i == 0:
            return prev_ref, slice(0, BLOCK)
        if i == q_blocks + 1:
            return next_ref, slice(0, BLOCK)
        return cur_ref, slice((i - 1) * BLOCK, i * BLOCK)

    def operand(ctx_ref, refs, qb, h):
        hs = slice(h * HEAD_DIM, (h + 1) * HEAD_DIM)
        parts = [ctx_ref[:, hs]]
        for i in range(qb, qb + 3):
            ref, rows = block_of(refs, i)
            parts.append(ref[rows, hs])
        return jnp.concatenate(parts, axis=0)

    def mask_fn(qb, q0):
        qi = lax.broadcasted_iota(jnp.int32, (SOFTMAX_ROWS, BLOCK), 0) + q0
        kj = lax.broadcasted_iota(jnp.int32, (SOFTMAX_ROWS, BLOCK), 1)
        lo = kj >= qi
        hi = kj <= qi
        if qb == 0:
            lo = lo & (kj >= prev_lo)
        if qb == q_blocks - 1:
            hi = hi & (kj < next_hi)
        return {ctx_tiles: lo, ctx_tiles + 2: hi}

    units = []
    for qb in range(q_blocks):
        for h in range(N_KV_HEADS):
            units.append((
                slice(qb * BLOCK, (qb + 1) * BLOCK), h,
                functools.partial(operand, kx_ref, (kp_ref, kc_ref, kn_ref), qb, h),
                functools.partial(operand, vx_ref, (vp_ref, vc_ref, vn_ref), qb, h),
                functools.partial(mask_fn, qb),
            ))
    _attn_units(sink_ref, q_ref, units, acc_ref, ss_ref, s_ref, p_ref)
    _rms_rows_from_sumsq(acc_ref, ss_ref, g_ref, o_ref)


def _latent_attn(px, pc, sink, g_attn):
    n_rows = px.shape[0]
    n_ctx = pc.shape[0]
    nb = n_rows // BLOCK
    qb = ATTN_Q_BLOCKS if nb % ATTN_Q_BLOCKS == 0 else 1
    kcol = Q_WIDTH // KV_WIDTH
    vcol = kcol + 1
    n_keys = n_ctx + 3 * BLOCK

    def kv_specs(colblk):
        return [
            pl.BlockSpec((BLOCK, KV_WIDTH), lambda n: (jnp.maximum(n * qb - 1, 0), colblk)),
            pl.BlockSpec((qb * BLOCK, KV_WIDTH), lambda n: (n, colblk)),
            pl.BlockSpec((BLOCK, KV_WIDTH), lambda n: (jnp.minimum(n * qb + qb, nb - 1), colblk)),
        ]

    in_specs = [
        pl.BlockSpec(memory_space=pltpu.SMEM),
        pl.BlockSpec((qb * BLOCK, Q_WIDTH), lambda n: (n, 0)),
        *kv_specs(kcol), *kv_specs(vcol),
        pl.BlockSpec((n_ctx, KV_WIDTH), lambda n: (0, kcol)),
        pl.BlockSpec((n_ctx, KV_WIDTH), lambda n: (0, vcol)),
        pl.BlockSpec((1, Q_WIDTH), lambda n: (0, 0)),
    ]
    return pl.pallas_call(
        functools.partial(_latent_attn_kernel, n_ctx=n_ctx, q_blocks=qb),
        grid=(nb // qb,),
        in_specs=in_specs,
        out_specs=pl.BlockSpec((qb * BLOCK, Q_WIDTH), lambda n: (n, 0)),
        out_shape=jax.ShapeDtypeStruct((n_rows, Q_WIDTH), BF16),
        scratch_shapes=[
            pltpu.VMEM((qb * BLOCK, Q_WIDTH), F32),
            pltpu.VMEM((qb * BLOCK, 1), F32),
            pltpu.VMEM((2, GROUP * BLOCK, n_keys), F32),
            pltpu.VMEM((2, GROUP * BLOCK, n_keys), BF16),
        ],
        compiler_params=_params(("parallel",), 24 << 20),
        cost_estimate=pl.CostEstimate(
            flops=4 * n_rows * N_HEADS * n_keys * HEAD_DIM, transcendentals=n_rows * N_HEADS * n_keys,
            bytes_accessed=n_rows * (2 * Q_WIDTH + 6 * KV_WIDTH) * 2),
        name="latent_attn",
    )(sink, px, px, px, px, px, px, px, pc, pc, g_attn.reshape(1, Q_WIDTH))


def _ctx_attn_kernel(sink_ref, q_ref, k_ref, v_ref, g_ref, o_ref, acc_ref, ss_ref, s_ref, p_ref):
    def operand(ref, h):
        return ref[:, h * HEAD_DIM:(h + 1) * HEAD_DIM]

    units = [(slice(0, q_ref.shape[0]), h, functools.partial(operand, k_ref, h),
              functools.partial(operand, v_ref, h), lambda q0: {}) for h in range(N_KV_HEADS)]
    _attn_units(sink_ref, q_ref, units, acc_ref, ss_ref, s_ref, p_ref)
    _rms_rows_from_sumsq(acc_ref, ss_ref, g_ref, o_ref)


def _ctx_attn(pc, sink, g_attn):
    n_ctx = pc.shape[0]
    kcol = Q_WIDTH // KV_WIDTH
    return pl.pallas_call(
        _ctx_attn_kernel,
        grid=(1,),
        in_specs=[
            pl.BlockSpec(memory_space=pltpu.SMEM),
            pl.BlockSpec((n_ctx, Q_WIDTH), lambda i: (0, 0)),
            pl.BlockSpec((n_ctx, KV_WIDTH), lambda i: (0, kcol)),
            pl.BlockSpec((n_ctx, KV_WIDTH), lambda i: (0, kcol + 1)),
            pl.BlockSpec((1, Q_WIDTH), lambda i: (0, 0)),
        ],
        out_specs=pl.BlockSpec((n_ctx, Q_WIDTH), lambda i: (0, 0)),
        out_shape=jax.ShapeDtypeStruct((n_ctx, Q_WIDTH), BF16),
        scratch_shapes=[
            pltpu.VMEM((n_ctx, Q_WIDTH), F32),
            pltpu.VMEM((n_ctx, 1), F32),
            pltpu.VMEM((2, GROUP * n_ctx, n_ctx), F32),
            pltpu.VMEM((2, GROUP * n_ctx, n_ctx), BF16),
        ],
        compiler_params=_params(("arbitrary",), 16 << 20),
        name="ctx_attn",
    )(sink, pc, pc, pc, g_attn.reshape(1, Q_WIDTH))


def _dft_tables(n):
    ang = 2.0 * np.pi * (np.outer(np.arange(n), np.arange(n)) % n) / n
    return np.cos(ang), np.sin(ang)


def _fourier_head_kernel(g_ref, z_ref, o_ref):
    for r in range(z_ref.shape[0]):
        o_ref[r] = _dot(g_ref[r], z_ref[r]).astype(o_ref.dtype)


def _fourier_head(zt, gmat):
    nb, na, c = zt.shape
    rb = min(HEAD_RESIDUES, nb)
    return pl.pallas_call(
        _fourier_head_kernel,
        grid=(nb // rb,),
        in_specs=[
            pl.BlockSpec((rb, 2 * na, na), lambda b: (b, 0, 0)),
            pl.BlockSpec((rb, na, c), lambda b: (b, 0, 0)),
        ],
        out_specs=pl.BlockSpec((rb, 2 * na, c), lambda b: (b, 0, 0)),
        out_shape=jax.ShapeDtypeStruct((nb, 2 * na, c), BF16),
        compiler_params=_params(("parallel",), 6 * rb * na * c * 2),
        cost_estimate=pl.CostEstimate(
            flops=4 * nb * na * na * c, transcendentals=0,
            bytes_accessed=(3 * nb * na * c + 2 * nb * na * na) * 2),
        name="fourier_head",
    )(gmat, zt)


def _fourier_tail_kernel(m_ref, d_ref, cs_ref, wf_ref, g_ref, o_ref, acc_ref, ss_ref, *, pos_scale):
    kb, p = o_ref.shape[0], o_ref.shape[1]
    xs = [(_dot(m_ref[...], d_ref[blk]) * pos_scale).astype(BF16) for blk in range(kb)]
    lhs = jnp.concatenate(
        [jnp.concatenate([x[:p, g * F_DIM:(g + 1) * F_DIM], x[p:, g * F_DIM:(g + 1) * F_DIM]], axis=1)
         for g in range(N_FGROUPS) for x in xs], axis=0)
    f = (_dot(lhs, cs_ref[...]) * (F_DIM ** -0.5)).astype(BF16)
    for g in range(N_FGROUPS):
        og = _dot(f[g * kb * p:(g + 1) * kb * p, :], wf_ref[g])
        for blk in range(kb):
            piece = og[blk * p:(blk + 1) * p, :]
            acc_ref[blk, :, g * F_DIM:(g + 1) * F_DIM] = piece
            ss_ref[blk] = _sumsq(piece) if g == 0 else ss_ref[blk] + _sumsq(piece)
    for blk in range(kb):
        _rms_rows_from_sumsq(acc_ref.at[blk], ss_ref.at[blk], g_ref, o_ref.at[blk])


def _fourier_tail(stage_mat, data, cs, wf_all, layer, g_four, pos_scale):
    nblk, k_in, c = data.shape
    p = stage_mat.shape[0] // 2
    kb = min(TAIL_BLOCKS, nblk)
    temps = kb * p * c * (4 + 2 + 2 + 4 + 2)
    vmem = 2 * kb * (k_in + p) * c * 2 + kb * p * c * 4 + 2 * (2 * p * k_in + 6 * F_DIM * F_DIM) * 2 + temps
    return pl.pallas_call(
        functools.partial(_fourier_tail_kernel, pos_scale=pos_scale),
        grid=(nblk // kb,),
        in_specs=[
            pl.BlockSpec((2 * p, k_in), lambda i: (0, 0)),
            pl.BlockSpec((kb, k_in, c), lambda i: (i, 0, 0)),
            pl.BlockSpec((2 * F_DIM, F_DIM), lambda i: (0, 0)),
            pl.BlockSpec((None, N_FGROUPS, F_DIM, F_DIM), lambda i: (layer, 0, 0, 0)),
            pl.BlockSpec((1, c), lambda i: (0, 0)),
        ],
        out_specs=pl.BlockSpec((kb, p, c), lambda i: (i, 0, 0)),
        out_shape=jax.ShapeDtypeStruct((nblk, p, c), BF16),
        scratch_shapes=[pltpu.VMEM((kb, p, c), F32), pltpu.VMEM((kb, p, 1), F32)],
        compiler_params=_params(("parallel",), vmem),
        cost_estimate=pl.CostEstimate(
            flops=2 * nblk * (2 * p * k_in * c + p * c * 2 * F_DIM + p * c * F_DIM), transcendentals=nblk * p,
            bytes_accessed=nblk * (k_in + p) * c * 2 + 2 * p * k_in * 2 + 6 * F_DIM * F_DIM * 2),
        name="fourier_tail",
    )(stage_mat, data, cs, wf_all, g_four.reshape(1, c))


class _FourierConsts:
    def __init__(self, n_lat, n_ctx):
        a = b = int(round(math.sqrt(n_lat)))
        assert a * b == n_lat
        self.a, self.b = a, b
        k_lo = np.arange(a)[None, :, None]
        n = (np.arange(a)[None, None, :] * b + np.arange(b)[:, None, None])
        ang = 2.0 * np.pi * ((k_lo * n) % n_lat) / n_lat
        self.head = jnp.asarray(np.concatenate([np.cos(ang), -np.sin(ang)], axis=1), BF16)
        cb, sb = _dft_tables(b)
        self.tail = jnp.asarray(np.block([[cb, sb], [-sb, cb]]), BF16)
        cc, sc = _dft_tables(n_ctx)
        self.ctx = jnp.asarray(np.concatenate([cc, -sc], axis=0), BF16)
        cf, sf = _dft_tables(F_DIM)
        self.chan = jnp.asarray(np.concatenate([cf, sf], axis=0), BF16)


def _latent_fourier(pf, fc, wf_all, layer, g_four):
    n_rows = pf.shape[0]
    a, b = fc.a, fc.b
    zt = jnp.transpose(pf.reshape(a, b, F_WIDTH), (1, 0, 2))
    t = _fourier_head(zt, fc.head)
    t = jnp.transpose(t.reshape(b, 2, a, F_WIDTH), (2, 1, 0, 3)).reshape(a, 2 * b, F_WIDTH)
    o = _fourier_tail(fc.tail, t, fc.chan, wf_all, layer, g_four, 1.0 / math.sqrt(n_rows))
    return jnp.transpose(o, (1, 0, 2)).reshape(n_rows, F_WIDTH)


def _ctx_fourier(pf, fc, wf_all, layer, g_four):
    n_ctx = pf.shape[0]
    d = pf.reshape(1, n_ctx, F_WIDTH)
    o = _fourier_tail(fc.ctx, d, fc.chan, wf_all, layer, g_four, 1.0 / math.sqrt(n_ctx))
    return o.reshape(n_ctx, F_WIDTH)


def _outproj_kernel(na_ref, nf_ref, w_ref, x_ref, gate_ref, o_ref):
    ka = na_ref.shape[1]
    acc = _dot(na_ref[...], w_ref[:ka, :]) + _dot(nf_ref[...], w_ref[ka:, :])
    o_ref[...] = x_ref[...] + gate_ref[...] * acc


def _outproj(na, nf, w_all, layer, x, gate, tm, tn):
    n_rows, d = x.shape
    ka, kf = na.shape[1], nf.shape[1]
    vmem = 2 * (tm * (ka + kf) * 2 + (ka + kf) * tn * 2 + 2 * tm * tn * 4) + tm * tn * 4
    return pl.pallas_call(
        _outproj_kernel,
        grid=(n_rows // tm, d // tn),
        in_specs=[
            pl.BlockSpec((tm, ka), lambda i, j: (i, 0)),
            pl.BlockSpec((tm, kf), lambda i, j: (i, 0)),
            pl.BlockSpec((None, ka + kf, tn), lambda i, j: (layer, 0, j)),
            pl.BlockSpec((tm, tn), lambda i, j: (i, j)),
            pl.BlockSpec((1, tn), lambda i, j: (0, j)),
        ],
        out_specs=pl.BlockSpec((tm, tn), lambda i, j: (i, j)),
        out_shape=jax.ShapeDtypeStruct((n_rows, d), F32),
        compiler_params=_params(("parallel", "parallel"), vmem),
        cost_estimate=pl.CostEstimate(
            flops=2 * n_rows * (ka + kf) * d, transcendentals=0,
            bytes_accessed=n_rows * (ka + kf) * 2 + (n_rows // tm) * (ka + kf) * d * 2 + 2 * n_rows * d * 4),
        name="outproj",
    )(na, nf, w_all, x, gate.reshape(1, d))


def _mlp_kernel(x_ref, gss_ref, w1_ref, w2_ref, *rest, final, n_cast):
    raw_refs = rest[:n_cast]
    o_ref = rest[n_cast]
    cast_refs = rest[n_cast + 1:2 * n_cast + 1]
    h_ref, ops_ref = rest[2 * n_cast + 1:]
    f = pl.program_id(1)

    @pl.when(f == 0)
    def _():
        _normmod_rows(x_ref, gss_ref, ops_ref, h_ref, copy_ref=o_ref)

    for raw_ref, cast_ref in zip(raw_refs, cast_refs):
        cast_ref[...] = raw_ref[...].astype(cast_ref.dtype)

    u = jnp.maximum(_dot(h_ref[...], w1_ref[...]), 0.0)
    u = (u * u).astype(BF16)
    tn = MLP_OUT_CHUNK
    for c in range(o_ref.shape[1] // tn):
        cs = slice(c * tn, (c + 1) * tn)
        o_ref[:, cs] += gss_ref[3:4, cs] * _dot(u, w2_ref[:, cs])

    if final:
        @pl.when(f == pl.num_programs(1) - 1)
        def _():
            _rms_rows(o_ref, gss_ref.at[4:5, :], o_ref)


def _cast_slab(shape, n_steps):
    n_r, n_c = shape
    for q in (1, 2, 4, 8, 16):
        if n_steps % q or n_c % (q * 128) or n_r % (n_steps // q):
            continue
        br = n_r // (n_steps // q)
        if br % 16 == 0:
            return br, n_c // q, q
    raise ValueError(f"no bf16-tile-aligned slab split of {shape} into {n_steps} steps")


def _mlp(x, gss, w1_all, w2_all, layer, tm, tf, final, cast_next=()):
    n_rows, d = x.shape
    d_ff = w1_all.shape[2]
    n_f = d_ff // tf
    n_steps = (n_rows // tm) * n_f
    in_specs = [
        pl.BlockSpec((tm, d), lambda i, f: (i, 0)),
        pl.BlockSpec((8, d), lambda i, f: (0, 0)),
        pl.BlockSpec((None, d, tf), lambda i, f: (layer, 0, f)),
        pl.BlockSpec((None, tf, d), lambda i, f: (layer, f, 0)),
    ]
    out_specs = [pl.BlockSpec((tm, d), lambda i, f: (i, 0))]
    out_shape = [jax.ShapeDtypeStruct((n_rows, d), F32)]
    args = [x, gss, w1_all, w2_all]
    vmem = 2 * (2 * tm * d * 4 + 2 * d * tf * 2) + tm * d * 2 + tm * tf * 6 + 2 * tm * MLP_OUT_CHUNK * 4
    cast_bytes = 0
    for w_raw, src in cast_next:
        br, bc, q = _cast_slab(w_raw.shape[1:], n_steps)
        in_specs.append(pl.BlockSpec(
            (None, br, bc), lambda i, f, src=src, q=q: (src, (i * n_f + f) // q, (i * n_f + f) % q)))
        out_specs.append(pl.BlockSpec(
            (None, br, bc), lambda i, f, q=q: (0, (i * n_f + f) // q, (i * n_f + f) % q)))
        out_shape.append(jax.ShapeDtypeStruct((1,) + w_raw.shape[1:], BF16))
        args.append(w_raw)
        vmem += 2 * br * bc * 6
        cast_bytes += w_raw.shape[1] * w_raw.shape[2] * 6
    outs = pl.pallas_call(
        functools.partial(_mlp_kernel, final=final, n_cast=len(cast_next)),
        grid=(n_rows // tm, n_f),
        in_specs=in_specs,
        out_specs=out_specs,
        out_shape=out_shape,
        scratch_shapes=[pltpu.VMEM((tm, d), BF16), pltpu.VMEM((8, d), F32)],
        compiler_params=_params(("parallel", "arbitrary"), vmem),
        cost_estimate=pl.CostEstimate(
            flops=4 * n_rows * d * d_ff, transcendentals=n_rows,
            bytes_accessed=2 * n_rows * d * 4 + (n_rows // tm) * 2 * d * d_ff * 2 + cast_bytes),
        name="mlp_final" if final else "mlp",
    )(*args)
    return outs[0], tuple(outs[1:])


def _rope_tables(n_rows):
    quarter = HEAD_DIM // 4
    inv = ROPE_THETA ** (-jnp.arange(quarter, dtype=F32) / quarter)
    pos = jnp.arange(n_rows)
    rows = (pos // GRID_W).astype(F32)
    cols = (pos % GRID_W).astype(F32)
    ang = jnp.concatenate([rows[:, None] * inv[None, :]] * 2 + [cols[:, None] * inv[None, :]] * 2, axis=1)
    sign = jnp.tile(jnp.concatenate([-jnp.ones(quarter, F32), jnp.ones(quarter, F32)]), 2)
    return jnp.cos(ang), jnp.sin(ang) * sign[None, :]


def _rows8(*vecs):
    d = vecs[0].shape[0]
    pad = [jnp.zeros((d,), F32)] * (8 - len(vecs))
    return jnp.stack(list(vecs) + pad, axis=0)


def _row_tile(n_rows, want):
    return min(want, n_rows)


def kernel(x, c, ctx, c_ctx, ada_a, ada_b, ada_bias, g_mix, w_in, sink, w_f, g_attn_out, g_four_out,
           w_out, g_mlp, w1, w2, g_final):
    assert x.shape[0] == 1 and ctx.shape[0] == 1
    depth = w_in.shape[0]
    d = x.shape[2]
    xs = x[0]
    cs = ctx[0]
    n_lat, n_ctx = xs.shape[0], cs.shape[0]

    cv = jnp.zeros((ADA_ROWS, d), F32).at[0].set(c[0]).at[1].set(c_ctx)
    mods = _ada(cv, ada_a, ada_b, ada_bias)
    rope = _rope_tables(n_lat)
    fconst = _FourierConsts(n_lat, n_ctx)
    wf_b = w_f.astype(BF16)
    w_in_b, w_out_b, w1_b, w2_b = (w[0:1].astype(BF16) for w in (w_in, w_out, w1, w2))

    for l in range(depth):
        last = l == depth - 1
        m_lat = [mods[l, 0, i * d:(i + 1) * d] for i in range(N_MOD)]
        m_ctx = [mods[l, 1, i * d:(i + 1) * d] for i in range(N_MOD)]

        pc, pcf = _inproj(cs, _rows8(g_mix[l], m_ctx[0], m_ctx[1]), w_in_b, 0, None, _row_tile(n_ctx, 512), 1024)
        px, pxf = _inproj(xs, _rows8(g_mix[l], m_lat[0], m_lat[1]), w_in_b, 0, rope, _row_tile(n_lat, 512), 1024)

        na = _latent_attn(px, pc, sink[l], g_attn_out[l])
        nf = _latent_fourier(pxf, fconst, wf_b, l, g_four_out[l])
        x_mid = _outproj(na, nf, w_out_b, 0, xs, m_lat[2], _row_tile(n_lat, 1024), 1024)
        gss = _rows8(g_mlp[l], m_lat[3], m_lat[4], m_lat[5], g_final)
        cast_next = () if last else tuple((w, l + 1) for w in (w_in, w_out, w1, w2))
        xs, next_weights = _mlp(x_mid, gss, w1_b, w2_b, 0, _row_tile(n_lat, 512), 512, final=last,
                                cast_next=cast_next)

        if not last:
            nac = _ctx_attn(pc, sink[l], g_attn_out[l])
            nfc = _ctx_fourier(pcf, fconst, wf_b, l, g_four_out[l])
            c_mid = _outproj(nac, nfc, w_out_b, 0, cs, m_ctx[2], _row_tile(n_ctx, 1024), 1024)
            gss_c = _rows8(g_mlp[l], m_ctx[3], m_ctx[4], m_ctx[5])
            cs, _ = _mlp(c_mid, gss_c, w1_b, w2_b, 0, _row_tile(n_ctx, 512), 512, final=False)
            w_in_b, w_out_b, w1_b, w2_b = next_weights

    return xs[None]
```

```python
import functools
import math

import numpy as np
import jax
import jax.numpy as jnp
from jax import lax
from jax.experimental import pallas as pl
from jax.experimental.pallas import tpu as pltpu

F32 = jnp.float32
BF16 = jnp.bfloat16

N_HEADS = 16
N_KV_HEADS = 4
HEAD_DIM = 128
GROUP = N_HEADS // N_KV_HEADS
Q_WIDTH = N_HEADS * HEAD_DIM
KV_WIDTH = N_KV_HEADS * HEAD_DIM
BLOCK = 128
GRID_W = 64
N_FGROUPS = 4
F_DIM = 512
F_WIDTH = N_FGROUPS * F_DIM
N_MOD = 6
ROPE_THETA = 10000.0
EPS = 1e-6
NEG_INF = -1e30
LOG2E = 1.4426950408889634

V7X_VMEM_BYTES = 64 * 1024 * 1024
ADA_ROWS = 8
MLP_OUT_CHUNK = 512
ROW_CHUNK = 16
ROW_UNROLL = 2
STREAM_UNROLL = 8
DOT_COLS = 256
SOFTMAX_ROWS = 32
ATTN_Q_BLOCKS = 2
HEAD_RESIDUES = 8
TAIL_BLOCKS = 4


def _params(semantics, vmem_bytes):
    limit = min(int(vmem_bytes * 1.15) + (4 << 20), V7X_VMEM_BYTES - (6 << 20))
    return pltpu.CompilerParams(dimension_semantics=semantics, vmem_limit_bytes=limit)


def _dot(a, b):
    return jnp.dot(a, b, preferred_element_type=F32)


def _dot_nt(a, b):
    return lax.dot_general(a, b, (((1,), (1,)), ((), ())), preferred_element_type=F32)


def _rms(x, g):
    return x * lax.rsqrt(jnp.mean(x * x, axis=-1, keepdims=True) + EPS) * g


def _for_row_chunks(n_rows, body, unroll=ROW_UNROLL):
    rc = min(ROW_CHUNK, n_rows)

    def step(r, carry):
        body(pl.ds(pl.multiple_of(r * rc, rc), rc))
        return carry

    lax.fori_loop(0, n_rows // rc, step, 0, unroll=min(unroll, n_rows // rc))


def _normmod_rows(x_ref, gss_ref, ops_ref, h_ref, copy_ref=None):
    ops_ref[0:1, :] = 1.0 + gss_ref[2:3, :]

    def body(rows):
        x = x_ref[rows, :]
        y = _rms(x, gss_ref[0:1, :])
        h_ref[rows, :] = (y * ops_ref[0:1, :] + gss_ref[1:2, :]).astype(h_ref.dtype)
        if copy_ref is not None:
            copy_ref[rows, :] = x

    _for_row_chunks(x_ref.shape[0], body)


def _rms_rows(src_ref, g_ref, dst_ref):
    def body(rows):
        dst_ref[rows, :] = _rms(src_ref[rows, :], g_ref[...]).astype(dst_ref.dtype)

    _for_row_chunks(src_ref.shape[0], body)


def _rms_rows_from_sumsq(src_ref, ss_ref, g_ref, dst_ref):
    width = src_ref.shape[1]

    def body(rows):
        inv = lax.rsqrt(ss_ref[rows, :] * (1.0 / width) + EPS)
        dst_ref[rows, :] = (src_ref[rows, :] * inv * g_ref[...]).astype(dst_ref.dtype)

    _for_row_chunks(src_ref.shape[0], body, unroll=STREAM_UNROLL)


def _sumsq(x):
    return jnp.sum(x * x, axis=-1, keepdims=True)


def _ada_kernel(cv_ref, a_ref, b_ref, bias_ref, o_ref, h_ref):
    @pl.when(pl.program_id(1) == 0)
    def _():
        cv = cv_ref[...]
        s = cv * jax.nn.sigmoid(cv)
        h_ref[...] = _dot(s.astype(BF16), a_ref[0].astype(BF16))

    o_ref[0] = _dot(h_ref[...].astype(BF16), b_ref[0].astype(BF16)) + bias_ref[0]


def _ada(cv, ada_a, ada_b, ada_bias):
    depth, d, rank = ada_a.shape
    n = ada_b.shape[2]
    tn = d
    vmem = 2 * (d * rank * 4 + rank * tn * 4) + 4 * ADA_ROWS * (d + tn) * 4
    return pl.pallas_call(
        _ada_kernel,
        grid=(depth, n // tn),
        in_specs=[
            pl.BlockSpec((ADA_ROWS, d), lambda l, j: (0, 0)),
            pl.BlockSpec((1, d, rank), lambda l, j: (l, 0, 0)),
            pl.BlockSpec((1, rank, tn), lambda l, j: (l, 0, j)),
            pl.BlockSpec((1, 1, tn), lambda l, j: (l, 0, j)),
        ],
        out_specs=pl.BlockSpec((1, ADA_ROWS, tn), lambda l, j: (l, 0, j)),
        out_shape=jax.ShapeDtypeStruct((depth, ADA_ROWS, n), F32),
        scratch_shapes=[pltpu.VMEM((ADA_ROWS, rank), F32)],
        compiler_params=_params(("parallel", "arbitrary"), vmem),
        name="ada",
    )(cv, ada_a, ada_b, ada_bias.reshape(depth, 1, n))


def _rope_chunk(a, cos, sin, lane):
    up = pltpu.roll(a, HEAD_DIM - 32, axis=1)
    down = pltpu.roll(a, 32, axis=1)
    partner = jnp.where((lane % 64) < 32, up, down)
    return a * cos + partner * sin


def _normmod_kernel(x_ref, gss_ref, h_ref, ops_ref):
    _normmod_rows(x_ref, gss_ref, ops_ref, h_ref)


def _normmod(x, gss, tm):
    n_rows, d = x.shape
    return pl.pallas_call(
        _normmod_kernel,
        grid=(n_rows // tm,),
        in_specs=[pl.BlockSpec((tm, d), lambda i: (i, 0)), pl.BlockSpec((8, d), lambda i: (0, 0))],
        out_specs=pl.BlockSpec((tm, d), lambda i: (i, 0)),
        out_shape=jax.ShapeDtypeStruct((n_rows, d), BF16),
        scratch_shapes=[pltpu.VMEM((8, d), F32)],
        compiler_params=_params(("parallel",), 2 * tm * d * 6 + 16 * d * 4),
        cost_estimate=pl.CostEstimate(flops=8 * n_rows * d, transcendentals=n_rows, bytes_accessed=n_rows * d * 6),
        name="normmod",
    )(x, gss)


def _inproj_kernel(h_ref, w_ref, *rest, rope_cols, tn):
    if rope_cols:
        cos_ref, sin_ref, qkv_ref, f_ref = rest
    else:
        qkv_ref, f_ref = rest
    j = pl.program_id(1)
    heads_per_dot = DOT_COLS // HEAD_DIM

    def tile(n_rope, o_ref):
        if n_rope:
            lane = lax.broadcasted_iota(jnp.int32, (h_ref.shape[0], HEAD_DIM), 1)
        for c in range(tn // DOT_COLS):
            acc = _dot(h_ref[...], w_ref[:, c * DOT_COLS:(c + 1) * DOT_COLS])
            for k in range(heads_per_dot):
                head = c * heads_per_dot + k
                a = acc[:, k * HEAD_DIM:(k + 1) * HEAD_DIM]
                if head < n_rope:
                    a = _rope_chunk(a, cos_ref[...], sin_ref[...], lane)
                o_ref[:, head * HEAD_DIM:(head + 1) * HEAD_DIM] = a.astype(o_ref.dtype)

    qkv_tiles = (Q_WIDTH + 2 * KV_WIDTH) // tn
    full_tiles, part = divmod(rope_cols, tn)
    if full_tiles:
        pl.when(j < full_tiles)(functools.partial(tile, tn // HEAD_DIM, qkv_ref))
    if part:
        pl.when(j == full_tiles)(functools.partial(tile, part // HEAD_DIM, qkv_ref))
        full_tiles += 1
    if full_tiles < qkv_tiles:
        pl.when((j >= full_tiles) & (j < qkv_tiles))(functools.partial(tile, 0, qkv_ref))
    pl.when(j >= qkv_tiles)(functools.partial(tile, 0, f_ref))


def _inproj(h, w_all, layer, rope_tables, tm, tn):
    n_rows, d = h.shape
    n_out = w_all.shape[2]
    qkv_cols = Q_WIDTH + 2 * KV_WIDTH
    qkv_tiles = qkv_cols // tn
    rope_cols = Q_WIDTH + KV_WIDTH if rope_tables is not None else 0
    in_specs = [
        pl.BlockSpec((tm, d), lambda i, j: (i, 0)),
        pl.BlockSpec((None, d, tn), lambda i, j: (layer, 0, j)),
    ]
    args = [h, w_all]
    if rope_tables is not None:
        in_specs += [pl.BlockSpec((tm, HEAD_DIM), lambda i, j: (i, 0))] * 2
        args += list(rope_tables)
    vmem = 2 * (tm * d * 2 + d * tn * 2 + 2 * tm * tn * 2 + 2 * tm * HEAD_DIM * 4) + 4 * tm * DOT_COLS * 4
    cost = pl.CostEstimate(flops=2 * n_rows * d * n_out, transcendentals=0,
                           bytes_accessed=n_rows * d * 2 + (n_rows // tm) * d * n_out * 2 + n_rows * n_out * 2)
    return pl.pallas_call(
        functools.partial(_inproj_kernel, rope_cols=rope_cols, tn=tn),
        grid=(n_rows // tm, n_out // tn),
        in_specs=in_specs,
        out_specs=[
            pl.BlockSpec((tm, tn), lambda i, j: (i, jnp.minimum(j, qkv_tiles - 1))),
            pl.BlockSpec((tm, tn), lambda i, j: (i, jnp.maximum(j - qkv_tiles, 0))),
        ],
        out_shape=[
            jax.ShapeDtypeStruct((n_rows, qkv_cols), BF16),
            jax.ShapeDtypeStruct((n_rows, n_out - qkv_cols), BF16),
        ],
        compiler_params=_params(("parallel", "arbitrary"), vmem),
        cost_estimate=cost,
        name="inproj_rope" if rope_cols else "inproj_ctx",
    )(*args)


def _stack_heads(q_ref, rows, h):
    return jnp.concatenate(
        [q_ref[rows, (h * GROUP + g) * HEAD_DIM:(h * GROUP + g + 1) * HEAD_DIM] for g in range(GROUP)], axis=0)


def _softmax_chunk(s_ref, p_ref, rows, sink, masks):
    n_tiles = s_ref.shape[1] // BLOCK
    tiles = []
    for t in range(n_tiles):
        v = s_ref[rows, t * BLOCK:(t + 1) * BLOCK]
        if masks.get(t) is not None:
            v = jnp.where(masks[t], v, NEG_INF)
        tiles.append(v)
    m_raw = jnp.max(functools.reduce(jnp.maximum, tiles), axis=-1, keepdims=True)
    m = jnp.maximum(m_raw * (HEAD_DIM ** -0.5), sink)
    mb = m * LOG2E
    es = [jnp.exp2(v * (HEAD_DIM ** -0.5 * LOG2E) - mb) for v in tiles]
    denom = jnp.sum(functools.reduce(jnp.add, es), axis=-1, keepdims=True) + jnp.exp2(sink * LOG2E - mb)
    inv = 1.0 / denom
    for t in range(n_tiles):
        p_ref[rows, t * BLOCK:(t + 1) * BLOCK] = (es[t] * inv).astype(p_ref.dtype)


def _attn_units(sink_ref, q_ref, units, acc_ref, ss_ref, s_ref, p_ref):
    def scores(u):
        q_rows, h, keys, _, _ = units[u]
        s_ref[u % 2] = _dot_nt(_stack_heads(q_ref, q_rows, h), keys())

    scores(0)
    for u, (q_rows, h, _, values, mask_fn) in enumerate(units):
        if u + 1 < len(units):
            scores(u + 1)
        n_q = q_rows.stop - q_rows.start
        chunks_per_head = n_q // SOFTMAX_ROWS
        for r in range(GROUP * chunks_per_head):
            rows = slice(r * SOFTMAX_ROWS, (r + 1) * SOFTMAX_ROWS)
            sink = sink_ref[h * GROUP + r // chunks_per_head]
            q0 = (r % chunks_per_head) * SOFTMAX_ROWS
            _softmax_chunk(s_ref.at[u % 2], p_ref.at[u % 2], rows, sink, mask_fn(q0))
        o = _dot(p_ref[u % 2], values())
        ss = None
        for g in range(GROUP):
            c0 = (h * GROUP + g) * HEAD_DIM
            og = o[g * n_q:(g + 1) * n_q, :]
            acc_ref[q_rows, c0:c0 + HEAD_DIM] = og
            ss = _sumsq(og) if ss is None else ss + _sumsq(og)
        ss_ref[q_rows, :] = ss if h == 0 else ss_ref[q_rows, :] + ss


def _latent_attn_kernel(sink_ref, q_ref, kp_ref, kc_ref, kn_ref, vp_ref, vc_ref, vn_ref,
                        kx_ref, vx_ref, g_ref, o_ref, acc_ref, ss_ref, s_ref, p_ref, *, n_ctx, q_blocks):
    n = pl.program_id(0)
    nb = pl.num_programs(0)
    ctx_tiles = n_ctx // BLOCK
    prev_lo = jnp.where(n == 0, BLOCK, 0)
    next_hi = jnp.where(n == nb - 1, 0, BLOCK)

    def block_of(refs, i):
        prev_ref, cur_ref, next_ref = refs
        if i == 0:
            return prev_ref, slice(0, BLOCK)
        if i == q_blocks + 1:
            return next_ref, slice(0, BLOCK)
        return cur_ref, slice((i - 1) * BLOCK, i * BLOCK)

    def operand(ctx_ref, refs, qb, h):
        hs = slice(h * HEAD_DIM, (h + 1) * HEAD_DIM)
        parts = [ctx_ref[:, hs]]
        for i in range(qb, qb + 3):
            ref, rows = block_of(refs, i)
            parts.append(ref[rows, hs])
        return jnp.concatenate(parts, axis=0)

    def mask_fn(qb, q0):
        qi = lax.broadcasted_iota(jnp.int32, (SOFTMAX_ROWS, BLOCK), 0) + q0
        kj = lax.broadcasted_iota(jnp.int32, (SOFTMAX_ROWS, BLOCK), 1)
        lo = kj >= qi
        hi = kj <= qi
        if qb == 0:
            lo = lo & (kj >= prev_lo)
        if qb == q_blocks - 1:
            hi = hi & (kj < next_hi)
        return {ctx_tiles: lo, ctx_tiles + 2: hi}

    units = []
    for qb in range(q_blocks):
        for h in range(N_KV_HEADS):
            units.append((
                slice(qb * BLOCK, (qb + 1) * BLOCK), h,
                functools.partial(operand, kx_ref, (kp_ref, kc_ref, kn_ref), qb, h),
                functools.partial(operand, vx_ref, (vp_ref, vc_ref, vn_ref), qb, h),
                functools.partial(mask_fn, qb),
            ))
    _attn_units(sink_ref, q_ref, units, acc_ref, ss_ref, s_ref, p_ref)
    _rms_rows_from_sumsq(acc_ref, ss_ref, g_ref, o_ref)


def _latent_attn(px, pc, sink, g_attn):
    n_rows = px.shape[0]
    n_ctx = pc.shape[0]
    nb = n_rows // BLOCK
    qb = ATTN_Q_BLOCKS if nb % ATTN_Q_BLOCKS == 0 else 1
    kcol = Q_WIDTH // KV_WIDTH
    vcol = kcol + 1
    n_keys = n_ctx + 3 * BLOCK

    def kv_specs(colblk):
        return [
            pl.BlockSpec((BLOCK, KV_WIDTH), lambda n: (jnp.maximum(n * qb - 1, 0), colblk)),
            pl.BlockSpec((qb * BLOCK, KV_WIDTH), lambda n: (n, colblk)),
            pl.BlockSpec((BLOCK, KV_WIDTH), lambda n: (jnp.minimum(n * qb + qb, nb - 1), colblk)),
        ]

    in_specs = [
        pl.BlockSpec(memory_space=pltpu.SMEM),
        pl.BlockSpec((qb * BLOCK, Q_WIDTH), lambda n: (n, 0)),
        *kv_specs(kcol), *kv_specs(vcol),
        pl.BlockSpec((n_ctx, KV_WIDTH), lambda n: (0, kcol)),
        pl.BlockSpec((n_ctx, KV_WIDTH), lambda n: (0, vcol)),
        pl.BlockSpec((1, Q_WIDTH), lambda n: (0, 0)),
    ]
    return pl.pallas_call(
        functools.partial(_latent_attn_kernel, n_ctx=n_ctx, q_blocks=qb),
        grid=(nb // qb,),
        in_specs=in_specs,
        out_specs=pl.BlockSpec((qb * BLOCK, Q_WIDTH), lambda n: (n, 0)),
        out_shape=jax.ShapeDtypeStruct((n_rows, Q_WIDTH), BF16),
        scratch_shapes=[
            pltpu.VMEM((qb * BLOCK, Q_WIDTH), F32),
            pltpu.VMEM((qb * BLOCK, 1), F32),
            pltpu.VMEM((2, GROUP * BLOCK, n_keys), F32),
            pltpu.VMEM((2, GROUP * BLOCK, n_keys), BF16),
        ],
        compiler_params=_params(("parallel",), 24 << 20),
        cost_estimate=pl.CostEstimate(
            flops=4 * n_rows * N_HEADS * n_keys * HEAD_DIM, transcendentals=n_rows * N_HEADS * n_keys,
            bytes_accessed=n_rows * (2 * Q_WIDTH + 6 * KV_WIDTH) * 2),
        name="latent_attn",
    )(sink, px, px, px, px, px, px, px, pc, pc, g_attn.reshape(1, Q_WIDTH))


def _ctx_attn_kernel(sink_ref, q_ref, k_ref, v_ref, g_ref, o_ref, acc_ref, ss_ref, s_ref, p_ref):
    def operand(ref, h):
        return ref[:, h * HEAD_DIM:(h + 1) * HEAD_DIM]

    units = [(slice(0, q_ref.shape[0]), h, functools.partial(operand, k_ref, h),
              functools.partial(operand, v_ref, h), lambda q0: {}) for h in range(N_KV_HEADS)]
    _attn_units(sink_ref, q_ref, units, acc_ref, ss_ref, s_ref, p_ref)
    _rms_rows_from_sumsq(acc_ref, ss_ref, g_ref, o_ref)


def _ctx_attn(pc, sink, g_attn):
    n_ctx = pc.shape[0]
    kcol = Q_WIDTH // KV_WIDTH
    return pl.pallas_call(
        _ctx_attn_kernel,
        grid=(1,),
        in_specs=[
            pl.BlockSpec(memory_space=pltpu.SMEM),
            pl.BlockSpec((n_ctx, Q_WIDTH), lambda i: (0, 0)),
            pl.BlockSpec((n_ctx, KV_WIDTH), lambda i: (0, kcol)),
            pl.BlockSpec((n_ctx, KV_WIDTH), lambda i: (0, kcol + 1)),
            pl.BlockSpec((1, Q_WIDTH), lambda i: (0, 0)),
        ],
        out_specs=pl.BlockSpec((n_ctx, Q_WIDTH), lambda i: (0, 0)),
        out_shape=jax.ShapeDtypeStruct((n_ctx, Q_WIDTH), BF16),
        scratch_shapes=[
            pltpu.VMEM((n_ctx, Q_WIDTH), F32),
            pltpu.VMEM((n_ctx, 1), F32),
            pltpu.VMEM((2, GROUP * n_ctx, n_ctx), F32),
            pltpu.VMEM((2, GROUP * n_ctx, n_ctx), BF16),
        ],
        compiler_params=_params(("arbitrary",), 16 << 20),
        name="ctx_attn",
    )(sink, pc, pc, pc, g_attn.reshape(1, Q_WIDTH))


def _dft_tables(n):
    ang = 2.0 * np.pi * (np.outer(np.arange(n), np.arange(n)) % n) / n
    return np.cos(ang), np.sin(ang)


def _fourier_head_kernel(g_ref, z_ref, o_ref):
    for r in range(z_ref.shape[0]):
        o_ref[r] = _dot(g_ref[r], z_ref[r]).astype(o_ref.dtype)


def _fourier_head(zt, gmat):
    nb, na, c = zt.shape
    rb = min(HEAD_RESIDUES, nb)
    return pl.pallas_call(
        _fourier_head_kernel,
        grid=(nb // rb,),
        in_specs=[
            pl.BlockSpec((rb, 2 * na, na), lambda b: (b, 0, 0)),
            pl.BlockSpec((rb, na, c), lambda b: (b, 0, 0)),
        ],
        out_specs=pl.BlockSpec((rb, 2 * na, c), lambda b: (b, 0, 0)),
        out_shape=jax.ShapeDtypeStruct((nb, 2 * na, c), BF16),
        compiler_params=_params(("parallel",), 6 * rb * na * c * 2),
        cost_estimate=pl.CostEstimate(
            flops=4 * nb * na * na * c, transcendentals=0,
            bytes_accessed=(3 * nb * na * c + 2 * nb * na * na) * 2),
        name="fourier_head",
    )(gmat, zt)


def _fourier_tail_kernel(m_ref, d_ref, cs_ref, wf_ref, g_ref, o_ref, acc_ref, ss_ref, *, pos_scale):
    kb, p = o_ref.shape[0], o_ref.shape[1]
    xs = [(_dot(m_ref[...], d_ref[blk]) * pos_scale).astype(BF16) for blk in range(kb)]
    lhs = jnp.concatenate(
        [jnp.concatenate([x[:p, g * F_DIM:(g + 1) * F_DIM], x[p:, g * F_DIM:(g + 1) * F_DIM]], axis=1)
         for g in range(N_FGROUPS) for x in xs], axis=0)
    f = (_dot(lhs, cs_ref[...]) * (F_DIM ** -0.5)).astype(BF16)
    for g in range(N_FGROUPS):
        og = _dot(f[g * kb * p:(g + 1) * kb * p, :], wf_ref[g])
        for blk in range(kb):
            piece = og[blk * p:(blk + 1) * p, :]
            acc_ref[blk, :, g * F_DIM:(g + 1) * F_DIM] = piece
            ss_ref[blk] = _sumsq(piece) if g == 0 else ss_ref[blk] + _sumsq(piece)
    for blk in range(kb):
        _rms_rows_from_sumsq(acc_ref.at[blk], ss_ref.at[blk], g_ref, o_ref.at[blk])


def _fourier_tail(stage_mat, data, cs, wf_all, layer, g_four, pos_scale):
    nblk, k_in, c = data.shape
    p = stage_mat.shape[0] // 2
    kb = min(TAIL_BLOCKS, nblk)
    temps = kb * p * c * (4 + 2 + 2 + 4 + 2)
    vmem = 2 * kb * (k_in + p) * c * 2 + kb * p * c * 4 + 2 * (2 * p * k_in + 6 * F_DIM * F_DIM) * 2 + temps
    return pl.pallas_call(
        functools.partial(_fourier_tail_kernel, pos_scale=pos_scale),
        grid=(nblk // kb,),
        in_specs=[
            pl.BlockSpec((2 * p, k_in), lambda i: (0, 0)),
            pl.BlockSpec((kb, k_in, c), lambda i: (i, 0, 0)),
            pl.BlockSpec((2 * F_DIM, F_DIM), lambda i: (0, 0)),
            pl.BlockSpec((None, N_FGROUPS, F_DIM, F_DIM), lambda i: (layer, 0, 0, 0)),
            pl.BlockSpec((1, c), lambda i: (0, 0)),
        ],
        out_specs=pl.BlockSpec((kb, p, c), lambda i: (i, 0, 0)),
        out_shape=jax.ShapeDtypeStruct((nblk, p, c), BF16),
        scratch_shapes=[pltpu.VMEM((kb, p, c), F32), pltpu.VMEM((kb, p, 1), F32)],
        compiler_params=_params(("parallel",), vmem),
        cost_estimate=pl.CostEstimate(
            flops=2 * nblk * (2 * p * k_in * c + p * c * 2 * F_DIM + p * c * F_DIM), transcendentals=nblk * p,
            bytes_accessed=nblk * (k_in + p) * c * 2 + 2 * p * k_in * 2 + 6 * F_DIM * F_DIM * 2),
        name="fourier_tail",
    )(stage_mat, data, cs, wf_all, g_four.reshape(1, c))


class _FourierConsts:
    def __init__(self, n_lat, n_ctx):
        a = b = int(round(math.sqrt(n_lat)))
        assert a * b == n_lat
        self.a, self.b = a, b
        k_lo = np.arange(a)[None, :, None]
        n = (np.arange(a)[None, None, :] * b + np.arange(b)[:, None, None])
        ang = 2.0 * np.pi * ((k_lo * n) % n_lat) / n_lat
        self.head = jnp.asarray(np.concatenate([np.cos(ang), -np.sin(ang)], axis=1), BF16)
        cb, sb = _dft_tables(b)
        self.tail = jnp.asarray(np.block([[cb, sb], [-sb, cb]]), BF16)
        cc, sc = _dft_tables(n_ctx)
        self.ctx = jnp.asarray(np.concatenate([cc, -sc], axis=0), BF16)
        cf, sf = _dft_tables(F_DIM)
        self.chan = jnp.asarray(np.concatenate([cf, sf], axis=0), BF16)


def _latent_fourier(pf, fc, wf_all, layer, g_four):
    n_rows = pf.shape[0]
    a, b = fc.a, fc.b
    zt = jnp.transpose(pf.reshape(a, b, F_WIDTH), (1, 0, 2))
    t = _fourier_head(zt, fc.head)
    t = jnp.transpose(t.reshape(b, 2, a, F_WIDTH), (2, 1, 0, 3)).reshape(a, 2 * b, F_WIDTH)
    o = _fourier_tail(fc.tail, t, fc.chan, wf_all, layer, g_four, 1.0 / math.sqrt(n_rows))
    return jnp.transpose(o, (1, 0, 2)).reshape(n_rows, F_WIDTH)


def _ctx_fourier(pf, fc, wf_all, layer, g_four):
    n_ctx = pf.shape[0]
    d = pf.reshape(1, n_ctx, F_WIDTH)
    o = _fourier_tail(fc.ctx, d, fc.chan, wf_all, layer, g_four, 1.0 / math.sqrt(n_ctx))
    return o.reshape(n_ctx, F_WIDTH)


def _outproj_kernel(na_ref, nf_ref, w_ref, x_ref, gate_ref, o_ref):
    ka = na_ref.shape[1]
    acc = _dot(na_ref[...], w_ref[:ka, :]) + _dot(nf_ref[...], w_ref[ka:, :])
    o_ref[...] = x_ref[...] + gate_ref[...] * acc


def _outproj(na, nf, w_all, layer, x, gate, tm, tn):
    n_rows, d = x.shape
    ka, kf = na.shape[1], nf.shape[1]
    vmem = 2 * (tm * (ka + kf) * 2 + (ka + kf) * tn * 2 + 2 * tm * tn * 4) + tm * tn * 4
    return pl.pallas_call(
        _outproj_kernel,
        grid=(n_rows // tm, d // tn),
        in_specs=[
            pl.BlockSpec((tm, ka), lambda i, j: (i, 0)),
            pl.BlockSpec((tm, kf), lambda i, j: (i, 0)),
            pl.BlockSpec((None, ka + kf, tn), lambda i, j: (layer, 0, j)),
            pl.BlockSpec((tm, tn), lambda i, j: (i, j)),
            pl.BlockSpec((1, tn), lambda i, j: (0, j)),
        ],
        out_specs=pl.BlockSpec((tm, tn), lambda i, j: (i, j)),
        out_shape=jax.ShapeDtypeStruct((n_rows, d), F32),
        compiler_params=_params(("parallel", "parallel"), vmem),
        cost_estimate=pl.CostEstimate(
            flops=2 * n_rows * (ka + kf) * d, transcendentals=0,
            bytes_accessed=n_rows * (ka + kf) * 2 + (n_rows // tm) * (ka + kf) * d * 2 + 2 * n_rows * d * 4),
        name="outproj",
    )(na, nf, w_all, x, gate.reshape(1, d))


def _mlp_kernel(x_ref, gss_ref, w1_ref, w2_ref, *rest, final, n_cast):
    raw_refs = rest[:n_cast]
    o_ref = rest[n_cast]
    cast_refs = rest[n_cast + 1:2 * n_cast + 1]
    h_ref, ops_ref = rest[2 * n_cast + 1:]
    f = pl.program_id(1)

    @pl.when(f == 0)
    def _():
        _normmod_rows(x_ref, gss_ref, ops_ref, h_ref, copy_ref=o_ref)

    for raw_ref, cast_ref in zip(raw_refs, cast_refs):
        cast_ref[...] = raw_ref[...].astype(cast_ref.dtype)

    u = jnp.maximum(_dot(h_ref[...], w1_ref[...]), 0.0)
    u = (u * u).astype(BF16)
    tn = MLP_OUT_CHUNK
    for c in range(o_ref.shape[1] // tn):
        cs = slice(c * tn, (c + 1) * tn)
        o_ref[:, cs] += gss_ref[3:4, cs] * _dot(u, w2_ref[:, cs])

    if final:
        @pl.when(f == pl.num_programs(1) - 1)
        def _():
            _rms_rows(o_ref, gss_ref.at[4:5, :], o_ref)


def _cast_slab(shape, n_steps):
    n_r, n_c = shape
    for q in (1, 2, 4, 8, 16):
        if n_steps % q or n_c % (q * 128) or n_r % (n_steps // q):
            continue
        br = n_r // (n_steps // q)
        if br % 16 == 0:
            return br, n_c // q, q
    raise ValueError(f"no bf16-tile-aligned slab split of {shape} into {n_steps} steps")


def _mlp(x, gss, w1_all, w2_all, layer, tm, tf, final, cast_next=()):
    n_rows, d = x.shape
    d_ff = w1_all.shape[2]
    n_f = d_ff // tf
    n_steps = (n_rows // tm) * n_f
    in_specs = [
        pl.BlockSpec((tm, d), lambda i, f: (i, 0)),
        pl.BlockSpec((8, d), lambda i, f: (0, 0)),
        pl.BlockSpec((None, d, tf), lambda i, f: (layer, 0, f)),
        pl.BlockSpec((None, tf, d), lambda i, f: (layer, f, 0)),
    ]
    out_specs = [pl.BlockSpec((tm, d), lambda i, f: (i, 0))]
    out_shape = [jax.ShapeDtypeStruct((n_rows, d), F32)]
    args = [x, gss, w1_all, w2_all]
    vmem = 2 * (2 * tm * d * 4 + 2 * d * tf * 2) + tm * d * 2 + tm * tf * 6 + 2 * tm * MLP_OUT_CHUNK * 4
    cast_bytes = 0
    for w_raw, src in cast_next:
        br, bc, q = _cast_slab(w_raw.shape[1:], n_steps)
        in_specs.append(pl.BlockSpec(
            (None, br, bc), lambda i, f, src=src, q=q: (src, (i * n_f + f) // q, (i * n_f + f) % q)))
        out_specs.append(pl.BlockSpec(
            (None, br, bc), lambda i, f, q=q: (0, (i * n_f + f) // q, (i * n_f + f) % q)))
        out_shape.append(jax.ShapeDtypeStruct((1,) + w_raw.shape[1:], BF16))
        args.append(w_raw)
        vmem += 2 * br * bc * 6
        cast_bytes += w_raw.shape[1] * w_raw.shape[2] * 6
    outs = pl.pallas_call(
        functools.partial(_mlp_kernel, final=final, n_cast=len(cast_next)),
        grid=(n_rows // tm, n_f),
        in_specs=in_specs,
        out_specs=out_specs,
        out_shape=out_shape,
        scratch_shapes=[pltpu.VMEM((tm, d), BF16), pltpu.VMEM((8, d), F32)],
        compiler_params=_params(("parallel", "arbitrary"), vmem),
        cost_estimate=pl.CostEstimate(
            flops=4 * n_rows * d * d_ff, transcendentals=n_rows,
            bytes_accessed=2 * n_rows * d * 4 + (n_rows // tm) * 2 * d * d_ff * 2 + cast_bytes),
        name="mlp_final" if final else "mlp",
    )(*args)
    return outs[0], tuple(outs[1:])


def _rope_tables(n_rows):
    quarter = HEAD_DIM // 4
    inv = ROPE_THETA ** (-jnp.arange(quarter, dtype=F32) / quarter)
    pos = jnp.arange(n_rows)
    rows = (pos // GRID_W).astype(F32)
    cols = (pos % GRID_W).astype(F32)
    ang = jnp.concatenate([rows[:, None] * inv[None, :]] * 2 + [cols[:, None] * inv[None, :]] * 2, axis=1)
    sign = jnp.tile(jnp.concatenate([-jnp.ones(quarter, F32), jnp.ones(quarter, F32)]), 2)
    return jnp.cos(ang), jnp.sin(ang) * sign[None, :]


def _rows8(*vecs):
    d = vecs[0].shape[0]
    pad = [jnp.zeros((d,), F32)] * (8 - len(vecs))
    return jnp.stack(list(vecs) + pad, axis=0)


def _row_tile(n_rows, want):
    return min(want, n_rows)


def kernel(x, c, ctx, c_ctx, ada_a, ada_b, ada_bias, g_mix, w_in, sink, w_f, g_attn_out, g_four_out,
           w_out, g_mlp, w1, w2, g_final):
    assert x.shape[0] == 1 and ctx.shape[0] == 1
    depth = w_in.shape[0]
    d = x.shape[2]
    xs = x[0]
    cs = ctx[0]
    n_lat, n_ctx = xs.shape[0], cs.shape[0]

    cv = jnp.zeros((ADA_ROWS, d), F32).at[0].set(c[0]).at[1].set(c_ctx)
    mods = _ada(cv, ada_a, ada_b, ada_bias)
    rope = _rope_tables(n_lat)
    fconst = _FourierConsts(n_lat, n_ctx)
    wf_b = w_f.astype(BF16)
    w_in_b, w_out_b, w1_b, w2_b = (w[0:1].astype(BF16) for w in (w_in, w_out, w1, w2))

    for l in range(depth):
        last = l == depth - 1
        m_lat = [mods[l, 0, i * d:(i + 1) * d] for i in range(N_MOD)]
        m_ctx = [mods[l, 1, i * d:(i + 1) * d] for i in range(N_MOD)]

        hc = _normmod(cs, _rows8(g_mix[l], m_ctx[0], m_ctx[1]), _row_tile(n_ctx, 256))
        hx = _normmod(xs, _rows8(g_mix[l], m_lat[0], m_lat[1]), _row_tile(n_lat, 256))
        pc, pcf = _inproj(hc, w_in_b, 0, None, _row_tile(n_ctx, 1024), 1024)
        px, pxf = _inproj(hx, w_in_b, 0, rope, _row_tile(n_lat, 1024), 1024)

        na = _latent_attn(px, pc, sink[l], g_attn_out[l])
        nf = _latent_fourier(pxf, fconst, wf_b, l, g_four_out[l])
        x_mid = _outproj(na, nf, w_out_b, 0, xs, m_lat[2], _row_tile(n_lat, 1024), 1024)
        gss = _rows8(g_mlp[l], m_lat[3], m_lat[4], m_lat[5], g_final)
        cast_next = () if last else tuple((w, l + 1) for w in (w_in, w_out, w1, w2))
        xs, next_weights = _mlp(x_mid, gss, w1_b, w2_b, 0, _row_tile(n_lat, 512), 512, final=last,
                                cast_next=cast_next)

        if not last:
            nac = _ctx_attn(pc, sink[l], g_attn_out[l])
            nfc = _ctx_fourier(pcf, fconst, wf_b, l, g_four_out[l])
            c_mid = _outproj(nac, nfc, w_out_b, 0, cs, m_ctx[2], _row_tile(n_ctx, 1024), 1024)
            gss_c = _rows8(g_mlp[l], m_ctx[3], m_ctx[4], m_ctx[5])
            cs, _ = _mlp(c_mid, gss_c, w1_b, w2_b, 0, _row_tile(n_ctx, 512), 512, final=False)
            w_in_b, w_out_b, w1_b, w2_b = next_weights

    return xs[None]
```

```python
import functools
import math

import numpy as np
import jax
import jax.numpy as jnp
from jax import lax
from jax.experimental import pallas as pl
from jax.experimental.pallas import tpu as pltpu

F32 = jnp.float32
BF16 = jnp.bfloat16

N_HEADS = 16
N_KV_HEADS = 4
HEAD_DIM = 128
GROUP = N_HEADS // N_KV_HEADS
Q_WIDTH = N_HEADS * HEAD_DIM
KV_WIDTH = N_KV_HEADS * HEAD_DIM
BLOCK = 128
GRID_W = 64
N_FGROUPS = 4
F_DIM = 512
F_WIDTH = N_FGROUPS * F_DIM
N_MOD = 6
ROPE_THETA = 10000.0
EPS = 1e-6
NEG_INF = -1e30
LOG2E = 1.4426950408889634

V7X_VMEM_BYTES = 64 * 1024 * 1024
ADA_ROWS = 8
MLP_OUT_CHUNK = 512
ROW_CHUNK = 16
ROW_UNROLL = 2
STREAM_UNROLL = 8
DOT_COLS = 256
SOFTMAX_ROWS = 32
ATTN_Q_BLOCKS = 2
HEAD_RESIDUES = 8
TAIL_BLOCKS = 4


def _params(semantics, vmem_bytes):
    limit = min(int(vmem_bytes * 1.15) + (4 << 20), V7X_VMEM_BYTES - (6 << 20))
    return pltpu.CompilerParams(dimension_semantics=semantics, vmem_limit_bytes=limit)


def _dot(a, b):
    return jnp.dot(a, b, preferred_element_type=F32)


def _dot_nt(a, b):
    return lax.dot_general(a, b, (((1,), (1,)), ((), ())), preferred_element_type=F32)


def _rms(x, g):
    return x * lax.rsqrt(jnp.mean(x * x, axis=-1, keepdims=True) + EPS) * g


def _for_row_chunks(n_rows, body, unroll=ROW_UNROLL):
    rc = min(ROW_CHUNK, n_rows)

    def step(r, carry):
        body(pl.ds(pl.multiple_of(r * rc, rc), rc))
        return carry

    lax.fori_loop(0, n_rows // rc, step, 0, unroll=min(unroll, n_rows // rc))


def _normmod_rows(x_ref, gss_ref, ops_ref, h_ref, copy_ref=None):
    ops_ref[0:1, :] = gss_ref[0:1, :] * (1.0 + gss_ref[2:3, :])

    def body(rows):
        x = x_ref[rows, :]
        y = x * lax.rsqrt(jnp.mean(x * x, axis=-1, keepdims=True) + EPS)
        h_ref[rows, :] = (y * ops_ref[0:1, :] + gss_ref[1:2, :]).astype(h_ref.dtype)
        if copy_ref is not None:
            copy_ref[rows, :] = x

    _for_row_chunks(x_ref.shape[0], body)


def _rms_rows(src_ref, g_ref, dst_ref):
    def body(rows):
        dst_ref[rows, :] = _rms(src_ref[rows, :], g_ref[...]).astype(dst_ref.dtype)

    _for_row_chunks(src_ref.shape[0], body)


def _rms_rows_from_sumsq(src_ref, ss_ref, g_ref, dst_ref):
    width = src_ref.shape[1]

    def body(rows):
        inv = lax.rsqrt(ss_ref[rows, :] * (1.0 / width) + EPS)
        dst_ref[rows, :] = (src_ref[rows, :] * inv * g_ref[...]).astype(dst_ref.dtype)

    _for_row_chunks(src_ref.shape[0], body, unroll=STREAM_UNROLL)


def _sumsq(x):
    return jnp.sum(x * x, axis=-1, keepdims=True)


def _ada_kernel(cv_ref, a_ref, b_ref, bias_ref, o_ref, h_ref):
    @pl.when(pl.program_id(1) == 0)
    def _():
        cv = cv_ref[...]
        s = cv * jax.nn.sigmoid(cv)
        h_ref[...] = _dot(s.astype(BF16), a_ref[0].astype(BF16))

    o_ref[0] = _dot(h_ref[...].astype(BF16), b_ref[0].astype(BF16)) + bias_ref[0]


def _ada(cv, ada_a, ada_b, ada_bias):
    depth, d, rank = ada_a.shape
    n = ada_b.shape[2]
    tn = d
    vmem = 2 * (d * rank * 4 + rank * tn * 4) + 4 * ADA_ROWS * (d + tn) * 4
    return pl.pallas_call(
        _ada_kernel,
        grid=(depth, n // tn),
        in_specs=[
            pl.BlockSpec((ADA_ROWS, d), lambda l, j: (0, 0)),
            pl.BlockSpec((1, d, rank), lambda l, j: (l, 0, 0)),
            pl.BlockSpec((1, rank, tn), lambda l, j: (l, 0, j)),
            pl.BlockSpec((1, 1, tn), lambda l, j: (l, 0, j)),
        ],
        out_specs=pl.BlockSpec((1, ADA_ROWS, tn), lambda l, j: (l, 0, j)),
        out_shape=jax.ShapeDtypeStruct((depth, ADA_ROWS, n), F32),
        scratch_shapes=[pltpu.VMEM((ADA_ROWS, rank), F32)],
        compiler_params=_params(("parallel", "arbitrary"), vmem),
        name="ada",
    )(cv, ada_a, ada_b, ada_bias.reshape(depth, 1, n))


def _rope_chunk(a, cos, sin, lane):
    up = pltpu.roll(a, HEAD_DIM - 32, axis=1)
    down = pltpu.roll(a, 32, axis=1)
    partner = jnp.where((lane % 64) < 32, up, down)
    return a * cos + partner * sin


def _normmod_kernel(x_ref, gss_ref, h_ref, ops_ref):
    _normmod_rows(x_ref, gss_ref, ops_ref, h_ref)


def _normmod(x, gss, tm):
    n_rows, d = x.shape
    return pl.pallas_call(
        _normmod_kernel,
        grid=(n_rows // tm,),
        in_specs=[pl.BlockSpec((tm, d), lambda i: (i, 0)), pl.BlockSpec((8, d), lambda i: (0, 0))],
        out_specs=pl.BlockSpec((tm, d), lambda i: (i, 0)),
        out_shape=jax.ShapeDtypeStruct((n_rows, d), BF16),
        scratch_shapes=[pltpu.VMEM((8, d), F32)],
        compiler_params=_params(("parallel",), 2 * tm * d * 6 + 16 * d * 4),
        cost_estimate=pl.CostEstimate(flops=8 * n_rows * d, transcendentals=n_rows, bytes_accessed=n_rows * d * 6),
        name="normmod",
    )(x, gss)


def _inproj_kernel(h_ref, w_ref, *rest, rope_cols, tn):
    if rope_cols:
        cos_ref, sin_ref, qkv_ref, f_ref = rest
    else:
        qkv_ref, f_ref = rest
    j = pl.program_id(1)
    heads_per_dot = DOT_COLS // HEAD_DIM

    def tile(n_rope, o_ref):
        if n_rope:
            lane = lax.broadcasted_iota(jnp.int32, (h_ref.shape[0], HEAD_DIM), 1)
        for c in range(tn // DOT_COLS):
            acc = _dot(h_ref[...], w_ref[:, c * DOT_COLS:(c + 1) * DOT_COLS])
            for k in range(heads_per_dot):
                head = c * heads_per_dot + k
                a = acc[:, k * HEAD_DIM:(k + 1) * HEAD_DIM]
                if head < n_rope:
                    a = _rope_chunk(a, cos_ref[...], sin_ref[...], lane)
                o_ref[:, head * HEAD_DIM:(head + 1) * HEAD_DIM] = a.astype(o_ref.dtype)

    qkv_tiles = (Q_WIDTH + 2 * KV_WIDTH) // tn
    full_tiles, part = divmod(rope_cols, tn)
    if full_tiles:
        pl.when(j < full_tiles)(functools.partial(tile, tn // HEAD_DIM, qkv_ref))
    if part:
        pl.when(j == full_tiles)(functools.partial(tile, part // HEAD_DIM, qkv_ref))
        full_tiles += 1
    if full_tiles < qkv_tiles:
        pl.when((j >= full_tiles) & (j < qkv_tiles))(functools.partial(tile, 0, qkv_ref))
    pl.when(j >= qkv_tiles)(functools.partial(tile, 0, f_ref))


def _inproj(h, w_all, layer, rope_tables, tm, tn):
    n_rows, d = h.shape
    n_out = w_all.shape[2]
    qkv_cols = Q_WIDTH + 2 * KV_WIDTH
    qkv_tiles = qkv_cols // tn
    rope_cols = Q_WIDTH + KV_WIDTH if rope_tables is not None else 0
    in_specs = [
        pl.BlockSpec((tm, d), lambda i, j: (i, 0)),
        pl.BlockSpec((None, d, tn), lambda i, j: (layer, 0, j)),
    ]
    args = [h, w_all]
    if rope_tables is not None:
        in_specs += [pl.BlockSpec((tm, HEAD_DIM), lambda i, j: (i, 0))] * 2
        args += list(rope_tables)
    vmem = 2 * (tm * d * 2 + d * tn * 2 + 2 * tm * tn * 2 + 2 * tm * HEAD_DIM * 4) + 4 * tm * DOT_COLS * 4
    cost = pl.CostEstimate(flops=2 * n_rows * d * n_out, transcendentals=0,
                           bytes_accessed=n_rows * d * 2 + (n_rows // tm) * d * n_out * 2 + n_rows * n_out * 2)
    return pl.pallas_call(
        functools.partial(_inproj_kernel, rope_cols=rope_cols, tn=tn),
        grid=(n_rows // tm, n_out // tn),
        in_specs=in_specs,
        out_specs=[
            pl.BlockSpec((tm, tn), lambda i, j: (i, jnp.minimum(j, qkv_tiles - 1))),
            pl.BlockSpec((tm, tn), lambda i, j: (i, jnp.maximum(j - qkv_tiles, 0))),
        ],
        out_shape=[
            jax.ShapeDtypeStruct((n_rows, qkv_cols), BF16),
            jax.ShapeDtypeStruct((n_rows, n_out - qkv_cols), BF16),
        ],
        compiler_params=_params(("parallel", "arbitrary"), vmem),
        cost_estimate=cost,
        name="inproj_rope" if rope_cols else "inproj_ctx",
    )(*args)


def _stack_heads(q_ref, rows, h):
    return jnp.concatenate(
        [q_ref[rows, (h * GROUP + g) * HEAD_DIM:(h * GROUP + g + 1) * HEAD_DIM] for g in range(GROUP)], axis=0)


def _softmax_chunk(s_ref, p_ref, rows, sink, masks):
    n_tiles = s_ref.shape[1] // BLOCK
    tiles = []
    for t in range(n_tiles):
        v = s_ref[rows, t * BLOCK:(t + 1) * BLOCK]
        if masks.get(t) is not None:
            v = jnp.where(masks[t], v, NEG_INF)
        tiles.append(v)
    m_raw = jnp.max(functools.reduce(jnp.maximum, tiles), axis=-1, keepdims=True)
    m = jnp.maximum(m_raw * (HEAD_DIM ** -0.5), sink)
    mb = m * LOG2E
    es = [jnp.exp2(v * (HEAD_DIM ** -0.5 * LOG2E) - mb) for v in tiles]
    denom = jnp.sum(functools.reduce(jnp.add, es), axis=-1, keepdims=True) + jnp.exp2(sink * LOG2E - mb)
    inv = 1.0 / denom
    for t in range(n_tiles):
        p_ref[rows, t * BLOCK:(t + 1) * BLOCK] = (es[t] * inv).astype(p_ref.dtype)


def _attn_units(sink_ref, q_ref, units, acc_ref, ss_ref, s_ref, p_ref):
    def scores(u):
        q_rows, h, keys, _, _ = units[u]
        s_ref[u % 2] = _dot_nt(_stack_heads(q_ref, q_rows, h), keys())

    scores(0)
    for u, (q_rows, h, _, values, mask_fn) in enumerate(units):
        if u + 1 < len(units):
            scores(u + 1)
        n_q = q_rows.stop - q_rows.start
        chunks_per_head = n_q // SOFTMAX_ROWS
        for r in range(GROUP * chunks_per_head):
            rows = slice(r * SOFTMAX_ROWS, (r + 1) * SOFTMAX_ROWS)
            sink = sink_ref[h * GROUP + r // chunks_per_head]
            q0 = (r % chunks_per_head) * SOFTMAX_ROWS
            _softmax_chunk(s_ref.at[u % 2], p_ref.at[u % 2], rows, sink, mask_fn(q0))
        o = _dot(p_ref[u % 2], values())
        ss = None
        for g in range(GROUP):
            c0 = (h * GROUP + g) * HEAD_DIM
            og = o[g * n_q:(g + 1) * n_q, :]
            acc_ref[q_rows, c0:c0 + HEAD_DIM] = og
            ss = _sumsq(og) if ss is None else ss + _sumsq(og)
        ss_ref[q_rows, :] = ss if h == 0 else ss_ref[q_rows, :] + ss


def _latent_attn_kernel(sink_ref, q_ref, kp_ref, kc_ref, kn_ref, vp_ref, vc_ref, vn_ref,
                        kx_ref, vx_ref, g_ref, o_ref, acc_ref, ss_ref, s_ref, p_ref, *, n_ctx, q_blocks):
    n = pl.program_id(0)
    nb = pl.num_programs(0)
    ctx_tiles = n_ctx // BLOCK
    prev_lo = jnp.where(n == 0, BLOCK, 0)
    next_hi = jnp.where(n == nb - 1, 0, BLOCK)

    def block_of(refs, i):
        prev_ref, cur_ref, next_ref = refs
        if i == 0:
            return prev_ref, slice(0, BLOCK)
        if i == q_blocks + 1:
            return next_ref, slice(0, BLOCK)
        return cur_ref, slice((i - 1) * BLOCK, i * BLOCK)

    def operand(ctx_ref, refs, qb, h):
        hs = slice(h * HEAD_DIM, (h + 1) * HEAD_DIM)
        parts = [ctx_ref[:, hs]]
        for i in range(qb, qb + 3):
            ref, rows = block_of(refs, i)
            parts.append(ref[rows, hs])
        return jnp.concatenate(parts, axis=0)

    def mask_fn(qb, q0):
        qi = lax.broadcasted_iota(jnp.int32, (SOFTMAX_ROWS, BLOCK), 0) + q0
        kj = lax.broadcasted_iota(jnp.int32, (SOFTMAX_ROWS, BLOCK), 1)
        lo = kj >= qi
        hi = kj <= qi
        if qb == 0:
            lo = lo & (kj >= prev_lo)
        if qb == q_blocks - 1:
            hi = hi & (kj < next_hi)
        return {ctx_tiles: lo, ctx_tiles + 2: hi}

    units = []
    for qb in range(q_blocks):
        for h in range(N_KV_HEADS):
            units.append((
                slice(qb * BLOCK, (qb + 1) * BLOCK), h,
                functools.partial(operand, kx_ref, (kp_ref, kc_ref, kn_ref), qb, h),
                functools.partial(operand, vx_ref, (vp_ref, vc_ref, vn_ref), qb, h),
                functools.partial(mask_fn, qb),
            ))
    _attn_units(sink_ref, q_ref, units, acc_ref, ss_ref, s_ref, p_ref)
    _rms_rows_from_sumsq(acc_ref, ss_ref, g_ref, o_ref)


def _latent_attn(px, pc, sink, g_attn):
    n_rows = px.shape[0]
    n_ctx = pc.shape[0]
    nb = n_rows // BLOCK
    qb = ATTN_Q_BLOCKS if nb % ATTN_Q_BLOCKS == 0 else 1
    kcol = Q_WIDTH // KV_WIDTH
    vcol = kcol + 1
    n_keys = n_ctx + 3 * BLOCK

    def kv_specs(colblk):
        return [
            pl.BlockSpec((BLOCK, KV_WIDTH), lambda n: (jnp.maximum(n * qb - 1, 0), colblk)),
            pl.BlockSpec((qb * BLOCK, KV_WIDTH), lambda n: (n, colblk)),
            pl.BlockSpec((BLOCK, KV_WIDTH), lambda n: (jnp.minimum(n * qb + qb, nb - 1), colblk)),
        ]

    in_specs = [
        pl.BlockSpec(memory_space=pltpu.SMEM),
        pl.BlockSpec((qb * BLOCK, Q_WIDTH), lambda n: (n, 0)),
        *kv_specs(kcol), *kv_specs(vcol),
        pl.BlockSpec((n_ctx, KV_WIDTH), lambda n: (0, kcol)),
        pl.BlockSpec((n_ctx, KV_WIDTH), lambda n: (0, vcol)),
        pl.BlockSpec((1, Q_WIDTH), lambda n: (0, 0)),
    ]
    return pl.pallas_call(
        functools.partial(_latent_attn_kernel, n_ctx=n_ctx, q_blocks=qb),
        grid=(nb // qb,),
        in_specs=in_specs,
        out_specs=pl.BlockSpec((qb * BLOCK, Q_WIDTH), lambda n: (n, 0)),
        out_shape=jax.ShapeDtypeStruct((n_rows, Q_WIDTH), BF16),
        scratch_shapes=[
            pltpu.VMEM((qb * BLOCK, Q_WIDTH), F32),
            pltpu.VMEM((qb * BLOCK, 1), F32),
            pltpu.VMEM((2, GROUP * BLOCK, n_keys), F32),
            pltpu.VMEM((2, GROUP * BLOCK, n_keys), BF16),
        ],
        compiler_params=_params(("parallel",), 24 << 20),
        cost_estimate=pl.CostEstimate(
            flops=4 * n_rows * N_HEADS * n_keys * HEAD_DIM, transcendentals=n_rows * N_HEADS * n_keys,
            bytes_accessed=n_rows * (2 * Q_WIDTH + 6 * KV_WIDTH) * 2),
        name="latent_attn",
    )(sink, px, px, px, px, px, px, px, pc, pc, g_attn.reshape(1, Q_WIDTH))


def _ctx_attn_kernel(sink_ref, q_ref, k_ref, v_ref, g_ref, o_ref, acc_ref, ss_ref, s_ref, p_ref):
    def operand(ref, h):
        return ref[:, h * HEAD_DIM:(h + 1) * HEAD_DIM]

    units = [(slice(0, q_ref.shape[0]), h, functools.partial(operand, k_ref, h),
              functools.partial(operand, v_ref, h), lambda q0: {}) for h in range(N_KV_HEADS)]
    _attn_units(sink_ref, q_ref, units, acc_ref, ss_ref, s_ref, p_ref)
    _rms_rows_from_sumsq(acc_ref, ss_ref, g_ref, o_ref)


def _ctx_attn(pc, sink, g_attn):
    n_ctx = pc.shape[0]
    kcol = Q_WIDTH // KV_WIDTH
    return pl.pallas_call(
        _ctx_attn_kernel,
        grid=(1,),
        in_specs=[
            pl.BlockSpec(memory_space=pltpu.SMEM),
            pl.BlockSpec((n_ctx, Q_WIDTH), lambda i: (0, 0)),
            pl.BlockSpec((n_ctx, KV_WIDTH), lambda i: (0, kcol)),
            pl.BlockSpec((n_ctx, KV_WIDTH), lambda i: (0, kcol + 1)),
            pl.BlockSpec((1, Q_WIDTH), lambda i: (0, 0)),
        ],
        out_specs=pl.BlockSpec((n_ctx, Q_WIDTH), lambda i: (0, 0)),
        out_shape=jax.ShapeDtypeStruct((n_ctx, Q_WIDTH), BF16),
        scratch_shapes=[
            pltpu.VMEM((n_ctx, Q_WIDTH), F32),
            pltpu.VMEM((n_ctx, 1), F32),
            pltpu.VMEM((2, GROUP * n_ctx, n_ctx), F32),
            pltpu.VMEM((2, GROUP * n_ctx, n_ctx), BF16),
        ],
        compiler_params=_params(("arbitrary",), 16 << 20),
        name="ctx_attn",
    )(sink, pc, pc, pc, g_attn.reshape(1, Q_WIDTH))


def _dft_tables(n):
    ang = 2.0 * np.pi * (np.outer(np.arange(n), np.arange(n)) % n) / n
    return np.cos(ang), np.sin(ang)


def _fourier_head_kernel(g_ref, z_ref, o_ref):
    for r in range(z_ref.shape[0]):
        o_ref[r] = _dot(g_ref[r], z_ref[r]).astype(o_ref.dtype)


def _fourier_head(zt, gmat):
    nb, na, c = zt.shape
    rb = min(HEAD_RESIDUES, nb)
    return pl.pallas_call(
        _fourier_head_kernel,
        grid=(nb // rb,),
        in_specs=[
            pl.BlockSpec((rb, 2 * na, na), lambda b: (b, 0, 0)),
            pl.BlockSpec((rb, na, c), lambda b: (b, 0, 0)),
        ],
        out_specs=pl.BlockSpec((rb, 2 * na, c), lambda b: (b, 0, 0)),
        out_shape=jax.ShapeDtypeStruct((nb, 2 * na, c), BF16),
        compiler_params=_params(("parallel",), 6 * rb * na * c * 2),
        cost_estimate=pl.CostEstimate(
            flops=4 * nb * na * na * c, transcendentals=0,
            bytes_accessed=(3 * nb * na * c + 2 * nb * na * na) * 2),
        name="fourier_head",
    )(gmat, zt)


def _fourier_tail_kernel(m_ref, d_ref, cs_ref, wf_ref, g_ref, *rest, pos_scale):
    o_ref, acc_ref, ss_ref = rest[-3:]
    kb, p = o_ref.shape[0], o_ref.shape[1]
    xs = [(_dot(m_ref[...], d_ref[blk]) * pos_scale).astype(BF16) for blk in range(kb)]
    lhs = jnp.concatenate(
        [jnp.concatenate([x[:p, g * F_DIM:(g + 1) * F_DIM], x[p:, g * F_DIM:(g + 1) * F_DIM]], axis=1)
         for g in range(N_FGROUPS) for x in xs], axis=0)
    f = (_dot(lhs, cs_ref[...]) * (F_DIM ** -0.5)).astype(BF16)
    for g in range(N_FGROUPS):
        og = _dot(f[g * kb * p:(g + 1) * kb * p, :], wf_ref[g])
        for blk in range(kb):
            piece = og[blk * p:(blk + 1) * p, :]
            acc_ref[blk, :, g * F_DIM:(g + 1) * F_DIM] = piece
            ss_ref[blk] = _sumsq(piece) if g == 0 else ss_ref[blk] + _sumsq(piece)
    for blk in range(kb):
        _rms_rows_from_sumsq(acc_ref.at[blk], ss_ref.at[blk], g_ref, o_ref.at[blk])


def _fourier_tail(stage_mat, data, cs, wf_all, layer, g_four, pos_scale, run_after=None):
    nblk, k_in, c = data.shape
    p = stage_mat.shape[0] // 2
    kb = min(TAIL_BLOCKS, nblk)
    temps = kb * p * c * (4 + 2 + 2 + 4 + 2)
    vmem = 2 * kb * (k_in + p) * c * 2 + kb * p * c * 4 + 2 * (2 * p * k_in + 6 * F_DIM * F_DIM) * 2 + temps
    in_specs = [
        pl.BlockSpec((2 * p, k_in), lambda i: (0, 0)),
        pl.BlockSpec((kb, k_in, c), lambda i: (i, 0, 0)),
        pl.BlockSpec((2 * F_DIM, F_DIM), lambda i: (0, 0)),
        pl.BlockSpec((None, N_FGROUPS, F_DIM, F_DIM), lambda i: (layer, 0, 0, 0)),
        pl.BlockSpec((1, c), lambda i: (0, 0)),
    ]
    args = [stage_mat, data, cs, wf_all, g_four.reshape(1, c)]
    if run_after is not None:
        in_specs.append(pl.BlockSpec((16, 128), lambda i: (0, 0)))
        args.append(run_after)
    return pl.pallas_call(
        functools.partial(_fourier_tail_kernel, pos_scale=pos_scale),
        grid=(nblk // kb,),
        in_specs=in_specs,
        out_specs=pl.BlockSpec((kb, p, c), lambda i: (i, 0, 0)),
        out_shape=jax.ShapeDtypeStruct((nblk, p, c), BF16),
        scratch_shapes=[pltpu.VMEM((kb, p, c), F32), pltpu.VMEM((kb, p, 1), F32)],
        compiler_params=_params(("parallel",), vmem),
        cost_estimate=pl.CostEstimate(
            flops=2 * nblk * (2 * p * k_in * c + p * c * 2 * F_DIM + p * c * F_DIM), transcendentals=nblk * p,
            bytes_accessed=nblk * (k_in + p) * c * 2 + 2 * p * k_in * 2 + 6 * F_DIM * F_DIM * 2),
        name="fourier_tail",
    )(*args)


class _FourierConsts:
    def __init__(self, n_lat, n_ctx):
        a = b = int(round(math.sqrt(n_lat)))
        assert a * b == n_lat
        self.a, self.b = a, b
        k_lo = np.arange(a)[None, :, None]
        n = (np.arange(a)[None, None, :] * b + np.arange(b)[:, None, None])
        ang = 2.0 * np.pi * ((k_lo * n) % n_lat) / n_lat
        self.head = jnp.asarray(np.concatenate([np.cos(ang), -np.sin(ang)], axis=1), BF16)
        cb, sb = _dft_tables(b)
        self.tail = jnp.asarray(np.block([[cb, sb], [-sb, cb]]), BF16)
        cc, sc = _dft_tables(n_ctx)
        self.ctx = jnp.asarray(np.concatenate([cc, -sc], axis=0), BF16)
        cf, sf = _dft_tables(F_DIM)
        self.chan = jnp.asarray(np.concatenate([cf, sf], axis=0), BF16)


def _latent_fourier(pf, fc, wf_all, layer, g_four, run_after):
    n_rows = pf.shape[0]
    a, b = fc.a, fc.b
    zt = jnp.transpose(pf.reshape(a, b, F_WIDTH), (1, 0, 2))
    t = _fourier_head(zt, fc.head)
    t = jnp.transpose(t.reshape(b, 2, a, F_WIDTH), (2, 1, 0, 3)).reshape(a, 2 * b, F_WIDTH)
    o = _fourier_tail(fc.tail, t, fc.chan, wf_all, layer, g_four, 1.0 / math.sqrt(n_rows),
                      run_after=run_after)
    return jnp.transpose(o, (1, 0, 2)).reshape(n_rows, F_WIDTH)


def _ctx_fourier(pf, fc, wf_all, layer, g_four):
    n_ctx = pf.shape[0]
    d = pf.reshape(1, n_ctx, F_WIDTH)
    o = _fourier_tail(fc.ctx, d, fc.chan, wf_all, layer, g_four, 1.0 / math.sqrt(n_ctx))
    return o.reshape(n_ctx, F_WIDTH)


def _outproj_kernel(na_ref, nf_ref, w_ref, x_ref, gate_ref, o_ref):
    ka = na_ref.shape[1]
    acc = _dot(na_ref[...], w_ref[:ka, :]) + _dot(nf_ref[...], w_ref[ka:, :])
    o_ref[...] = x_ref[...] + gate_ref[...] * acc


def _outproj(na, nf, w_all, layer, x, gate, tm, tn):
    n_rows, d = x.shape
    ka, kf = na.shape[1], nf.shape[1]
    vmem = 2 * (tm * (ka + kf) * 2 + (ka + kf) * tn * 2 + 2 * tm * tn * 4) + tm * tn * 4
    return pl.pallas_call(
        _outproj_kernel,
        grid=(n_rows // tm, d // tn),
        in_specs=[
            pl.BlockSpec((tm, ka), lambda i, j: (i, 0)),
            pl.BlockSpec((tm, kf), lambda i, j: (i, 0)),
            pl.BlockSpec((None, ka + kf, tn), lambda i, j: (layer, 0, j)),
            pl.BlockSpec((tm, tn), lambda i, j: (i, j)),
            pl.BlockSpec((1, tn), lambda i, j: (0, j)),
        ],
        out_specs=pl.BlockSpec((tm, tn), lambda i, j: (i, j)),
        out_shape=jax.ShapeDtypeStruct((n_rows, d), F32),
        compiler_params=_params(("parallel", "parallel"), vmem),
        cost_estimate=pl.CostEstimate(
            flops=2 * n_rows * (ka + kf) * d, transcendentals=0,
            bytes_accessed=n_rows * (ka + kf) * 2 + (n_rows // tm) * (ka + kf) * d * 2 + 2 * n_rows * d * 4),
        name="outproj",
    )(na, nf, w_all, x, gate.reshape(1, d))


def _mlp_kernel(x_ref, gss_ref, w1_ref, w2_ref, *rest, final, n_cast):
    raw_refs = rest[:n_cast]
    o_ref = rest[n_cast]
    cast_refs = rest[n_cast + 1:2 * n_cast + 1]
    h_ref, ops_ref = rest[2 * n_cast + 1:]
    f = pl.program_id(1)

    @pl.when(f == 0)
    def _():
        _normmod_rows(x_ref, gss_ref, ops_ref, h_ref, copy_ref=o_ref)

    for raw_ref, cast_ref in zip(raw_refs, cast_refs):
        cast_ref[...] = raw_ref[...].astype(cast_ref.dtype)

    u = jnp.maximum(_dot(h_ref[...], w1_ref[...]), 0.0)
    u = (u * u).astype(BF16)
    tn = MLP_OUT_CHUNK
    for c in range(o_ref.shape[1] // tn):
        cs = slice(c * tn, (c + 1) * tn)
        o_ref[:, cs] += gss_ref[3:4, cs] * _dot(u, w2_ref[:, cs])

    if final:
        @pl.when(f == pl.num_programs(1) - 1)
        def _():
            _rms_rows(o_ref, gss_ref.at[4:5, :], o_ref)


def _cast_slab(shape, n_steps):
    n_r, n_c = shape
    for q in (1, 2, 4, 8, 16):
        if n_steps % q or n_c % (q * 128) or n_r % (n_steps // q):
            continue
        br = n_r // (n_steps // q)
        if br % 16 == 0:
            return br, n_c // q, q
    raise ValueError(f"no bf16-tile-aligned slab split of {shape} into {n_steps} steps")


def _mlp(x, gss, w1_all, w2_all, layer, tm, tf, final, cast_next=()):
    n_rows, d = x.shape
    d_ff = w1_all.shape[2]
    n_f = d_ff // tf
    n_steps = (n_rows // tm) * n_f
    in_specs = [
        pl.BlockSpec((tm, d), lambda i, f: (i, 0)),
        pl.BlockSpec((8, d), lambda i, f: (0, 0)),
        pl.BlockSpec((None, d, tf), lambda i, f: (layer, 0, f)),
        pl.BlockSpec((None, tf, d), lambda i, f: (layer, f, 0)),
    ]
    out_specs = [pl.BlockSpec((tm, d), lambda i, f: (i, 0))]
    out_shape = [jax.ShapeDtypeStruct((n_rows, d), F32)]
    args = [x, gss, w1_all, w2_all]
    vmem = 2 * (2 * tm * d * 4 + 2 * d * tf * 2) + tm * d * 2 + tm * tf * 6 + 2 * tm * MLP_OUT_CHUNK * 4
    cast_bytes = 0
    for w_raw, src in cast_next:
        br, bc, q = _cast_slab(w_raw.shape[1:], n_steps)
        in_specs.append(pl.BlockSpec(
            (None, br, bc), lambda i, f, src=src, q=q: (src, (i * n_f + f) // q, (i * n_f + f) % q)))
        out_specs.append(pl.BlockSpec(
            (None, br, bc), lambda i, f, q=q: (0, (i * n_f + f) // q, (i * n_f + f) % q)))
        out_shape.append(jax.ShapeDtypeStruct((1,) + w_raw.shape[1:], BF16))
        args.append(w_raw)
        vmem += 2 * br * bc * 6
        cast_bytes += w_raw.shape[1] * w_raw.shape[2] * 6
    outs = pl.pallas_call(
        functools.partial(_mlp_kernel, final=final, n_cast=len(cast_next)),
        grid=(n_rows // tm, n_f),
        in_specs=in_specs,
        out_specs=out_specs,
        out_shape=out_shape,
        scratch_shapes=[pltpu.VMEM((tm, d), BF16), pltpu.VMEM((8, d), F32)],
        compiler_params=_params(("parallel", "arbitrary"), vmem),
        cost_estimate=pl.CostEstimate(
            flops=4 * n_rows * d * d_ff, transcendentals=n_rows,
            bytes_accessed=2 * n_rows * d * 4 + (n_rows // tm) * 2 * d * d_ff * 2 + cast_bytes),
        name="mlp_final" if final else "mlp",
    )(*args)
    return outs[0], tuple(outs[1:])


def _rope_tables(n_rows):
    quarter = HEAD_DIM // 4
    inv = ROPE_THETA ** (-jnp.arange(quarter, dtype=F32) / quarter)
    pos = jnp.arange(n_rows)
    rows = (pos // GRID_W).astype(F32)
    cols = (pos % GRID_W).astype(F32)
    ang = jnp.concatenate([rows[:, None] * inv[None, :]] * 2 + [cols[:, None] * inv[None, :]] * 2, axis=1)
    sign = jnp.tile(jnp.concatenate([-jnp.ones(quarter, F32), jnp.ones(quarter, F32)]), 2)
    return jnp.cos(ang), jnp.sin(ang) * sign[None, :]


def _rows8(*vecs):
    d = vecs[0].shape[0]
    pad = [jnp.zeros((d,), F32)] * (8 - len(vecs))
    return jnp.stack(list(vecs) + pad, axis=0)


def _row_tile(n_rows, want):
    return min(want, n_rows)


def kernel(x, c, ctx, c_ctx, ada_a, ada_b, ada_bias, g_mix, w_in, sink, w_f, g_attn_out, g_four_out,
           w_out, g_mlp, w1, w2, g_final):
    assert x.shape[0] == 1 and ctx.shape[0] == 1
    depth = w_in.shape[0]
    d = x.shape[2]
    xs = x[0]
    cs = ctx[0]
    n_lat, n_ctx = xs.shape[0], cs.shape[0]

    cv = jnp.zeros((ADA_ROWS, d), F32).at[0].set(c[0]).at[1].set(c_ctx)
    mods = _ada(cv, ada_a, ada_b, ada_bias)
    rope = _rope_tables(n_lat)
    fconst = _FourierConsts(n_lat, n_ctx)
    wf_b = w_f.astype(BF16)
    w_in_b, w_out_b, w1_b, w2_b = (w[0:1].astype(BF16) for w in (w_in, w_out, w1, w2))

    for l in range(depth):
        last = l == depth - 1
        m_lat = [mods[l, 0, i * d:(i + 1) * d] for i in range(N_MOD)]
        m_ctx = [mods[l, 1, i * d:(i + 1) * d] for i in range(N_MOD)]

        hc = _normmod(cs, _rows8(g_mix[l], m_ctx[0], m_ctx[1]), _row_tile(n_ctx, 256))
        hx = _normmod(xs, _rows8(g_mix[l], m_lat[0], m_lat[1]), _row_tile(n_lat, 512))
        pc, pcf = _inproj(hc, w_in_b, 0, None, _row_tile(n_ctx, 1024), 1024)
        px, pxf = _inproj(hx, w_in_b, 0, rope, _row_tile(n_lat, 1024), 1024)

        na = _latent_attn(px, pc, sink[l], g_attn_out[l])
        nf = _latent_fourier(pxf, fconst, wf_b, l, g_four_out[l], run_after=pc)
        x_mid = _outproj(na, nf, w_out_b, 0, xs, m_lat[2], _row_tile(n_lat, 1024), 1024)
        gss = _rows8(g_mlp[l], m_lat[3], m_lat[4], m_lat[5], g_final)
        cast_next = () if last else tuple((w, l + 1) for w in (w_in, w_out, w1, w2))
        xs, next_weights = _mlp(x_mid, gss, w1_b, w2_b, 0, _row_tile(n_lat, 512), 512, final=last,
                                cast_next=cast_next)

        if not last:
            nac = _ctx_attn(pc, sink[l], g_attn_out[l])
            nfc = _ctx_fourier(pcf, fconst, wf_b, l, g_four_out[l])
            c_mid = _outproj(nac, nfc, w_out_b, 0, cs, m_ctx[2], _row_tile(n_ctx, 1024), 1024)
            gss_c = _rows8(g_mlp[l], m_ctx[3], m_ctx[4], m_ctx[5])
            cs, _ = _mlp(c_mid, gss_c, w1_b, w2_b, 0, _row_tile(n_ctx, 512), 512, final=False)
            w_in_b, w_out_b, w1_b, w2_b = next_weights

    return xs[None]
```

```python
import functools
import math

import numpy as np
import jax
import jax.numpy as jnp
from jax import lax
from jax.experimental import pallas as pl
from jax.experimental.pallas import tpu as pltpu

F32 = jnp.float32
BF16 = jnp.bfloat16

N_HEADS = 16
N_KV_HEADS = 4
HEAD_DIM = 128
GROUP = N_HEADS // N_KV_HEADS
Q_WIDTH = N_HEADS * HEAD_DIM
KV_WIDTH = N_KV_HEADS * HEAD_DIM
BLOCK = 128
GRID_W = 64
N_FGROUPS = 4
F_DIM = 512
F_WIDTH = N_FGROUPS * F_DIM
N_MOD = 6
ROPE_THETA = 10000.0
EPS = 1e-6
NEG_INF = -1e30
LOG2E = 1.4426950408889634

V7X_VMEM_BYTES = 64 * 1024 * 1024
ADA_ROWS = 8
MLP_OUT_CHUNK = 512
MLP_SINGLE_BUFFER_STEPS = 8
ROW_CHUNK = 16
ROW_UNROLL = 2
STREAM_UNROLL = 8
DOT_COLS = 256
SOFTMAX_ROWS = 32
ATTN_Q_BLOCKS = 2
HEAD_RESIDUES = 8
TAIL_BLOCKS = 4


def _params(semantics, vmem_bytes):
    limit = min(int(vmem_bytes * 1.15) + (4 << 20), V7X_VMEM_BYTES - (2 << 20))
    return pltpu.CompilerParams(dimension_semantics=semantics, vmem_limit_bytes=limit)


def _dot(a, b):
    return jnp.dot(a, b, preferred_element_type=F32)


def _dot_nt(a, b):
    return lax.dot_general(a, b, (((1,), (1,)), ((), ())), preferred_element_type=F32)


def _rms(x, g):
    return x * lax.rsqrt(jnp.mean(x * x, axis=-1, keepdims=True) + EPS) * g


def _for_row_chunks(n_rows, body, unroll=ROW_UNROLL):
    rc = min(ROW_CHUNK, n_rows)

    def step(r, carry):
        body(pl.ds(pl.multiple_of(r * rc, rc), rc))
        return carry

    lax.fori_loop(0, n_rows // rc, step, 0, unroll=min(unroll, n_rows // rc))


def _normmod_rows(x_ref, gss_ref, ops_ref, h_ref, copy_ref=None):
    ops_ref[0:1, :] = gss_ref[0:1, :] * (1.0 + gss_ref[2:3, :])

    def body(rows):
        x = x_ref[rows, :]
        y = x * lax.rsqrt(jnp.mean(x * x, axis=-1, keepdims=True) + EPS)
        h_ref[rows, :] = (y * ops_ref[0:1, :] + gss_ref[1:2, :]).astype(h_ref.dtype)
        if copy_ref is not None:
            copy_ref[rows, :] = x

    _for_row_chunks(x_ref.shape[0], body)


def _rms_rows(src_ref, g_ref, dst_ref):
    def body(rows):
        dst_ref[rows, :] = _rms(src_ref[rows, :], g_ref[...]).astype(dst_ref.dtype)

    _for_row_chunks(src_ref.shape[0], body)


def _rms_rows_from_sumsq(src_ref, ss_ref, g_ref, dst_ref):
    width = src_ref.shape[1]

    def body(rows):
        inv = lax.rsqrt(ss_ref[rows, :] * (1.0 / width) + EPS)
        dst_ref[rows, :] = (src_ref[rows, :] * inv * g_ref[...]).astype(dst_ref.dtype)

    _for_row_chunks(src_ref.shape[0], body, unroll=STREAM_UNROLL)


def _sumsq(x):
    return jnp.sum(x * x, axis=-1, keepdims=True)


def _cast_slab(shape, n_steps):
    n_r, n_c = shape
    for q in (1, 2, 4, 8, 16):
        if n_steps % q or n_c % (q * 128) or n_r % (n_steps // q):
            continue
        br = n_r // (n_steps // q)
        if br % 16 == 0:
            return br, n_c // q, q
    raise ValueError(f"no bf16-tile-aligned slab split of {shape} into {n_steps} steps")


def _cast_specs(jobs, n_steps, step_of):
    ins, outs, shapes, args, vmem = [], [], [], [], 0
    for w_raw, src in jobs:
        br, bc, q = _cast_slab(w_raw.shape[1:], n_steps)
        ins.append(pl.BlockSpec(
            (None, br, bc), lambda *g, src=src, q=q: (src, step_of(*g) // q, step_of(*g) % q)))
        outs.append(pl.BlockSpec((None, br, bc), lambda *g, q=q: (0, step_of(*g) // q, step_of(*g) % q)))
        shapes.append(jax.ShapeDtypeStruct((1,) + w_raw.shape[1:], BF16))
        args.append(w_raw)
        vmem += 2 * br * bc * 6
    return ins, outs, shapes, args, vmem


def _cast_slabs(raw_refs, cast_refs):
    for raw_ref, cast_ref in zip(raw_refs, cast_refs):
        cast_ref[...] = raw_ref[...].astype(cast_ref.dtype)


def _ada_kernel(cv_ref, a_ref, b_ref, bias_ref, o_ref, h_ref):
    @pl.when(pl.program_id(1) == 0)
    def _():
        cv = cv_ref[...]
        s = cv * jax.nn.sigmoid(cv)
        h_ref[...] = _dot(s.astype(BF16), a_ref[0].astype(BF16))

    o_ref[0] = _dot(h_ref[...].astype(BF16), b_ref[0].astype(BF16)) + bias_ref[0]


def _ada(cv, ada_a, ada_b, ada_bias):
    depth, d, rank = ada_a.shape
    n = ada_b.shape[2]
    tn = d
    vmem = 2 * (d * rank * 4 + rank * tn * 4) + 4 * ADA_ROWS * (d + tn) * 4
    return pl.pallas_call(
        _ada_kernel,
        grid=(depth, n // tn),
        in_specs=[
            pl.BlockSpec((ADA_ROWS, d), lambda l, j: (0, 0)),
            pl.BlockSpec((1, d, rank), lambda l, j: (l, 0, 0)),
            pl.BlockSpec((1, rank, tn), lambda l, j: (l, 0, j)),
            pl.BlockSpec((1, 1, tn), lambda l, j: (l, 0, j)),
        ],
        out_specs=pl.BlockSpec((1, ADA_ROWS, tn), lambda l, j: (l, 0, j)),
        out_shape=jax.ShapeDtypeStruct((depth, ADA_ROWS, n), F32),
        scratch_shapes=[pltpu.VMEM((ADA_ROWS, rank), F32)],
        compiler_params=_params(("parallel", "arbitrary"), vmem),
        name="ada",
    )(cv, ada_a, ada_b, ada_bias.reshape(depth, 1, n))


def _rope_chunk(a, cos, sin, lane):
    up = pltpu.roll(a, HEAD_DIM - 32, axis=1)
    down = pltpu.roll(a, 32, axis=1)
    partner = jnp.where((lane % 64) < 32, up, down)
    return a * cos + partner * sin


def _normmod_kernel(x_ref, gss_ref, h_ref, ops_ref):
    _normmod_rows(x_ref, gss_ref, ops_ref, h_ref)


def _normmod(x, gss, tm):
    n_rows, d = x.shape
    return pl.pallas_call(
        _normmod_kernel,
        grid=(n_rows // tm,),
        in_specs=[pl.BlockSpec((tm, d), lambda i: (i, 0)), pl.BlockSpec((8, d), lambda i: (0, 0))],
        out_specs=pl.BlockSpec((tm, d), lambda i: (i, 0)),
        out_shape=jax.ShapeDtypeStruct((n_rows, d), BF16),
        scratch_shapes=[pltpu.VMEM((8, d), F32)],
        compiler_params=_params(("parallel",), 2 * tm * d * 6 + 16 * d * 4),
        cost_estimate=pl.CostEstimate(flops=8 * n_rows * d, transcendentals=n_rows, bytes_accessed=n_rows * d * 6),
        name="normmod",
    )(x, gss)


def _inproj_kernel(h_ref, w_ref, *rest, rope_cols, tn, n_cast):
    if rope_cols:
        cos_ref, sin_ref = rest[:2]
        rest = rest[2:]
    raw_refs = rest[:n_cast]
    qkv_ref, f_ref = rest[n_cast:n_cast + 2]
    cast_refs = rest[n_cast + 2:]
    j = pl.program_id(1)
    heads_per_dot = DOT_COLS // HEAD_DIM

    def tile(n_rope, o_ref):
        _cast_slabs(raw_refs, cast_refs)
        if n_rope:
            lane = lax.broadcasted_iota(jnp.int32, (h_ref.shape[0], HEAD_DIM), 1)
        for c in range(tn // DOT_COLS):
            acc = _dot(h_ref[...], w_ref[:, c * DOT_COLS:(c + 1) * DOT_COLS])
            for k in range(heads_per_dot):
                head = c * heads_per_dot + k
                a = acc[:, k * HEAD_DIM:(k + 1) * HEAD_DIM]
                if head < n_rope:
                    a = _rope_chunk(a, cos_ref[...], sin_ref[...], lane)
                o_ref[:, head * HEAD_DIM:(head + 1) * HEAD_DIM] = a.astype(o_ref.dtype)

    qkv_tiles = (Q_WIDTH + 2 * KV_WIDTH) // tn
    full_tiles, part = divmod(rope_cols, tn)
    if full_tiles:
        pl.when(j < full_tiles)(functools.partial(tile, tn // HEAD_DIM, qkv_ref))
    if part:
        pl.when(j == full_tiles)(functools.partial(tile, part // HEAD_DIM, qkv_ref))
        full_tiles += 1
    if full_tiles < qkv_tiles:
        pl.when((j >= full_tiles) & (j < qkv_tiles))(functools.partial(tile, 0, qkv_ref))
    pl.when(j >= qkv_tiles)(functools.partial(tile, 0, f_ref))


def _inproj(h, w_all, layer, rope_tables, tm, tn, cast=()):
    n_rows, d = h.shape
    n_out = w_all.shape[2]
    qkv_cols = Q_WIDTH + 2 * KV_WIDTH
    qkv_tiles = qkv_cols // tn
    n_j = n_out // tn
    rope_cols = Q_WIDTH + KV_WIDTH if rope_tables is not None else 0
    in_specs = [
        pl.BlockSpec((tm, d), lambda i, j: (i, 0)),
        pl.BlockSpec((None, d, tn), lambda i, j: (layer, 0, j)),
    ]
    args = [h, w_all]
    if rope_tables is not None:
        in_specs += [pl.BlockSpec((tm, HEAD_DIM), lambda i, j: (i, 0))] * 2
        args += list(rope_tables)
    slabs = 1 << (n_j.bit_length() - 1)
    c_in, c_out, c_shape, c_args, c_vmem = _cast_specs(
        cast, (n_rows // tm) * slabs, lambda i, j: i * slabs + jnp.minimum(j, slabs - 1))
    vmem = (2 * (tm * d * 2 + d * tn * 2 + 2 * tm * tn * 2 + 2 * tm * HEAD_DIM * 4) + 4 * tm * DOT_COLS * 4
            + c_vmem)
    cost = pl.CostEstimate(
        flops=2 * n_rows * d * n_out, transcendentals=0,
        bytes_accessed=(n_rows * d * 2 + (n_rows // tm) * d * n_out * 2 + n_rows * n_out * 2
                        + sum(6 * s.shape[1] * s.shape[2] for s in c_shape)))
    outs = pl.pallas_call(
        functools.partial(_inproj_kernel, rope_cols=rope_cols, tn=tn, n_cast=len(cast)),
        grid=(n_rows // tm, n_j),
        in_specs=in_specs + c_in,
        out_specs=[
            pl.BlockSpec((tm, tn), lambda i, j: (i, jnp.minimum(j, qkv_tiles - 1))),
            pl.BlockSpec((tm, tn), lambda i, j: (i, jnp.maximum(j - qkv_tiles, 0))),
            *c_out,
        ],
        out_shape=[
            jax.ShapeDtypeStruct((n_rows, qkv_cols), BF16),
            jax.ShapeDtypeStruct((n_rows, n_out - qkv_cols), BF16),
            *c_shape,
        ],
        compiler_params=_params(("parallel", "arbitrary"), vmem),
        cost_estimate=cost,
        name="inproj_rope" if rope_cols else "inproj_ctx",
    )(*args, *c_args)
    return outs[0], outs[1], tuple(outs[2:])


def _stack_heads(q_ref, rows, h):
    return jnp.concatenate(
        [q_ref[rows, (h * GROUP + g) * HEAD_DIM:(h * GROUP + g + 1) * HEAD_DIM] for g in range(GROUP)], axis=0)


def _softmax_chunk(s_ref, p_ref, rows, sink, masks):
    n_tiles = s_ref.shape[1] // BLOCK
    tiles = []
    for t in range(n_tiles):
        v = s_ref[rows, t * BLOCK:(t + 1) * BLOCK]
        if masks.get(t) is not None:
            v = jnp.where(masks[t], v, NEG_INF)
        tiles.append(v)
    m_raw = jnp.max(functools.reduce(jnp.maximum, tiles), axis=-1, keepdims=True)
    m = jnp.maximum(m_raw * (HEAD_DIM ** -0.5), sink)
    mb = m * LOG2E
    es = [jnp.exp2(v * (HEAD_DIM ** -0.5 * LOG2E) - mb) for v in tiles]
    denom = jnp.sum(functools.reduce(jnp.add, es), axis=-1, keepdims=True) + jnp.exp2(sink * LOG2E - mb)
    inv = 1.0 / denom
    for t in range(n_tiles):
        p_ref[rows, t * BLOCK:(t + 1) * BLOCK] = (es[t] * inv).astype(p_ref.dtype)


def _attn_units(sink_ref, q_ref, units, acc_ref, ss_ref, s_ref, p_ref):
    def scores(u):
        q_rows, h, keys, _, _ = units[u]
        s_ref[u % 2] = _dot_nt(_stack_heads(q_ref, q_rows, h), keys())

    scores(0)
    for u, (q_rows, h, _, values, mask_fn) in enumerate(units):
        if u + 1 < len(units):
            scores(u + 1)
        n_q = q_rows.stop - q_rows.start
        chunks_per_head = n_q // SOFTMAX_ROWS
        for r in range(GROUP * chunks_per_head):
            rows = slice(r * SOFTMAX_ROWS, (r + 1) * SOFTMAX_ROWS)
            sink = sink_ref[h * GROUP + r // chunks_per_head]
            q0 = (r % chunks_per_head) * SOFTMAX_ROWS
            _softmax_chunk(s_ref.at[u % 2], p_ref.at[u % 2], rows, sink, mask_fn(q0))
        o = _dot(p_ref[u % 2], values())
        ss = None
        for g in range(GROUP):
            c0 = (h * GROUP + g) * HEAD_DIM
            og = o[g * n_q:(g + 1) * n_q, :]
            acc_ref[q_rows, c0:c0 + HEAD_DIM] = og
            ss = _sumsq(og) if ss is None else ss + _sumsq(og)
        ss_ref[q_rows, :] = ss if h == 0 else ss_ref[q_rows, :] + ss


def _latent_attn_kernel(sink_ref, q_ref, kp_ref, kc_ref, kn_ref, vp_ref, vc_ref, vn_ref,
                        kx_ref, vx_ref, g_ref, *rest, n_ctx, q_blocks, n_cast):
    raw_refs = rest[:n_cast]
    o_ref = rest[n_cast]
    cast_refs = rest[n_cast + 1:2 * n_cast + 1]
    acc_ref, ss_ref, s_ref, p_ref = rest[2 * n_cast + 1:]
    _cast_slabs(raw_refs, cast_refs)
    n = pl.program_id(0)
    nb = pl.num_programs(0)
    ctx_tiles = n_ctx // BLOCK
    prev_lo = jnp.where(n == 0, BLOCK, 0)
    next_hi = jnp.where(n == nb - 1, 0, BLOCK)

    def block_of(refs, i):
        prev_ref, cur_ref, next_ref = refs
        if i == 0:
            return prev_ref, slice(0, BLOCK)
        if i == q_blocks + 1:
            return next_ref, slice(0, BLOCK)
        return cur_ref, slice((i - 1) * BLOCK, i * BLOCK)

    def operand(ctx_ref, refs, qb, h):
        hs = slice(h * HEAD_DIM, (h + 1) * HEAD_DIM)
        parts = [ctx_ref[:, hs]]
        for i in range(qb, qb + 3):
            ref, rows = block_of(refs, i)
            parts.append(ref[rows, hs])
        return jnp.concatenate(parts, axis=0)

    def mask_fn(qb, q0):
        qi = lax.broadcasted_iota(jnp.int32, (SOFTMAX_ROWS, BLOCK), 0) + q0
        kj = lax.broadcasted_iota(jnp.int32, (SOFTMAX_ROWS, BLOCK), 1)
        lo = kj >= qi
        hi = kj <= qi
        if qb == 0:
            lo = lo & (kj >= prev_lo)
        if qb == q_blocks - 1:
            hi = hi & (kj < next_hi)
        return {ctx_tiles: lo, ctx_tiles + 2: hi}

    units = []
    for qb in range(q_blocks):
        for h in range(N_KV_HEADS):
            units.append((
                slice(qb * BLOCK, (qb + 1) * BLOCK), h,
                functools.partial(operand, kx_ref, (kp_ref, kc_ref, kn_ref), qb, h),
                functools.partial(operand, vx_ref, (vp_ref, vc_ref, vn_ref), qb, h),
                functools.partial(mask_fn, qb),
            ))
    _attn_units(sink_ref, q_ref, units, acc_ref, ss_ref, s_ref, p_ref)
    _rms_rows_from_sumsq(acc_ref, ss_ref, g_ref, o_ref)


def _latent_attn(px, pc, sink, g_attn, cast=()):
    n_rows = px.shape[0]
    n_ctx = pc.shape[0]
    nb = n_rows // BLOCK
    qb = ATTN_Q_BLOCKS if nb % ATTN_Q_BLOCKS == 0 else 1
    kcol = Q_WIDTH // KV_WIDTH
    vcol = kcol + 1
    n_keys = n_ctx + 3 * BLOCK

    def kv_specs(colblk):
        return [
            pl.BlockSpec((BLOCK, KV_WIDTH), lambda n: (jnp.maximum(n * qb - 1, 0), colblk)),
            pl.BlockSpec((qb * BLOCK, KV_WIDTH), lambda n: (n, colblk)),
            pl.BlockSpec((BLOCK, KV_WIDTH), lambda n: (jnp.minimum(n * qb + qb, nb - 1), colblk)),
        ]

    in_specs = [
        pl.BlockSpec(memory_space=pltpu.SMEM),
        pl.BlockSpec((qb * BLOCK, Q_WIDTH), lambda n: (n, 0)),
        *kv_specs(kcol), *kv_specs(vcol),
        pl.BlockSpec((n_ctx, KV_WIDTH), lambda n: (0, kcol)),
        pl.BlockSpec((n_ctx, KV_WIDTH), lambda n: (0, vcol)),
        pl.BlockSpec((1, Q_WIDTH), lambda n: (0, 0)),
    ]
    c_in, c_out, c_shape, c_args, c_vmem = _cast_specs(cast, nb // qb, lambda n: n)
    outs = pl.pallas_call(
        functools.partial(_latent_attn_kernel, n_ctx=n_ctx, q_blocks=qb, n_cast=len(cast)),
        grid=(nb // qb,),
        in_specs=in_specs + c_in,
        out_specs=[pl.BlockSpec((qb * BLOCK, Q_WIDTH), lambda n: (n, 0)), *c_out],
        out_shape=[jax.ShapeDtypeStruct((n_rows, Q_WIDTH), BF16), *c_shape],
        scratch_shapes=[
            pltpu.VMEM((qb * BLOCK, Q_WIDTH), F32),
            pltpu.VMEM((qb * BLOCK, 1), F32),
            pltpu.VMEM((2, GROUP * BLOCK, n_keys), F32),
            pltpu.VMEM((2, GROUP * BLOCK, n_keys), BF16),
        ],
        compiler_params=_params(("parallel",), (24 << 20) + c_vmem),
        cost_estimate=pl.CostEstimate(
            flops=4 * n_rows * N_HEADS * n_keys * HEAD_DIM, transcendentals=n_rows * N_HEADS * n_keys,
            bytes_accessed=(n_rows * (2 * Q_WIDTH + 6 * KV_WIDTH) * 2
                            + sum(6 * s.shape[1] * s.shape[2] for s in c_shape))),
        name="latent_attn",
    )(sink, px, px, px, px, px, px, px, pc, pc, g_attn.reshape(1, Q_WIDTH), *c_args)
    return outs[0], tuple(outs[1:])


def _ctx_attn_kernel(sink_ref, q_ref, k_ref, v_ref, g_ref, o_ref, acc_ref, ss_ref, s_ref, p_ref):
    def operand(ref, h):
        return ref[:, h * HEAD_DIM:(h + 1) * HEAD_DIM]

    units = [(slice(0, q_ref.shape[0]), h, functools.partial(operand, k_ref, h),
              functools.partial(operand, v_ref, h), lambda q0: {}) for h in range(N_KV_HEADS)]
    _attn_units(sink_ref, q_ref, units, acc_ref, ss_ref, s_ref, p_ref)
    _rms_rows_from_sumsq(acc_ref, ss_ref, g_ref, o_ref)


def _ctx_attn(pc, sink, g_attn):
    n_ctx = pc.shape[0]
    kcol = Q_WIDTH // KV_WIDTH
    return pl.pallas_call(
        _ctx_attn_kernel,
        grid=(1,),
        in_specs=[
            pl.BlockSpec(memory_space=pltpu.SMEM),
            pl.BlockSpec((n_ctx, Q_WIDTH), lambda i: (0, 0)),
            pl.BlockSpec((n_ctx, KV_WIDTH), lambda i: (0, kcol)),
            pl.BlockSpec((n_ctx, KV_WIDTH), lambda i: (0, kcol + 1)),
            pl.BlockSpec((1, Q_WIDTH), lambda i: (0, 0)),
        ],
        out_specs=pl.BlockSpec((n_ctx, Q_WIDTH), lambda i: (0, 0)),
        out_shape=jax.ShapeDtypeStruct((n_ctx, Q_WIDTH), BF16),
        scratch_shapes=[
            pltpu.VMEM((n_ctx, Q_WIDTH), F32),
            pltpu.VMEM((n_ctx, 1), F32),
            pltpu.VMEM((2, GROUP * n_ctx, n_ctx), F32),
            pltpu.VMEM((2, GROUP * n_ctx, n_ctx), BF16),
        ],
        compiler_params=_params(("arbitrary",), 16 << 20),
        name="ctx_attn",
    )(sink, pc, pc, pc, g_attn.reshape(1, Q_WIDTH))


def _dft_tables(n):
    ang = 2.0 * np.pi * (np.outer(np.arange(n), np.arange(n)) % n) / n
    return np.cos(ang), np.sin(ang)


def _fourier_head_kernel(g_ref, z_ref, o_ref):
    for r in range(z_ref.shape[0]):
        o_ref[r] = _dot(g_ref[r], z_ref[r]).astype(o_ref.dtype)


def _fourier_head(zt, gmat):
    nb, na, c = zt.shape
    rb = min(HEAD_RESIDUES, nb)
    return pl.pallas_call(
        _fourier_head_kernel,
        grid=(nb // rb,),
        in_specs=[
            pl.BlockSpec((rb, 2 * na, na), lambda b: (b, 0, 0)),
            pl.BlockSpec((rb, na, c), lambda b: (b, 0, 0)),
        ],
        out_specs=pl.BlockSpec((rb, 2 * na, c), lambda b: (b, 0, 0)),
        out_shape=jax.ShapeDtypeStruct((nb, 2 * na, c), BF16),
        compiler_params=_params(("parallel",), 6 * rb * na * c * 2),
        cost_estimate=pl.CostEstimate(
            flops=4 * nb * na * na * c, transcendentals=0,
            bytes_accessed=(3 * nb * na * c + 2 * nb * na * na) * 2),
        name="fourier_head",
    )(gmat, zt)


def _fourier_tail_kernel(m_ref, d_ref, cs_ref, wf_ref, g_ref, *rest, pos_scale):
    o_ref, acc_ref, ss_ref = rest[-3:]
    kb, p = o_ref.shape[0], o_ref.shape[1]
    xs = [(_dot(m_ref[...], d_ref[blk]) * pos_scale).astype(BF16) for blk in range(kb)]
    lhs = jnp.concatenate(
        [jnp.concatenate([x[:p, g * F_DIM:(g + 1) * F_DIM], x[p:, g * F_DIM:(g + 1) * F_DIM]], axis=1)
         for g in range(N_FGROUPS) for x in xs], axis=0)
    f = (_dot(lhs, cs_ref[...]) * (F_DIM ** -0.5)).astype(BF16)
    for g in range(N_FGROUPS):
        og = _dot(f[g * kb * p:(g + 1) * kb * p, :], wf_ref[g])
        for blk in range(kb):
            piece = og[blk * p:(blk + 1) * p, :]
            acc_ref[blk, :, g * F_DIM:(g + 1) * F_DIM] = piece
            ss_ref[blk] = _sumsq(piece) if g == 0 else ss_ref[blk] + _sumsq(piece)
    for blk in range(kb):
        _rms_rows_from_sumsq(acc_ref.at[blk], ss_ref.at[blk], g_ref, o_ref.at[blk])


def _fourier_tail(stage_mat, data, cs, wf_all, layer, g_four, pos_scale, run_after=None):
    nblk, k_in, c = data.shape
    p = stage_mat.shape[0] // 2
    kb = min(TAIL_BLOCKS, nblk)
    temps = kb * p * c * (4 + 2 + 2 + 4 + 2)
    vmem = 2 * kb * (k_in + p) * c * 2 + kb * p * c * 4 + 2 * (2 * p * k_in + 6 * F_DIM * F_DIM) * 2 + temps
    in_specs = [
        pl.BlockSpec((2 * p, k_in), lambda i: (0, 0)),
        pl.BlockSpec((kb, k_in, c), lambda i: (i, 0, 0)),
        pl.BlockSpec((2 * F_DIM, F_DIM), lambda i: (0, 0)),
        pl.BlockSpec((None, N_FGROUPS, F_DIM, F_DIM), lambda i: (layer, 0, 0, 0)),
        pl.BlockSpec((1, c), lambda i: (0, 0)),
    ]
    args = [stage_mat, data, cs, wf_all, g_four.reshape(1, c)]
    if run_after is not None:
        in_specs.append(pl.BlockSpec((16, 128), lambda i: (0, 0)))
        args.append(run_after)
    return pl.pallas_call(
        functools.partial(_fourier_tail_kernel, pos_scale=pos_scale),
        grid=(nblk // kb,),
        in_specs=in_specs,
        out_specs=pl.BlockSpec((kb, p, c), lambda i: (i, 0, 0)),
        out_shape=jax.ShapeDtypeStruct((nblk, p, c), BF16),
        scratch_shapes=[pltpu.VMEM((kb, p, c), F32), pltpu.VMEM((kb, p, 1), F32)],
        compiler_params=_params(("parallel",), vmem),
        cost_estimate=pl.CostEstimate(
            flops=2 * nblk * (2 * p * k_in * c + p * c * 2 * F_DIM + p * c * F_DIM), transcendentals=nblk * p,
            bytes_accessed=nblk * (k_in + p) * c * 2 + 2 * p * k_in * 2 + 6 * F_DIM * F_DIM * 2),
        name="fourier_tail",
    )(*args)


class _FourierConsts:
    def __init__(self, n_lat, n_ctx):
        a = b = int(round(math.sqrt(n_lat)))
        assert a * b == n_lat
        self.a, self.b = a, b
        k_lo = np.arange(a)[None, :, None]
        n = (np.arange(a)[None, None, :] * b + np.arange(b)[:, None, None])
        ang = 2.0 * np.pi * ((k_lo * n) % n_lat) / n_lat
        self.head = jnp.asarray(np.concatenate([np.cos(ang), -np.sin(ang)], axis=1), BF16)
        cb, sb = _dft_tables(b)
        self.tail = jnp.asarray(np.block([[cb, sb], [-sb, cb]]), BF16)
        cc, sc = _dft_tables(n_ctx)
        self.ctx = jnp.asarray(np.concatenate([cc, -sc], axis=0), BF16)
        cf, sf = _dft_tables(F_DIM)
        self.chan = jnp.asarray(np.concatenate([cf, sf], axis=0), BF16)


def _latent_fourier(pf, fc, wf_all, layer, g_four, run_after):
    n_rows = pf.shape[0]
    a, b = fc.a, fc.b
    zt = jnp.transpose(pf.reshape(a, b, F_WIDTH), (1, 0, 2))
    t = _fourier_head(zt, fc.head)
    t = jnp.transpose(t.reshape(b, 2, a, F_WIDTH), (2, 1, 0, 3)).reshape(a, 2 * b, F_WIDTH)
    o = _fourier_tail(fc.tail, t, fc.chan, wf_all, layer, g_four, 1.0 / math.sqrt(n_rows),
                      run_after=run_after)
    return jnp.transpose(o, (1, 0, 2)).reshape(n_rows, F_WIDTH)


def _ctx_fourier(pf, fc, wf_all, layer, g_four):
    n_ctx = pf.shape[0]
    d = pf.reshape(1, n_ctx, F_WIDTH)
    o = _fourier_tail(fc.ctx, d, fc.chan, wf_all, layer, g_four, 1.0 / math.sqrt(n_ctx))
    return o.reshape(n_ctx, F_WIDTH)


def _outproj_kernel(na_ref, nf_ref, w_ref, x_ref, gate_ref, o_ref):
    ka = na_ref.shape[1]
    acc = _dot(na_ref[...], w_ref[:ka, :]) + _dot(nf_ref[...], w_ref[ka:, :])
    o_ref[...] = x_ref[...] + gate_ref[...] * acc


def _outproj(na, nf, w_all, layer, x, gate, tm, tn):
    n_rows, d = x.shape
    ka, kf = na.shape[1], nf.shape[1]
    vmem = 2 * (tm * (ka + kf) * 2 + (ka + kf) * tn * 2 + 2 * tm * tn * 4) + tm * tn * 4
    return pl.pallas_call(
        _outproj_kernel,
        grid=(n_rows // tm, d // tn),
        in_specs=[
            pl.BlockSpec((tm, ka), lambda i, j: (i, 0)),
            pl.BlockSpec((tm, kf), lambda i, j: (i, 0)),
            pl.BlockSpec((None, ka + kf, tn), lambda i, j: (layer, 0, j)),
            pl.BlockSpec((tm, tn), lambda i, j: (i, j)),
            pl.BlockSpec((1, tn), lambda i, j: (0, j)),
        ],
        out_specs=pl.BlockSpec((tm, tn), lambda i, j: (i, j)),
        out_shape=jax.ShapeDtypeStruct((n_rows, d), F32),
        compiler_params=_params(("parallel", "parallel"), vmem),
        cost_estimate=pl.CostEstimate(
            flops=2 * n_rows * (ka + kf) * d, transcendentals=0,
            bytes_accessed=n_rows * (ka + kf) * 2 + (n_rows // tm) * (ka + kf) * d * 2 + 2 * n_rows * d * 4),
        name="outproj",
    )(na, nf, w_all, x, gate.reshape(1, d))


def _mlp_kernel(x_ref, gss_ref, w1_ref, w2_ref, *rest, final, n_cast):
    raw_refs = rest[:n_cast]
    o_ref = rest[n_cast]
    cast_refs = rest[n_cast + 1:2 * n_cast + 1]
    h_ref, ops_ref = rest[2 * n_cast + 1:]
    f = pl.program_id(1)

    @pl.when(f == 0)
    def _():
        _normmod_rows(x_ref, gss_ref, ops_ref, h_ref, copy_ref=o_ref)

    _cast_slabs(raw_refs, cast_refs)
    u = jnp.maximum(_dot(h_ref[...], w1_ref[...]), 0.0)
    u = (u * u).astype(BF16)
    tn = MLP_OUT_CHUNK
    for c in range(o_ref.shape[1] // tn):
        cs = slice(c * tn, (c + 1) * tn)
        o_ref[:, cs] += gss_ref[3:4, cs] * _dot(u, w2_ref[:, cs])

    if final:
        @pl.when(f == pl.num_programs(1) - 1)
        def _():
            _rms_rows(o_ref, gss_ref.at[4:5, :], o_ref)


def _mlp(x, gss, w1_all, w2_all, layer, tm, tf, final, cast_next=()):
    n_rows, d = x.shape
    d_ff = w1_all.shape[2]
    n_f = d_ff // tf
    c_in, c_out, c_shape, c_args, c_vmem = _cast_specs(
        cast_next, (n_rows // tm) * n_f, lambda i, f: i * n_f + f)
    one_row_buffer = n_f >= MLP_SINGLE_BUFFER_STEPS and n_rows > tm
    row_mode = pl.Buffered(1) if one_row_buffer else None
    vmem = ((1 if one_row_buffer else 2) * 2 * tm * d * 4 + 2 * 2 * d * tf * 2 + tm * d * 2 + tm * tf * 6
            + 2 * tm * MLP_OUT_CHUNK * 4 + c_vmem)
    outs = pl.pallas_call(
        functools.partial(_mlp_kernel, final=final, n_cast=len(cast_next)),
        grid=(n_rows // tm, n_f),
        in_specs=[
            pl.BlockSpec((tm, d), lambda i, f: (i, 0), pipeline_mode=row_mode),
            pl.BlockSpec((8, d), lambda i, f: (0, 0)),
            pl.BlockSpec((None, d, tf), lambda i, f: (layer, 0, f)),
            pl.BlockSpec((None, tf, d), lambda i, f: (layer, f, 0)),
            *c_in,
        ],
        out_specs=[pl.BlockSpec((tm, d), lambda i, f: (i, 0), pipeline_mode=row_mode), *c_out],
        out_shape=[jax.ShapeDtypeStruct((n_rows, d), F32), *c_shape],
        scratch_shapes=[pltpu.VMEM((tm, d), BF16), pltpu.VMEM((8, d), F32)],
        compiler_params=_params(("parallel", "arbitrary"), vmem),
        cost_estimate=pl.CostEstimate(
            flops=4 * n_rows * d * d_ff, transcendentals=n_rows,
            bytes_accessed=(2 * n_rows * d * 4 + (n_rows // tm) * 2 * d * d_ff * 2
                            + sum(6 * s.shape[1] * s.shape[2] for s in c_shape))),
        name="mlp_final" if final else "mlp",
    )(x, gss, w1_all, w2_all, *c_args)
    return outs[0], tuple(outs[1:])


def _rope_tables(n_rows):
    quarter = HEAD_DIM // 4
    inv = ROPE_THETA ** (-jnp.arange(quarter, dtype=F32) / quarter)
    pos = jnp.arange(n_rows)
    rows = (pos // GRID_W).astype(F32)
    cols = (pos % GRID_W).astype(F32)
    ang = jnp.concatenate([rows[:, None] * inv[None, :]] * 2 + [cols[:, None] * inv[None, :]] * 2, axis=1)
    sign = jnp.tile(jnp.concatenate([-jnp.ones(quarter, F32), jnp.ones(quarter, F32)]), 2)
    return jnp.cos(ang), jnp.sin(ang) * sign[None, :]


def _rows8(*vecs):
    d = vecs[0].shape[0]
    pad = [jnp.zeros((d,), F32)] * (8 - len(vecs))
    return jnp.stack(list(vecs) + pad, axis=0)


def _row_tile(n_rows, want):
    return min(want, n_rows)


def kernel(x, c, ctx, c_ctx, ada_a, ada_b, ada_bias, g_mix, w_in, sink, w_f, g_attn_out, g_four_out,
           w_out, g_mlp, w1, w2, g_final):
    assert x.shape[0] == 1 and ctx.shape[0] == 1
    depth = w_in.shape[0]
    d = x.shape[2]
    xs = x[0]
    cs = ctx[0]
    n_lat, n_ctx = xs.shape[0], cs.shape[0]

    cv = jnp.zeros((ADA_ROWS, d), F32).at[0].set(c[0]).at[1].set(c_ctx)
    mods = _ada(cv, ada_a, ada_b, ada_bias)
    rope = _rope_tables(n_lat)
    fconst = _FourierConsts(n_lat, n_ctx)
    wf_b = w_f.astype(BF16)
    w_in_b, w_out_b = (w[0:1].astype(BF16) for w in (w_in, w_out))
    w1_b = w2_b = None

    for l in range(depth):
        last = l == depth - 1
        m_lat = [mods[l, 0, i * d:(i + 1) * d] for i in range(N_MOD)]
        m_ctx = [mods[l, 1, i * d:(i + 1) * d] for i in range(N_MOD)]

        hc = _normmod(cs, _rows8(g_mix[l], m_ctx[0], m_ctx[1]), _row_tile(n_ctx, 256))
        hx = _normmod(xs, _rows8(g_mix[l], m_lat[0], m_lat[1]), _row_tile(n_lat, 512))
        pc, pcf, _ = _inproj(hc, w_in_b, 0, None, _row_tile(n_ctx, 1024), 1024)
        px, pxf, cast1 = _inproj(hx, w_in_b, 0, rope, _row_tile(n_lat, 1024), 1024,
                                 cast=((w1, 0),) if l == 0 else ())

        na, cast2 = _latent_attn(px, pc, sink[l], g_attn_out[l], cast=((w2, 0),) if l == 0 else ())
        nf = _latent_fourier(pxf, fconst, wf_b, l, g_four_out[l], run_after=pc)
        x_mid = _outproj(na, nf, w_out_b, 0, xs, m_lat[2], _row_tile(n_lat, 1024), 1024)
        if l == 0:
            (w1_b,), (w2_b,) = cast1, cast2
        gss = _rows8(g_mlp[l], m_lat[3], m_lat[4], m_lat[5], g_final)
        cast_next = () if last else tuple((w, l + 1) for w in (w_in, w_out, w1, w2))
        xs, next_weights = _mlp(x_mid, gss, w1_b, w2_b, 0, _row_tile(n_lat, 512), 1024, final=last,
                                cast_next=cast_next)

        if not last:
            nac = _ctx_attn(pc, sink[l], g_attn_out[l])
            nfc = _ctx_fourier(pcf, fconst, wf_b, l, g_four_out[l])
            c_mid = _outproj(nac, nfc, w_out_b, 0, cs, m_ctx[2], _row_tile(n_ctx, 1024), 1024)
            gss_c = _rows8(g_mlp[l], m_ctx[3], m_ctx[4], m_ctx[5])
            cs, _ = _mlp(c_mid, gss_c, w1_b, w2_b, 0, _row_tile(n_ctx, 512), 512, final=False)
            w_in_b, w_out_b, w1_b, w2_b = next_weights

    return xs[None]
```

```python
import functools
import math

import numpy as np
import jax
import jax.numpy as jnp
from jax import lax
from jax.experimental import pallas as pl
from jax.experimental.pallas import tpu as pltpu

F32 = jnp.float32
BF16 = jnp.bfloat16

N_HEADS = 16
N_KV_HEADS = 4
HEAD_DIM = 128
GROUP = N_HEADS // N_KV_HEADS
Q_WIDTH = N_HEADS * HEAD_DIM
KV_WIDTH = N_KV_HEADS * HEAD_DIM
BLOCK = 128
GRID_W = 64
N_FGROUPS = 4
F_DIM = 512
F_WIDTH = N_FGROUPS * F_DIM
N_MOD = 6
ROPE_THETA = 10000.0
EPS = 1e-6
NEG_INF = -1e30
LOG2E = 1.4426950408889634

V7X_VMEM_BYTES = 64 * 1024 * 1024
ADA_ROWS = 8
MLP_OUT_CHUNK = 512
ROW_CHUNK = 16
ROW_UNROLL = 2
STREAM_UNROLL = 8
DOT_COLS = 256
SOFTMAX_ROWS = 32
ATTN_Q_BLOCKS = 2
HEAD_RESIDUES = 8
TAIL_BLOCKS = 8


def _params(semantics, vmem_bytes):
    limit = min(int(vmem_bytes * 1.15) + (4 << 20), V7X_VMEM_BYTES - (2 << 20))
    return pltpu.CompilerParams(dimension_semantics=semantics, vmem_limit_bytes=limit)


def _dot(a, b):
    return jnp.dot(a, b, preferred_element_type=F32)


def _dot_nt(a, b):
    return lax.dot_general(a, b, (((1,), (1,)), ((), ())), preferred_element_type=F32)


def _rms(x, g):
    return x * lax.rsqrt(jnp.mean(x * x, axis=-1, keepdims=True) + EPS) * g


def _for_row_chunks(n_rows, body, unroll=ROW_UNROLL):
    rc = min(ROW_CHUNK, n_rows)

    def step(r, carry):
        body(pl.ds(pl.multiple_of(r * rc, rc), rc))
        return carry

    lax.fori_loop(0, n_rows // rc, step, 0, unroll=min(unroll, n_rows // rc))


def _normmod_rows(x_ref, gss_ref, ops_ref, h_ref, copy_ref=None):
    ops_ref[0:1, :] = gss_ref[0:1, :] * (1.0 + gss_ref[2:3, :])

    def body(rows):
        x = x_ref[rows, :]
        y = x * lax.rsqrt(jnp.mean(x * x, axis=-1, keepdims=True) + EPS)
        h_ref[rows, :] = (y * ops_ref[0:1, :] + gss_ref[1:2, :]).astype(h_ref.dtype)
        if copy_ref is not None:
            copy_ref[rows, :] = x

    _for_row_chunks(x_ref.shape[0], body)


def _rms_rows(src_ref, g_ref, dst_ref):
    def body(rows):
        dst_ref[rows, :] = _rms(src_ref[rows, :], g_ref[...]).astype(dst_ref.dtype)

    _for_row_chunks(src_ref.shape[0], body)


def _rms_rows_from_sumsq(src_ref, ss_ref, g_ref, dst_ref):
    width = src_ref.shape[1]

    def body(rows):
        inv = lax.rsqrt(ss_ref[rows, :] * (1.0 / width) + EPS)
        dst_ref[rows, :] = (src_ref[rows, :] * inv * g_ref[...]).astype(dst_ref.dtype)

    _for_row_chunks(src_ref.shape[0], body, unroll=STREAM_UNROLL)


def _sumsq(x):
    return jnp.sum(x * x, axis=-1, keepdims=True)


def _cast_slab(shape, n_steps):
    n_r, n_c = shape
    for q in (1, 2, 4, 8, 16):
        if n_steps % q or n_c % (q * 128) or n_r % (n_steps // q):
            continue
        br = n_r // (n_steps // q)
        if br % 16 == 0:
            return br, n_c // q, q
    raise ValueError(f"no bf16-tile-aligned slab split of {shape} into {n_steps} steps")


def _cast_specs(jobs, n_steps, step_of):
    ins, outs, shapes, args, vmem = [], [], [], [], 0
    for w_raw, src in jobs:
        br, bc, q = _cast_slab(w_raw.shape[1:], n_steps)
        ins.append(pl.BlockSpec(
            (None, br, bc), lambda *g, src=src, q=q: (src, step_of(*g) // q, step_of(*g) % q)))
        outs.append(pl.BlockSpec((None, br, bc), lambda *g, q=q: (0, step_of(*g) // q, step_of(*g) % q)))
        shapes.append(jax.ShapeDtypeStruct((1,) + w_raw.shape[1:], BF16))
        args.append(w_raw)
        vmem += 2 * br * bc * 6
    return ins, outs, shapes, args, vmem


def _cast_slabs(raw_refs, cast_refs):
    for raw_ref, cast_ref in zip(raw_refs, cast_refs):
        cast_ref[...] = raw_ref[...].astype(cast_ref.dtype)


def _ada_kernel(cv_ref, a_ref, b_ref, bias_ref, o_ref, h_ref):
    @pl.when(pl.program_id(1) == 0)
    def _():
        cv = cv_ref[...]
        s = cv * jax.nn.sigmoid(cv)
        h_ref[...] = _dot(s.astype(BF16), a_ref[0].astype(BF16))

    o_ref[0] = _dot(h_ref[...].astype(BF16), b_ref[0].astype(BF16)) + bias_ref[0]


def _ada(cv, ada_a, ada_b, ada_bias):
    depth, d, rank = ada_a.shape
    n = ada_b.shape[2]
    tn = d
    vmem = 2 * (d * rank * 4 + rank * tn * 4) + 4 * ADA_ROWS * (d + tn) * 4
    return pl.pallas_call(
        _ada_kernel,
        grid=(depth, n // tn),
        in_specs=[
            pl.BlockSpec((ADA_ROWS, d), lambda l, j: (0, 0)),
            pl.BlockSpec((1, d, rank), lambda l, j: (l, 0, 0)),
            pl.BlockSpec((1, rank, tn), lambda l, j: (l, 0, j)),
            pl.BlockSpec((1, 1, tn), lambda l, j: (l, 0, j)),
        ],
        out_specs=pl.BlockSpec((1, ADA_ROWS, tn), lambda l, j: (l, 0, j)),
        out_shape=jax.ShapeDtypeStruct((depth, ADA_ROWS, n), F32),
        scratch_shapes=[pltpu.VMEM((ADA_ROWS, rank), F32)],
        compiler_params=_params(("parallel", "arbitrary"), vmem),
        name="ada",
    )(cv, ada_a, ada_b, ada_bias.reshape(depth, 1, n))


def _rope_chunk(a, cos, sin, lane):
    up = pltpu.roll(a, HEAD_DIM - 32, axis=1)
    down = pltpu.roll(a, 32, axis=1)
    partner = jnp.where((lane % 64) < 32, up, down)
    return a * cos + partner * sin


def _normmod_kernel(x_ref, gss_ref, h_ref, ops_ref):
    _normmod_rows(x_ref, gss_ref, ops_ref, h_ref)


def _normmod(x, gss, tm):
    n_rows, d = x.shape
    return pl.pallas_call(
        _normmod_kernel,
        grid=(n_rows // tm,),
        in_specs=[pl.BlockSpec((tm, d), lambda i: (i, 0)), pl.BlockSpec((8, d), lambda i: (0, 0))],
        out_specs=pl.BlockSpec((tm, d), lambda i: (i, 0)),
        out_shape=jax.ShapeDtypeStruct((n_rows, d), BF16),
        scratch_shapes=[pltpu.VMEM((8, d), F32)],
        compiler_params=_params(("parallel",), 2 * tm * d * 6 + 16 * d * 4),
        cost_estimate=pl.CostEstimate(flops=8 * n_rows * d, transcendentals=n_rows, bytes_accessed=n_rows * d * 6),
        name="normmod",
    )(x, gss)


def _inproj_kernel(h_ref, w_ref, *rest, rope_cols, tn, n_cast):
    if rope_cols:
        cos_ref, sin_ref = rest[:2]
        rest = rest[2:]
    raw_refs = rest[:n_cast]
    qkv_ref, f_ref = rest[n_cast:n_cast + 2]
    cast_refs = rest[n_cast + 2:]
    j = pl.program_id(1)
    heads_per_dot = DOT_COLS // HEAD_DIM

    def tile(n_rope, o_ref):
        _cast_slabs(raw_refs, cast_refs)
        if n_rope:
            lane = lax.broadcasted_iota(jnp.int32, (h_ref.shape[0], HEAD_DIM), 1)
        for c in range(tn // DOT_COLS):
            acc = _dot(h_ref[...], w_ref[:, c * DOT_COLS:(c + 1) * DOT_COLS])
            for k in range(heads_per_dot):
                head = c * heads_per_dot + k
                a = acc[:, k * HEAD_DIM:(k + 1) * HEAD_DIM]
                if head < n_rope:
                    a = _rope_chunk(a, cos_ref[...], sin_ref[...], lane)
                o_ref[:, head * HEAD_DIM:(head + 1) * HEAD_DIM] = a.astype(o_ref.dtype)

    qkv_tiles = (Q_WIDTH + 2 * KV_WIDTH) // tn
    full_tiles, part = divmod(rope_cols, tn)
    if full_tiles:
        pl.when(j < full_tiles)(functools.partial(tile, tn // HEAD_DIM, qkv_ref))
    if part:
        pl.when(j == full_tiles)(functools.partial(tile, part // HEAD_DIM, qkv_ref))
        full_tiles += 1
    if full_tiles < qkv_tiles:
        pl.when((j >= full_tiles) & (j < qkv_tiles))(functools.partial(tile, 0, qkv_ref))
    pl.when(j >= qkv_tiles)(functools.partial(tile, 0, f_ref))


def _inproj(h, w_all, layer, rope_tables, tm, tn, cast=()):
    n_rows, d = h.shape
    n_out = w_all.shape[2]
    qkv_cols = Q_WIDTH + 2 * KV_WIDTH
    qkv_tiles = qkv_cols // tn
    n_j = n_out // tn
    rope_cols = Q_WIDTH + KV_WIDTH if rope_tables is not None else 0
    in_specs = [
        pl.BlockSpec((tm, d), lambda i, j: (i, 0)),
        pl.BlockSpec((None, d, tn), lambda i, j: (layer, 0, j)),
    ]
    args = [h, w_all]
    if rope_tables is not None:
        in_specs += [pl.BlockSpec((tm, HEAD_DIM), lambda i, j: (i, 0))] * 2
        args += list(rope_tables)
    slabs = 1 << (n_j.bit_length() - 1)
    c_in, c_out, c_shape, c_args, c_vmem = _cast_specs(
        cast, (n_rows // tm) * slabs, lambda i, j: i * slabs + jnp.minimum(j, slabs - 1))
    vmem = (2 * (tm * d * 2 + d * tn * 2 + 2 * tm * tn * 2 + 2 * tm * HEAD_DIM * 4) + 4 * tm * DOT_COLS * 4
            + c_vmem)
    cost = pl.CostEstimate(
        flops=2 * n_rows * d * n_out, transcendentals=0,
        bytes_accessed=(n_rows * d * 2 + (n_rows // tm) * d * n_out * 2 + n_rows * n_out * 2
                        + sum(6 * s.shape[1] * s.shape[2] for s in c_shape)))
    outs = pl.pallas_call(
        functools.partial(_inproj_kernel, rope_cols=rope_cols, tn=tn, n_cast=len(cast)),
        grid=(n_rows // tm, n_j),
        in_specs=in_specs + c_in,
        out_specs=[
            pl.BlockSpec((tm, tn), lambda i, j: (i, jnp.minimum(j, qkv_tiles - 1))),
            pl.BlockSpec((tm, tn), lambda i, j: (i, jnp.maximum(j - qkv_tiles, 0))),
            *c_out,
        ],
        out_shape=[
            jax.ShapeDtypeStruct((n_rows, qkv_cols), BF16),
            jax.ShapeDtypeStruct((n_rows, n_out - qkv_cols), BF16),
            *c_shape,
        ],
        compiler_params=_params(("parallel", "arbitrary"), vmem),
        cost_estimate=cost,
        name="inproj_rope" if rope_cols else "inproj_ctx",
    )(*args, *c_args)
    return outs[0], outs[1], tuple(outs[2:])


def _stack_heads(q_ref, rows, h):
    return jnp.concatenate(
        [q_ref[rows, (h * GROUP + g) * HEAD_DIM:(h * GROUP + g + 1) * HEAD_DIM] for g in range(GROUP)], axis=0)


def _softmax_chunk(s_ref, p_ref, rows, sink, masks):
    n_tiles = s_ref.shape[1] // BLOCK
    tiles = []
    for t in range(n_tiles):
        v = s_ref[rows, t * BLOCK:(t + 1) * BLOCK]
        if masks.get(t) is not None:
            v = jnp.where(masks[t], v, NEG_INF)
        tiles.append(v)
    m_raw = jnp.max(functools.reduce(jnp.maximum, tiles), axis=-1, keepdims=True)
    m = jnp.maximum(m_raw * (HEAD_DIM ** -0.5), sink)
    mb = m * LOG2E
    es = [jnp.exp2(v * (HEAD_DIM ** -0.5 * LOG2E) - mb) for v in tiles]
    denom = jnp.sum(functools.reduce(jnp.add, es), axis=-1, keepdims=True) + jnp.exp2(sink * LOG2E - mb)
    inv = 1.0 / denom
    for t in range(n_tiles):
        p_ref[rows, t * BLOCK:(t + 1) * BLOCK] = (es[t] * inv).astype(p_ref.dtype)


def _attn_units(sink_ref, q_ref, units, acc_ref, ss_ref, s_ref, p_ref):
    def scores(u):
        q_rows, h, keys, _, _ = units[u]
        s_ref[u % 2] = _dot_nt(_stack_heads(q_ref, q_rows, h), keys())

    scores(0)
    for u, (q_rows, h, _, values, mask_fn) in enumerate(units):
        if u + 1 < len(units):
            scores(u + 1)
        n_q = q_rows.stop - q_rows.start
        chunks_per_head = n_q // SOFTMAX_ROWS
        for r in range(GROUP * chunks_per_head):
            rows = slice(r * SOFTMAX_ROWS, (r + 1) * SOFTMAX_ROWS)
            sink = sink_ref[h * GROUP + r // chunks_per_head]
            q0 = (r % chunks_per_head) * SOFTMAX_ROWS
            _softmax_chunk(s_ref.at[u % 2], p_ref.at[u % 2], rows, sink, mask_fn(q0))
        o = _dot(p_ref[u % 2], values())
        ss = None
        for g in range(GROUP):
            c0 = (h * GROUP + g) * HEAD_DIM
            og = o[g * n_q:(g + 1) * n_q, :]
            acc_ref[q_rows, c0:c0 + HEAD_DIM] = og
            ss = _sumsq(og) if ss is None else ss + _sumsq(og)
        ss_ref[q_rows, :] = ss if h == 0 else ss_ref[q_rows, :] + ss


def _latent_attn_kernel(sink_ref, q_ref, kp_ref, kc_ref, kn_ref, vp_ref, vc_ref, vn_ref,
                        kx_ref, vx_ref, g_ref, *rest, n_ctx, q_blocks, n_cast):
    raw_refs = rest[:n_cast]
    o_ref = rest[n_cast]
    cast_refs = rest[n_cast + 1:2 * n_cast + 1]
    acc_ref, ss_ref, s_ref, p_ref = rest[2 * n_cast + 1:]
    _cast_slabs(raw_refs, cast_refs)
    n = pl.program_id(0)
    nb = pl.num_programs(0)
    ctx_tiles = n_ctx // BLOCK
    prev_lo = jnp.where(n == 0, BLOCK, 0)
    next_hi = jnp.where(n == nb - 1, 0, BLOCK)

    def block_of(refs, i):
        prev_ref, cur_ref, next_ref = refs
        if i == 0:
            return prev_ref, slice(0, BLOCK)
        if i == q_blocks + 1:
            return next_ref, slice(0, BLOCK)
        return cur_ref, slice((i - 1) * BLOCK, i * BLOCK)

    def operand(ctx_ref, refs, qb, h):
        hs = slice(h * HEAD_DIM, (h + 1) * HEAD_DIM)
        parts = [ctx_ref[:, hs]]
        for i in range(qb, qb + 3):
            ref, rows = block_of(refs, i)
            parts.append(ref[rows, hs])
        return jnp.concatenate(parts, axis=0)

    def mask_fn(qb, q0):
        qi = lax.broadcasted_iota(jnp.int32, (SOFTMAX_ROWS, BLOCK), 0) + q0
        kj = lax.broadcasted_iota(jnp.int32, (SOFTMAX_ROWS, BLOCK), 1)
        lo = kj >= qi
        hi = kj <= qi
        if qb == 0:
            lo = lo & (kj >= prev_lo)
        if qb == q_blocks - 1:
            hi = hi & (kj < next_hi)
        return {ctx_tiles: lo, ctx_tiles + 2: hi}

    units = []
    for qb in range(q_blocks):
        for h in range(N_KV_HEADS):
            units.append((
                slice(qb * BLOCK, (qb + 1) * BLOCK), h,
                functools.partial(operand, kx_ref, (kp_ref, kc_ref, kn_ref), qb, h),
                functools.partial(operand, vx_ref, (vp_ref, vc_ref, vn_ref), qb, h),
                functools.partial(mask_fn, qb),
            ))
    _attn_units(sink_ref, q_ref, units, acc_ref, ss_ref, s_ref, p_ref)
    _rms_rows_from_sumsq(acc_ref, ss_ref, g_ref, o_ref)


def _latent_attn(px, pc, sink, g_attn, cast=()):
    n_rows = px.shape[0]
    n_ctx = pc.shape[0]
    nb = n_rows // BLOCK
    qb = ATTN_Q_BLOCKS if nb % ATTN_Q_BLOCKS == 0 else 1
    kcol = Q_WIDTH // KV_WIDTH
    vcol = kcol + 1
    n_keys = n_ctx + 3 * BLOCK

    def kv_specs(colblk):
        return [
            pl.BlockSpec((BLOCK, KV_WIDTH), lambda n: (jnp.maximum(n * qb - 1, 0), colblk)),
            pl.BlockSpec((qb * BLOCK, KV_WIDTH), lambda n: (n, colblk)),
            pl.BlockSpec((BLOCK, KV_WIDTH), lambda n: (jnp.minimum(n * qb + qb, nb - 1), colblk)),
        ]

    in_specs = [
        pl.BlockSpec(memory_space=pltpu.SMEM),
        pl.BlockSpec((qb * BLOCK, Q_WIDTH), lambda n: (n, 0)),
        *kv_specs(kcol), *kv_specs(vcol),
        pl.BlockSpec((n_ctx, KV_WIDTH), lambda n: (0, kcol)),
        pl.BlockSpec((n_ctx, KV_WIDTH), lambda n: (0, vcol)),
        pl.BlockSpec((1, Q_WIDTH), lambda n: (0, 0)),
    ]
    c_in, c_out, c_shape, c_args, c_vmem = _cast_specs(cast, nb // qb, lambda n: n)
    outs = pl.pallas_call(
        functools.partial(_latent_attn_kernel, n_ctx=n_ctx, q_blocks=qb, n_cast=len(cast)),
        grid=(nb // qb,),
        in_specs=in_specs + c_in,
        out_specs=[pl.BlockSpec((qb * BLOCK, Q_WIDTH), lambda n: (n, 0)), *c_out],
        out_shape=[jax.ShapeDtypeStruct((n_rows, Q_WIDTH), BF16), *c_shape],
        scratch_shapes=[
            pltpu.VMEM((qb * BLOCK, Q_WIDTH), F32),
            pltpu.VMEM((qb * BLOCK, 1), F32),
            pltpu.VMEM((2, GROUP * BLOCK, n_keys), F32),
            pltpu.VMEM((2, GROUP * BLOCK, n_keys), BF16),
        ],
        compiler_params=_params(("parallel",), (24 << 20) + c_vmem),
        cost_estimate=pl.CostEstimate(
            flops=4 * n_rows * N_HEADS * n_keys * HEAD_DIM, transcendentals=n_rows * N_HEADS * n_keys,
            bytes_accessed=(n_rows * (2 * Q_WIDTH + 6 * KV_WIDTH) * 2
                            + sum(6 * s.shape[1] * s.shape[2] for s in c_shape))),
        name="latent_attn",
    )(sink, px, px, px, px, px, px, px, pc, pc, g_attn.reshape(1, Q_WIDTH), *c_args)
    return outs[0], tuple(outs[1:])


def _ctx_attn_kernel(sink_ref, q_ref, k_ref, v_ref, g_ref, o_ref, acc_ref, ss_ref, s_ref, p_ref):
    def operand(ref, h):
        return ref[:, h * HEAD_DIM:(h + 1) * HEAD_DIM]

    units = [(slice(0, q_ref.shape[0]), h, functools.partial(operand, k_ref, h),
              functools.partial(operand, v_ref, h), lambda q0: {}) for h in range(N_KV_HEADS)]
    _attn_units(sink_ref, q_ref, units, acc_ref, ss_ref, s_ref, p_ref)
    _rms_rows_from_sumsq(acc_ref, ss_ref, g_ref, o_ref)


def _ctx_attn(pc, sink, g_attn):
    n_ctx = pc.shape[0]
    kcol = Q_WIDTH // KV_WIDTH
    return pl.pallas_call(
        _ctx_attn_kernel,
        grid=(1,),
        in_specs=[
            pl.BlockSpec(memory_space=pltpu.SMEM),
            pl.BlockSpec((n_ctx, Q_WIDTH), lambda i: (0, 0)),
            pl.BlockSpec((n_ctx, KV_WIDTH), lambda i: (0, kcol)),
            pl.BlockSpec((n_ctx, KV_WIDTH), lambda i: (0, kcol + 1)),
            pl.BlockSpec((1, Q_WIDTH), lambda i: (0, 0)),
        ],
        out_specs=pl.BlockSpec((n_ctx, Q_WIDTH), lambda i: (0, 0)),
        out_shape=jax.ShapeDtypeStruct((n_ctx, Q_WIDTH), BF16),
        scratch_shapes=[
            pltpu.VMEM((n_ctx, Q_WIDTH), F32),
            pltpu.VMEM((n_ctx, 1), F32),
            pltpu.VMEM((2, GROUP * n_ctx, n_ctx), F32),
            pltpu.VMEM((2, GROUP * n_ctx, n_ctx), BF16),
        ],
        compiler_params=_params(("arbitrary",), 16 << 20),
        name="ctx_attn",
    )(sink, pc, pc, pc, g_attn.reshape(1, Q_WIDTH))


def _dft_tables(n):
    ang = 2.0 * np.pi * (np.outer(np.arange(n), np.arange(n)) % n) / n
    return np.cos(ang), np.sin(ang)


def _fourier_head_kernel(g_ref, z_ref, o_ref):
    for r in range(z_ref.shape[0]):
        o_ref[r] = _dot(g_ref[r], z_ref[r]).astype(o_ref.dtype)


def _fourier_head(zt, gmat):
    nb, na, c = zt.shape
    rb = min(HEAD_RESIDUES, nb)
    return pl.pallas_call(
        _fourier_head_kernel,
        grid=(nb // rb,),
        in_specs=[
            pl.BlockSpec((rb, 2 * na, na), lambda b: (b, 0, 0)),
            pl.BlockSpec((rb, na, c), lambda b: (b, 0, 0)),
        ],
        out_specs=pl.BlockSpec((rb, 2 * na, c), lambda b: (b, 0, 0)),
        out_shape=jax.ShapeDtypeStruct((nb, 2 * na, c), BF16),
        compiler_params=_params(("parallel",), 6 * rb * na * c * 2),
        cost_estimate=pl.CostEstimate(
            flops=4 * nb * na * na * c, transcendentals=0,
            bytes_accessed=(3 * nb * na * c + 2 * nb * na * na) * 2),
        name="fourier_head",
    )(gmat, zt)


def _fourier_tail_kernel(m_ref, d_ref, cs_ref, wf_ref, g_ref, *rest, pos_scale):
    o_ref, acc_ref, ss_ref = rest[-3:]
    kb, p = o_ref.shape[0], o_ref.shape[1]
    xs = [(_dot(m_ref[...], d_ref[blk]) * pos_scale).astype(BF16) for blk in range(kb)]
    lhs = jnp.concatenate(
        [jnp.concatenate([x[:p, g * F_DIM:(g + 1) * F_DIM], x[p:, g * F_DIM:(g + 1) * F_DIM]], axis=1)
         for g in range(N_FGROUPS) for x in xs], axis=0)
    f = (_dot(lhs, cs_ref[...]) * (F_DIM ** -0.5)).astype(BF16)
    for g in range(N_FGROUPS):
        og = _dot(f[g * kb * p:(g + 1) * kb * p, :], wf_ref[g])
        for blk in range(kb):
            piece = og[blk * p:(blk + 1) * p, :]
            acc_ref[blk, :, g * F_DIM:(g + 1) * F_DIM] = piece
            ss_ref[blk] = _sumsq(piece) if g == 0 else ss_ref[blk] + _sumsq(piece)
    for blk in range(kb):
        _rms_rows_from_sumsq(acc_ref.at[blk], ss_ref.at[blk], g_ref, o_ref.at[blk])


def _fourier_tail(stage_mat, data, cs, wf_all, layer, g_four, pos_scale, run_after=None):
    nblk, k_in, c = data.shape
    p = stage_mat.shape[0] // 2
    kb = min(TAIL_BLOCKS, nblk)
    temps = kb * p * c * (4 + 2 + 2 + 4 + 2)
    vmem = 2 * kb * (k_in + p) * c * 2 + kb * p * c * 4 + 2 * (2 * p * k_in + 6 * F_DIM * F_DIM) * 2 + temps
    in_specs = [
        pl.BlockSpec((2 * p, k_in), lambda i: (0, 0)),
        pl.BlockSpec((kb, k_in, c), lambda i: (i, 0, 0)),
        pl.BlockSpec((2 * F_DIM, F_DIM), lambda i: (0, 0)),
        pl.BlockSpec((None, N_FGROUPS, F_DIM, F_DIM), lambda i: (layer, 0, 0, 0)),
        pl.BlockSpec((1, c), lambda i: (0, 0)),
    ]
    args = [stage_mat, data, cs, wf_all, g_four.reshape(1, c)]
    if run_after is not None:
        in_specs.append(pl.BlockSpec((16, 128), lambda i: (0, 0)))
        args.append(run_after)
    return pl.pallas_call(
        functools.partial(_fourier_tail_kernel, pos_scale=pos_scale),
        grid=(nblk // kb,),
        in_specs=in_specs,
        out_specs=pl.BlockSpec((kb, p, c), lambda i: (i, 0, 0)),
        out_shape=jax.ShapeDtypeStruct((nblk, p, c), BF16),
        scratch_shapes=[pltpu.VMEM((kb, p, c), F32), pltpu.VMEM((kb, p, 1), F32)],
        compiler_params=_params(("parallel",), vmem),
        cost_estimate=pl.CostEstimate(
            flops=2 * nblk * (2 * p * k_in * c + p * c * 2 * F_DIM + p * c * F_DIM), transcendentals=nblk * p,
            bytes_accessed=nblk * (k_in + p) * c * 2 + 2 * p * k_in * 2 + 6 * F_DIM * F_DIM * 2),
        name="fourier_tail",
    )(*args)


class _FourierConsts:
    def __init__(self, n_lat, n_ctx):
        a = b = int(round(math.sqrt(n_lat)))
        assert a * b == n_lat
        self.a, self.b = a, b
        k_lo = np.arange(a)[None, :, None]
        n = (np.arange(a)[None, None, :] * b + np.arange(b)[:, None, None])
        ang = 2.0 * np.pi * ((k_lo * n) % n_lat) / n_lat
        self.head = jnp.asarray(np.concatenate([np.cos(ang), -np.sin(ang)], axis=1), BF16)
        cb, sb = _dft_tables(b)
        self.tail = jnp.asarray(np.block([[cb, sb], [-sb, cb]]), BF16)
        cc, sc = _dft_tables(n_ctx)
        self.ctx = jnp.asarray(np.concatenate([cc, -sc], axis=0), BF16)
        cf, sf = _dft_tables(F_DIM)
        self.chan = jnp.asarray(np.concatenate([cf, sf], axis=0), BF16)


def _latent_fourier(pf, fc, wf_all, layer, g_four, run_after):
    n_rows = pf.shape[0]
    a, b = fc.a, fc.b
    zt = jnp.transpose(pf.reshape(a, b, F_WIDTH), (1, 0, 2))
    t = _fourier_head(zt, fc.head)
    t = jnp.transpose(t.reshape(b, 2, a, F_WIDTH), (2, 1, 0, 3)).reshape(a, 2 * b, F_WIDTH)
    o = _fourier_tail(fc.tail, t, fc.chan, wf_all, layer, g_four, 1.0 / math.sqrt(n_rows),
                      run_after=run_after)
    return jnp.transpose(o, (1, 0, 2)).reshape(n_rows, F_WIDTH)


def _ctx_fourier(pf, fc, wf_all, layer, g_four):
    n_ctx = pf.shape[0]
    d = pf.reshape(1, n_ctx, F_WIDTH)
    o = _fourier_tail(fc.ctx, d, fc.chan, wf_all, layer, g_four, 1.0 / math.sqrt(n_ctx))
    return o.reshape(n_ctx, F_WIDTH)


def _outproj_kernel(na_ref, nf_ref, w_ref, x_ref, gate_ref, o_ref):
    ka = na_ref.shape[1]
    acc = _dot(na_ref[...], w_ref[:ka, :]) + _dot(nf_ref[...], w_ref[ka:, :])
    o_ref[...] = x_ref[...] + gate_ref[...] * acc


def _outproj(na, nf, w_all, layer, x, gate, tm, tn):
    n_rows, d = x.shape
    ka, kf = na.shape[1], nf.shape[1]
    vmem = 2 * (tm * (ka + kf) * 2 + (ka + kf) * tn * 2 + 2 * tm * tn * 4) + tm * tn * 4
    return pl.pallas_call(
        _outproj_kernel,
        grid=(n_rows // tm, d // tn),
        in_specs=[
            pl.BlockSpec((tm, ka), lambda i, j: (i, 0)),
            pl.BlockSpec((tm, kf), lambda i, j: (i, 0)),
            pl.BlockSpec((None, ka + kf, tn), lambda i, j: (layer, 0, j)),
            pl.BlockSpec((tm, tn), lambda i, j: (i, j)),
            pl.BlockSpec((1, tn), lambda i, j: (0, j)),
        ],
        out_specs=pl.BlockSpec((tm, tn), lambda i, j: (i, j)),
        out_shape=jax.ShapeDtypeStruct((n_rows, d), F32),
        compiler_params=_params(("parallel", "parallel"), vmem),
        cost_estimate=pl.CostEstimate(
            flops=2 * n_rows * (ka + kf) * d, transcendentals=0,
            bytes_accessed=n_rows * (ka + kf) * 2 + (n_rows // tm) * (ka + kf) * d * 2 + 2 * n_rows * d * 4),
        name="outproj",
    )(na, nf, w_all, x, gate.reshape(1, d))


def _mlp_kernel(x_ref, gss_ref, w1_ref, w2_ref, *rest, final, n_cast):
    raw_refs = rest[:n_cast]
    o_ref = rest[n_cast]
    cast_refs = rest[n_cast + 1:2 * n_cast + 1]
    h_ref, ops_ref = rest[2 * n_cast + 1:]
    f = pl.program_id(1)

    @pl.when(f == 0)
    def _():
        _normmod_rows(x_ref, gss_ref, ops_ref, h_ref, copy_ref=o_ref)

    _cast_slabs(raw_refs, cast_refs)
    u = jnp.maximum(_dot(h_ref[...], w1_ref[...]), 0.0)
    u = (u * u).astype(BF16)
    tn = MLP_OUT_CHUNK
    for c in range(o_ref.shape[1] // tn):
        cs = slice(c * tn, (c + 1) * tn)
        o_ref[:, cs] += gss_ref[3:4, cs] * _dot(u, w2_ref[:, cs])

    if final:
        @pl.when(f == pl.num_programs(1) - 1)
        def _():
            _rms_rows(o_ref, gss_ref.at[4:5, :], o_ref)


def _mlp(x, gss, w1_all, w2_all, layer, tm, tf, final, cast_next=()):
    n_rows, d = x.shape
    d_ff = w1_all.shape[2]
    n_f = d_ff // tf
    c_in, c_out, c_shape, c_args, c_vmem = _cast_specs(
        cast_next, (n_rows // tm) * n_f, lambda i, f: i * n_f + f)
    vmem = (2 * (2 * tm * d * 4 + 2 * d * tf * 2) + tm * d * 2 + tm * tf * 6
            + 2 * tm * MLP_OUT_CHUNK * 4 + c_vmem)
    outs = pl.pallas_call(
        functools.partial(_mlp_kernel, final=final, n_cast=len(cast_next)),
        grid=(n_rows // tm, n_f),
        in_specs=[
            pl.BlockSpec((tm, d), lambda i, f: (i, 0)),
            pl.BlockSpec((8, d), lambda i, f: (0, 0)),
            pl.BlockSpec((None, d, tf), lambda i, f: (layer, 0, f)),
            pl.BlockSpec((None, tf, d), lambda i, f: (layer, f, 0)),
            *c_in,
        ],
        out_specs=[pl.BlockSpec((tm, d), lambda i, f: (i, 0)), *c_out],
        out_shape=[jax.ShapeDtypeStruct((n_rows, d), F32), *c_shape],
        scratch_shapes=[pltpu.VMEM((tm, d), BF16), pltpu.VMEM((8, d), F32)],
        compiler_params=_params(("parallel", "arbitrary"), vmem),
        cost_estimate=pl.CostEstimate(
            flops=4 * n_rows * d * d_ff, transcendentals=n_rows,
            bytes_accessed=(2 * n_rows * d * 4 + (n_rows // tm) * 2 * d * d_ff * 2
                            + sum(6 * s.shape[1] * s.shape[2] for s in c_shape))),
        name="mlp_final" if final else "mlp",
    )(x, gss, w1_all, w2_all, *c_args)
    return outs[0], tuple(outs[1:])


def _rope_tables(n_rows):
    quarter = HEAD_DIM // 4
    inv = ROPE_THETA ** (-jnp.arange(quarter, dtype=F32) / quarter)
    pos = jnp.arange(n_rows)
    rows = (pos // GRID_W).astype(F32)
    cols = (pos % GRID_W).astype(F32)
    ang = jnp.concatenate([rows[:, None] * inv[None, :]] * 2 + [cols[:, None] * inv[None, :]] * 2, axis=1)
    sign = jnp.tile(jnp.concatenate([-jnp.ones(quarter, F32), jnp.ones(quarter, F32)]), 2)
    return jnp.cos(ang), jnp.sin(ang) * sign[None, :]


def _rows8(*vecs):
    d = vecs[0].shape[0]
    pad = [jnp.zeros((d,), F32)] * (8 - len(vecs))
    return jnp.stack(list(vecs) + pad, axis=0)


def _row_tile(n_rows, want):
    return min(want, n_rows)


def kernel(x, c, ctx, c_ctx, ada_a, ada_b, ada_bias, g_mix, w_in, sink, w_f, g_attn_out, g_four_out,
           w_out, g_mlp, w1, w2, g_final):
    assert x.shape[0] == 1 and ctx.shape[0] == 1
    depth = w_in.shape[0]
    d = x.shape[2]
    xs = x[0]
    cs = ctx[0]
    n_lat, n_ctx = xs.shape[0], cs.shape[0]

    cv = jnp.zeros((ADA_ROWS, d), F32).at[0].set(c[0]).at[1].set(c_ctx)
    mods = _ada(cv, ada_a, ada_b, ada_bias)
    rope = _rope_tables(n_lat)
    fconst = _FourierConsts(n_lat, n_ctx)
    wf_b = w_f.astype(BF16)
    w_in_b = w_in[0:1].astype(BF16)
    w_out_b = w1_b = w2_b = None

    for l in range(depth):
        last = l == depth - 1
        m_lat = [mods[l, 0, i * d:(i + 1) * d] for i in range(N_MOD)]
        m_ctx = [mods[l, 1, i * d:(i + 1) * d] for i in range(N_MOD)]

        hc = _normmod(cs, _rows8(g_mix[l], m_ctx[0], m_ctx[1]), _row_tile(n_ctx, 256))
        hx = _normmod(xs, _rows8(g_mix[l], m_lat[0], m_lat[1]), _row_tile(n_lat, 1024))
        pc, pcf, _ = _inproj(hc, w_in_b, 0, None, _row_tile(n_ctx, 1024), 1024)
        px, pxf, cast1 = _inproj(hx, w_in_b, 0, rope, _row_tile(n_lat, 1024), 1024,
                                 cast=((w1, 0),) if l == 0 else ())

        na, cast2 = _latent_attn(px, pc, sink[l], g_attn_out[l],
                                 cast=((w2, 0), (w_out, 0)) if l == 0 else ())
        if l == 0:
            (w1_b,), (w2_b, w_out_b) = cast1, cast2
        nf = _latent_fourier(pxf, fconst, wf_b, l, g_four_out[l], run_after=pc)
        x_mid = _outproj(na, nf, w_out_b, 0, xs, m_lat[2], _row_tile(n_lat, 1024), 1024)
        gss = _rows8(g_mlp[l], m_lat[3], m_lat[4], m_lat[5], g_final)
        cast_next = () if last else tuple((w, l + 1) for w in (w_in, w_out, w1, w2))
        xs, next_weights = _mlp(x_mid, gss, w1_b, w2_b, 0, _row_tile(n_lat, 512), 512, final=last,
                                cast_next=cast_next)

        if not last:
            nac = _ctx_attn(pc, sink[l], g_attn_out[l])
            nfc = _ctx_fourier(pcf, fconst, wf_b, l, g_four_out[l])
            c_mid = _outproj(nac, nfc, w_out_b, 0, cs, m_ctx[2], _row_tile(n_ctx, 1024), 1024)
            gss_c = _rows8(g_mlp[l], m_ctx[3], m_ctx[4], m_ctx[5])
            cs, _ = _mlp(c_mid, gss_c, w1_b, w2_b, 0, _row_tile(n_ctx, 512), 512, final=False)
            w_in_b, w_out_b, w1_b, w2_b = next_weights

    return xs[None]
```

```python
import functools
import math

import numpy as np
import jax
import jax.numpy as jnp
from jax import lax
from jax.experimental import pallas as pl
from jax.experimental.pallas import tpu as pltpu

F32 = jnp.float32
BF16 = jnp.bfloat16

N_HEADS = 16
N_KV_HEADS = 4
HEAD_DIM = 128
GROUP = N_HEADS // N_KV_HEADS
Q_WIDTH = N_HEADS * HEAD_DIM
KV_WIDTH = N_KV_HEADS * HEAD_DIM
BLOCK = 128
GRID_W = 64
N_FGROUPS = 4
F_DIM = 512
F_WIDTH = N_FGROUPS * F_DIM
N_MOD = 6
ROPE_THETA = 10000.0
EPS = 1e-6
NEG_INF = -1e30
LOG2E = 1.4426950408889634

V7X_VMEM_BYTES = 64 * 1024 * 1024
ADA_ROWS = 8
MLP_OUT_CHUNK = 512
MLP_SLOTS = 2
ROW_CHUNK = 16
ROW_UNROLL = 2
STREAM_UNROLL = 8
DOT_COLS = 256
SOFTMAX_ROWS = 32
ATTN_Q_BLOCKS = 2
HEAD_RESIDUES = 8
TAIL_BLOCKS = 4


def _params(semantics, vmem_bytes):
    limit = min(int(vmem_bytes * 1.15) + (4 << 20), V7X_VMEM_BYTES - (2 << 20))
    return pltpu.CompilerParams(dimension_semantics=semantics, vmem_limit_bytes=limit)


def _dot(a, b):
    return jnp.dot(a, b, preferred_element_type=F32)


def _dot_nt(a, b):
    return lax.dot_general(a, b, (((1,), (1,)), ((), ())), preferred_element_type=F32)


def _rms(x, g):
    return x * lax.rsqrt(jnp.mean(x * x, axis=-1, keepdims=True) + EPS) * g


def _for_row_chunks(n_rows, body, unroll=ROW_UNROLL):
    rc = min(ROW_CHUNK, n_rows)

    def step(r, carry):
        body(pl.ds(pl.multiple_of(r * rc, rc), rc))
        return carry

    lax.fori_loop(0, n_rows // rc, step, 0, unroll=min(unroll, n_rows // rc))


def _normmod_rows(x_ref, gss_ref, ops_ref, h_ref, copy_ref=None):
    ops_ref[0:1, :] = gss_ref[0:1, :] * (1.0 + gss_ref[2:3, :])

    def body(rows):
        x = x_ref[rows, :]
        y = x * lax.rsqrt(jnp.mean(x * x, axis=-1, keepdims=True) + EPS)
        h_ref[rows, :] = (y * ops_ref[0:1, :] + gss_ref[1:2, :]).astype(h_ref.dtype)
        if copy_ref is not None:
            copy_ref[rows, :] = x

    _for_row_chunks(x_ref.shape[0], body)


def _rms_rows(src_ref, g_ref, dst_ref):
    def body(rows):
        dst_ref[rows, :] = _rms(src_ref[rows, :], g_ref[...]).astype(dst_ref.dtype)

    _for_row_chunks(src_ref.shape[0], body)


def _rms_rows_from_sumsq(src_ref, ss_ref, g_ref, dst_ref):
    width = src_ref.shape[1]

    def body(rows):
        inv = lax.rsqrt(ss_ref[rows, :] * (1.0 / width) + EPS)
        dst_ref[rows, :] = (src_ref[rows, :] * inv * g_ref[...]).astype(dst_ref.dtype)

    _for_row_chunks(src_ref.shape[0], body, unroll=STREAM_UNROLL)


def _sumsq(x):
    return jnp.sum(x * x, axis=-1, keepdims=True)


def _cast_slab(shape, n_steps):
    n_r, n_c = shape
    for q in (1, 2, 4, 8, 16):
        if n_steps % q or n_c % (q * 128) or n_r % (n_steps // q):
            continue
        br = n_r // (n_steps // q)
        if br % 16 == 0:
            return br, n_c // q, q
    raise ValueError(f"no bf16-tile-aligned slab split of {shape} into {n_steps} steps")


def _cast_specs(jobs, n_steps, step_of):
    ins, outs, shapes, args, vmem = [], [], [], [], 0
    for w_raw, src in jobs:
        br, bc, q = _cast_slab(w_raw.shape[1:], n_steps)
        ins.append(pl.BlockSpec(
            (None, br, bc), lambda *g, src=src, q=q: (src, step_of(*g) // q, step_of(*g) % q)))
        outs.append(pl.BlockSpec((None, br, bc), lambda *g, q=q: (0, step_of(*g) // q, step_of(*g) % q)))
        shapes.append(jax.ShapeDtypeStruct((1,) + w_raw.shape[1:], BF16))
        args.append(w_raw)
        vmem += 2 * br * bc * 6
    return ins, outs, shapes, args, vmem


def _cast_slabs(raw_refs, cast_refs):
    for raw_ref, cast_ref in zip(raw_refs, cast_refs):
        cast_ref[...] = raw_ref[...].astype(cast_ref.dtype)


def _ada_kernel(cv_ref, a_ref, b_ref, bias_ref, o_ref, h_ref):
    @pl.when(pl.program_id(1) == 0)
    def _():
        cv = cv_ref[...]
        s = cv * jax.nn.sigmoid(cv)
        h_ref[...] = _dot(s.astype(BF16), a_ref[0].astype(BF16))

    o_ref[0] = _dot(h_ref[...].astype(BF16), b_ref[0].astype(BF16)) + bias_ref[0]


def _ada(cv, ada_a, ada_b, ada_bias):
    depth, d, rank = ada_a.shape
    n = ada_b.shape[2]
    tn = d
    vmem = 2 * (d * rank * 4 + rank * tn * 4) + 4 * ADA_ROWS * (d + tn) * 4
    return pl.pallas_call(
        _ada_kernel,
        grid=(depth, n // tn),
        in_specs=[
            pl.BlockSpec((ADA_ROWS, d), lambda l, j: (0, 0)),
            pl.BlockSpec((1, d, rank), lambda l, j: (l, 0, 0)),
            pl.BlockSpec((1, rank, tn), lambda l, j: (l, 0, j)),
            pl.BlockSpec((1, 1, tn), lambda l, j: (l, 0, j)),
        ],
        out_specs=pl.BlockSpec((1, ADA_ROWS, tn), lambda l, j: (l, 0, j)),
        out_shape=jax.ShapeDtypeStruct((depth, ADA_ROWS, n), F32),
        scratch_shapes=[pltpu.VMEM((ADA_ROWS, rank), F32)],
        compiler_params=_params(("parallel", "arbitrary"), vmem),
        name="ada",
    )(cv, ada_a, ada_b, ada_bias.reshape(depth, 1, n))


def _rope_chunk(a, cos, sin, lane):
    up = pltpu.roll(a, HEAD_DIM - 32, axis=1)
    down = pltpu.roll(a, 32, axis=1)
    partner = jnp.where((lane % 64) < 32, up, down)
    return a * cos + partner * sin


def _normmod_kernel(x_ref, gss_ref, h_ref, ops_ref):
    _normmod_rows(x_ref, gss_ref, ops_ref, h_ref)


def _normmod(x, gss, tm):
    n_rows, d = x.shape
    return pl.pallas_call(
        _normmod_kernel,
        grid=(n_rows // tm,),
        in_specs=[pl.BlockSpec((tm, d), lambda i: (i, 0)), pl.BlockSpec((8, d), lambda i: (0, 0))],
        out_specs=pl.BlockSpec((tm, d), lambda i: (i, 0)),
        out_shape=jax.ShapeDtypeStruct((n_rows, d), BF16),
        scratch_shapes=[pltpu.VMEM((8, d), F32)],
        compiler_params=_params(("parallel",), 2 * tm * d * 6 + 16 * d * 4),
        cost_estimate=pl.CostEstimate(flops=8 * n_rows * d, transcendentals=n_rows, bytes_accessed=n_rows * d * 6),
        name="normmod",
    )(x, gss)


def _inproj_kernel(h_ref, w_ref, *rest, rope_cols, tn, n_cast):
    if rope_cols:
        cos_ref, sin_ref = rest[:2]
        rest = rest[2:]
    raw_refs = rest[:n_cast]
    qkv_ref, f_ref = rest[n_cast:n_cast + 2]
    cast_refs = rest[n_cast + 2:]
    j = pl.program_id(1)
    heads_per_dot = DOT_COLS // HEAD_DIM

    def tile(n_rope, o_ref):
        _cast_slabs(raw_refs, cast_refs)
        if n_rope:
            lane = lax.broadcasted_iota(jnp.int32, (h_ref.shape[0], HEAD_DIM), 1)
        for c in range(tn // DOT_COLS):
            acc = _dot(h_ref[...], w_ref[:, c * DOT_COLS:(c + 1) * DOT_COLS])
            for k in range(heads_per_dot):
                head = c * heads_per_dot + k
                a = acc[:, k * HEAD_DIM:(k + 1) * HEAD_DIM]
                if head < n_rope:
                    a = _rope_chunk(a, cos_ref[...], sin_ref[...], lane)
                o_ref[:, head * HEAD_DIM:(head + 1) * HEAD_DIM] = a.astype(o_ref.dtype)

    qkv_tiles = (Q_WIDTH + 2 * KV_WIDTH) // tn
    full_tiles, part = divmod(rope_cols, tn)
    if full_tiles:
        pl.when(j < full_tiles)(functools.partial(tile, tn // HEAD_DIM, qkv_ref))
    if part:
        pl.when(j == full_tiles)(functools.partial(tile, part // HEAD_DIM, qkv_ref))
        full_tiles += 1
    if full_tiles < qkv_tiles:
        pl.when((j >= full_tiles) & (j < qkv_tiles))(functools.partial(tile, 0, qkv_ref))
    pl.when(j >= qkv_tiles)(functools.partial(tile, 0, f_ref))


def _inproj(h, w_all, layer, rope_tables, tm, tn, cast=()):
    n_rows, d = h.shape
    n_out = w_all.shape[2]
    qkv_cols = Q_WIDTH + 2 * KV_WIDTH
    qkv_tiles = qkv_cols // tn
    n_j = n_out // tn
    rope_cols = Q_WIDTH + KV_WIDTH if rope_tables is not None else 0
    in_specs = [
        pl.BlockSpec((tm, d), lambda i, j: (i, 0)),
        pl.BlockSpec((None, d, tn), lambda i, j: (layer, 0, j)),
    ]
    args = [h, w_all]
    if rope_tables is not None:
        in_specs += [pl.BlockSpec((tm, HEAD_DIM), lambda i, j: (i, 0))] * 2
        args += list(rope_tables)
    slabs = 1 << (n_j.bit_length() - 1)
    c_in, c_out, c_shape, c_args, c_vmem = _cast_specs(
        cast, (n_rows // tm) * slabs, lambda i, j: i * slabs + jnp.minimum(j, slabs - 1))
    vmem = (2 * (tm * d * 2 + d * tn * 2 + 2 * tm * tn * 2 + 2 * tm * HEAD_DIM * 4) + 4 * tm * DOT_COLS * 4
            + c_vmem)
    cost = pl.CostEstimate(
        flops=2 * n_rows * d * n_out, transcendentals=0,
        bytes_accessed=(n_rows * d * 2 + (n_rows // tm) * d * n_out * 2 + n_rows * n_out * 2
                        + sum(6 * s.shape[1] * s.shape[2] for s in c_shape)))
    outs = pl.pallas_call(
        functools.partial(_inproj_kernel, rope_cols=rope_cols, tn=tn, n_cast=len(cast)),
        grid=(n_rows // tm, n_j),
        in_specs=in_specs + c_in,
        out_specs=[
            pl.BlockSpec((tm, tn), lambda i, j: (i, jnp.minimum(j, qkv_tiles - 1))),
            pl.BlockSpec((tm, tn), lambda i, j: (i, jnp.maximum(j - qkv_tiles, 0))),
            *c_out,
        ],
        out_shape=[
            jax.ShapeDtypeStruct((n_rows, qkv_cols), BF16),
            jax.ShapeDtypeStruct((n_rows, n_out - qkv_cols), BF16),
            *c_shape,
        ],
        compiler_params=_params(("parallel", "arbitrary"), vmem),
        cost_estimate=cost,
        name="inproj_rope" if rope_cols else "inproj_ctx",
    )(*args, *c_args)
    return outs[0], outs[1], tuple(outs[2:])


def _stack_heads(q_ref, rows, h):
    return jnp.concatenate(
        [q_ref[rows, (h * GROUP + g) * HEAD_DIM:(h * GROUP + g + 1) * HEAD_DIM] for g in range(GROUP)], axis=0)


def _softmax_chunk(s_ref, p_ref, rows, sink, masks):
    n_tiles = s_ref.shape[1] // BLOCK
    tiles = []
    for t in range(n_tiles):
        v = s_ref[rows, t * BLOCK:(t + 1) * BLOCK]
        if masks.get(t) is not None:
            v = jnp.where(masks[t], v, NEG_INF)
        tiles.append(v)
    m_raw = jnp.max(functools.reduce(jnp.maximum, tiles), axis=-1, keepdims=True)
    m = jnp.maximum(m_raw * (HEAD_DIM ** -0.5), sink)
    mb = m * LOG2E
    es = [jnp.exp2(v * (HEAD_DIM ** -0.5 * LOG2E) - mb) for v in tiles]
    denom = jnp.sum(functools.reduce(jnp.add, es), axis=-1, keepdims=True) + jnp.exp2(sink * LOG2E - mb)
    inv = 1.0 / denom
    for t in range(n_tiles):
        p_ref[rows, t * BLOCK:(t + 1) * BLOCK] = (es[t] * inv).astype(p_ref.dtype)


def _attn_units(sink_ref, q_ref, units, acc_ref, ss_ref, s_ref, p_ref):
    def scores(u):
        q_rows, h, keys, _, _ = units[u]
        s_ref[u % 2] = _dot_nt(_stack_heads(q_ref, q_rows, h), keys())

    scores(0)
    for u, (q_rows, h, _, values, mask_fn) in enumerate(units):
        if u + 1 < len(units):
            scores(u + 1)
        n_q = q_rows.stop - q_rows.start
        chunks_per_head = n_q // SOFTMAX_ROWS
        for r in range(GROUP * chunks_per_head):
            rows = slice(r * SOFTMAX_ROWS, (r + 1) * SOFTMAX_ROWS)
            sink = sink_ref[h * GROUP + r // chunks_per_head]
            q0 = (r % chunks_per_head) * SOFTMAX_ROWS
            _softmax_chunk(s_ref.at[u % 2], p_ref.at[u % 2], rows, sink, mask_fn(q0))
        o = _dot(p_ref[u % 2], values())
        ss = None
        for g in range(GROUP):
            c0 = (h * GROUP + g) * HEAD_DIM
            og = o[g * n_q:(g + 1) * n_q, :]
            acc_ref[q_rows, c0:c0 + HEAD_DIM] = og
            ss = _sumsq(og) if ss is None else ss + _sumsq(og)
        ss_ref[q_rows, :] = ss if h == 0 else ss_ref[q_rows, :] + ss


def _latent_attn_kernel(sink_ref, q_ref, kp_ref, kc_ref, kn_ref, vp_ref, vc_ref, vn_ref,
                        kx_ref, vx_ref, g_ref, *rest, n_ctx, q_blocks, n_cast):
    raw_refs = rest[:n_cast]
    o_ref = rest[n_cast]
    cast_refs = rest[n_cast + 1:2 * n_cast + 1]
    acc_ref, ss_ref, s_ref, p_ref = rest[2 * n_cast + 1:]
    _cast_slabs(raw_refs, cast_refs)
    n = pl.program_id(0)
    nb = pl.num_programs(0)
    ctx_tiles = n_ctx // BLOCK
    prev_lo = jnp.where(n == 0, BLOCK, 0)
    next_hi = jnp.where(n == nb - 1, 0, BLOCK)

    def block_of(refs, i):
        prev_ref, cur_ref, next_ref = refs
        if i == 0:
            return prev_ref, slice(0, BLOCK)
        if i == q_blocks + 1:
            return next_ref, slice(0, BLOCK)
        return cur_ref, slice((i - 1) * BLOCK, i * BLOCK)

    def operand(ctx_ref, refs, qb, h):
        hs = slice(h * HEAD_DIM, (h + 1) * HEAD_DIM)
        parts = [ctx_ref[:, hs]]
        for i in range(qb, qb + 3):
            ref, rows = block_of(refs, i)
            parts.append(ref[rows, hs])
        return jnp.concatenate(parts, axis=0)

    def mask_fn(qb, q0):
        qi = lax.broadcasted_iota(jnp.int32, (SOFTMAX_ROWS, BLOCK), 0) + q0
        kj = lax.broadcasted_iota(jnp.int32, (SOFTMAX_ROWS, BLOCK), 1)
        lo = kj >= qi
        hi = kj <= qi
        if qb == 0:
            lo = lo & (kj >= prev_lo)
        if qb == q_blocks - 1:
            hi = hi & (kj < next_hi)
        return {ctx_tiles: lo, ctx_tiles + 2: hi}

    units = []
    for qb in range(q_blocks):
        for h in range(N_KV_HEADS):
            units.append((
                slice(qb * BLOCK, (qb + 1) * BLOCK), h,
                functools.partial(operand, kx_ref, (kp_ref, kc_ref, kn_ref), qb, h),
                functools.partial(operand, vx_ref, (vp_ref, vc_ref, vn_ref), qb, h),
                functools.partial(mask_fn, qb),
            ))
    _attn_units(sink_ref, q_ref, units, acc_ref, ss_ref, s_ref, p_ref)
    _rms_rows_from_sumsq(acc_ref, ss_ref, g_ref, o_ref)


def _latent_attn(px, pc, sink, g_attn, cast=()):
    n_rows = px.shape[0]
    n_ctx = pc.shape[0]
    nb = n_rows // BLOCK
    qb = ATTN_Q_BLOCKS if nb % ATTN_Q_BLOCKS == 0 else 1
    kcol = Q_WIDTH // KV_WIDTH
    vcol = kcol + 1
    n_keys = n_ctx + 3 * BLOCK

    def kv_specs(colblk):
        return [
            pl.BlockSpec((BLOCK, KV_WIDTH), lambda n: (jnp.maximum(n * qb - 1, 0), colblk)),
            pl.BlockSpec((qb * BLOCK, KV_WIDTH), lambda n: (n, colblk)),
            pl.BlockSpec((BLOCK, KV_WIDTH), lambda n: (jnp.minimum(n * qb + qb, nb - 1), colblk)),
        ]

    in_specs = [
        pl.BlockSpec(memory_space=pltpu.SMEM),
        pl.BlockSpec((qb * BLOCK, Q_WIDTH), lambda n: (n, 0)),
        *kv_specs(kcol), *kv_specs(vcol),
        pl.BlockSpec((n_ctx, KV_WIDTH), lambda n: (0, kcol)),
        pl.BlockSpec((n_ctx, KV_WIDTH), lambda n: (0, vcol)),
        pl.BlockSpec((1, Q_WIDTH), lambda n: (0, 0)),
    ]
    c_in, c_out, c_shape, c_args, c_vmem = _cast_specs(cast, nb // qb, lambda n: n)
    outs = pl.pallas_call(
        functools.partial(_latent_attn_kernel, n_ctx=n_ctx, q_blocks=qb, n_cast=len(cast)),
        grid=(nb // qb,),
        in_specs=in_specs + c_in,
        out_specs=[pl.BlockSpec((qb * BLOCK, Q_WIDTH), lambda n: (n, 0)), *c_out],
        out_shape=[jax.ShapeDtypeStruct((n_rows, Q_WIDTH), BF16), *c_shape],
        scratch_shapes=[
            pltpu.VMEM((qb * BLOCK, Q_WIDTH), F32),
            pltpu.VMEM((qb * BLOCK, 1), F32),
            pltpu.VMEM((2, GROUP * BLOCK, n_keys), F32),
            pltpu.VMEM((2, GROUP * BLOCK, n_keys), BF16),
        ],
        compiler_params=_params(("parallel",), (24 << 20) + c_vmem),
        cost_estimate=pl.CostEstimate(
            flops=4 * n_rows * N_HEADS * n_keys * HEAD_DIM, transcendentals=n_rows * N_HEADS * n_keys,
            bytes_accessed=(n_rows * (2 * Q_WIDTH + 6 * KV_WIDTH) * 2
                            + sum(6 * s.shape[1] * s.shape[2] for s in c_shape))),
        name="latent_attn",
    )(sink, px, px, px, px, px, px, px, pc, pc, g_attn.reshape(1, Q_WIDTH), *c_args)
    return outs[0], tuple(outs[1:])


def _ctx_attn_kernel(sink_ref, q_ref, k_ref, v_ref, g_ref, o_ref, acc_ref, ss_ref, s_ref, p_ref):
    def operand(ref, h):
        return ref[:, h * HEAD_DIM:(h + 1) * HEAD_DIM]

    units = [(slice(0, q_ref.shape[0]), h, functools.partial(operand, k_ref, h),
              functools.partial(operand, v_ref, h), lambda q0: {}) for h in range(N_KV_HEADS)]
    _attn_units(sink_ref, q_ref, units, acc_ref, ss_ref, s_ref, p_ref)
    _rms_rows_from_sumsq(acc_ref, ss_ref, g_ref, o_ref)


def _ctx_attn(pc, sink, g_attn):
    n_ctx = pc.shape[0]
    kcol = Q_WIDTH // KV_WIDTH
    return pl.pallas_call(
        _ctx_attn_kernel,
        grid=(1,),
        in_specs=[
            pl.BlockSpec(memory_space=pltpu.SMEM),
            pl.BlockSpec((n_ctx, Q_WIDTH), lambda i: (0, 0)),
            pl.BlockSpec((n_ctx, KV_WIDTH), lambda i: (0, kcol)),
            pl.BlockSpec((n_ctx, KV_WIDTH), lambda i: (0, kcol + 1)),
            pl.BlockSpec((1, Q_WIDTH), lambda i: (0, 0)),
        ],
        out_specs=pl.BlockSpec((n_ctx, Q_WIDTH), lambda i: (0, 0)),
        out_shape=jax.ShapeDtypeStruct((n_ctx, Q_WIDTH), BF16),
        scratch_shapes=[
            pltpu.VMEM((n_ctx, Q_WIDTH), F32),
            pltpu.VMEM((n_ctx, 1), F32),
            pltpu.VMEM((2, GROUP * n_ctx, n_ctx), F32),
            pltpu.VMEM((2, GROUP * n_ctx, n_ctx), BF16),
        ],
        compiler_params=_params(("arbitrary",), 16 << 20),
        name="ctx_attn",
    )(sink, pc, pc, pc, g_attn.reshape(1, Q_WIDTH))


def _dft_tables(n):
    ang = 2.0 * np.pi * (np.outer(np.arange(n), np.arange(n)) % n) / n
    return np.cos(ang), np.sin(ang)


def _fourier_head_kernel(g_ref, z_ref, o_ref):
    for r in range(z_ref.shape[0]):
        o_ref[r] = _dot(g_ref[r], z_ref[r]).astype(o_ref.dtype)


def _fourier_head(zt, gmat):
    nb, na, c = zt.shape
    rb = min(HEAD_RESIDUES, nb)
    return pl.pallas_call(
        _fourier_head_kernel,
        grid=(nb // rb,),
        in_specs=[
            pl.BlockSpec((rb, 2 * na, na), lambda b: (b, 0, 0)),
            pl.BlockSpec((rb, na, c), lambda b: (b, 0, 0)),
        ],
        out_specs=pl.BlockSpec((rb, 2 * na, c), lambda b: (b, 0, 0)),
        out_shape=jax.ShapeDtypeStruct((nb, 2 * na, c), BF16),
        compiler_params=_params(("parallel",), 6 * rb * na * c * 2),
        cost_estimate=pl.CostEstimate(
            flops=4 * nb * na * na * c, transcendentals=0,
            bytes_accessed=(3 * nb * na * c + 2 * nb * na * na) * 2),
        name="fourier_head",
    )(gmat, zt)


def _fourier_tail_kernel(m_ref, d_ref, cs_ref, wf_ref, g_ref, *rest, pos_scale):
    o_ref, acc_ref, ss_ref = rest[-3:]
    kb, p = o_ref.shape[0], o_ref.shape[1]
    xs = [(_dot(m_ref[...], d_ref[blk]) * pos_scale).astype(BF16) for blk in range(kb)]
    lhs = jnp.concatenate(
        [jnp.concatenate([x[:p, g * F_DIM:(g + 1) * F_DIM], x[p:, g * F_DIM:(g + 1) * F_DIM]], axis=1)
         for g in range(N_FGROUPS) for x in xs], axis=0)
    f = (_dot(lhs, cs_ref[...]) * (F_DIM ** -0.5)).astype(BF16)
    for g in range(N_FGROUPS):
        og = _dot(f[g * kb * p:(g + 1) * kb * p, :], wf_ref[g])
        for blk in range(kb):
            piece = og[blk * p:(blk + 1) * p, :]
            acc_ref[blk, :, g * F_DIM:(g + 1) * F_DIM] = piece
            ss_ref[blk] = _sumsq(piece) if g == 0 else ss_ref[blk] + _sumsq(piece)
    for blk in range(kb):
        _rms_rows_from_sumsq(acc_ref.at[blk], ss_ref.at[blk], g_ref, o_ref.at[blk])


def _fourier_tail(stage_mat, data, cs, wf_all, layer, g_four, pos_scale, run_after=None):
    nblk, k_in, c = data.shape
    p = stage_mat.shape[0] // 2
    kb = min(TAIL_BLOCKS, nblk)
    temps = kb * p * c * (4 + 2 + 2 + 4 + 2)
    vmem = 2 * kb * (k_in + p) * c * 2 + kb * p * c * 4 + 2 * (2 * p * k_in + 6 * F_DIM * F_DIM) * 2 + temps
    in_specs = [
        pl.BlockSpec((2 * p, k_in), lambda i: (0, 0)),
        pl.BlockSpec((kb, k_in, c), lambda i: (i, 0, 0)),
        pl.BlockSpec((2 * F_DIM, F_DIM), lambda i: (0, 0)),
        pl.BlockSpec((None, N_FGROUPS, F_DIM, F_DIM), lambda i: (layer, 0, 0, 0)),
        pl.BlockSpec((1, c), lambda i: (0, 0)),
    ]
    args = [stage_mat, data, cs, wf_all, g_four.reshape(1, c)]
    if run_after is not None:
        in_specs.append(pl.BlockSpec((16, 128), lambda i: (0, 0)))
        args.append(run_after)
    return pl.pallas_call(
        functools.partial(_fourier_tail_kernel, pos_scale=pos_scale),
        grid=(nblk // kb,),
        in_specs=in_specs,
        out_specs=pl.BlockSpec((kb, p, c), lambda i: (i, 0, 0)),
        out_shape=jax.ShapeDtypeStruct((nblk, p, c), BF16),
        scratch_shapes=[pltpu.VMEM((kb, p, c), F32), pltpu.VMEM((kb, p, 1), F32)],
        compiler_params=_params(("parallel",), vmem),
        cost_estimate=pl.CostEstimate(
            flops=2 * nblk * (2 * p * k_in * c + p * c * 2 * F_DIM + p * c * F_DIM), transcendentals=nblk * p,
            bytes_accessed=nblk * (k_in + p) * c * 2 + 2 * p * k_in * 2 + 6 * F_DIM * F_DIM * 2),
        name="fourier_tail",
    )(*args)


class _FourierConsts:
    def __init__(self, n_lat, n_ctx):
        a = b = int(round(math.sqrt(n_lat)))
        assert a * b == n_lat
        self.a, self.b = a, b
        k_lo = np.arange(a)[None, :, None]
        n = (np.arange(a)[None, None, :] * b + np.arange(b)[:, None, None])
        ang = 2.0 * np.pi * ((k_lo * n) % n_lat) / n_lat
        self.head = jnp.asarray(np.concatenate([np.cos(ang), -np.sin(ang)], axis=1), BF16)
        cb, sb = _dft_tables(b)
        self.tail = jnp.asarray(np.block([[cb, sb], [-sb, cb]]), BF16)
        cc, sc = _dft_tables(n_ctx)
        self.ctx = jnp.asarray(np.concatenate([cc, -sc], axis=0), BF16)
        cf, sf = _dft_tables(F_DIM)
        self.chan = jnp.asarray(np.concatenate([cf, sf], axis=0), BF16)


def _latent_fourier(pf, fc, wf_all, layer, g_four, run_after):
    n_rows = pf.shape[0]
    a, b = fc.a, fc.b
    zt = jnp.transpose(pf.reshape(a, b, F_WIDTH), (1, 0, 2))
    t = _fourier_head(zt, fc.head)
    t = jnp.transpose(t.reshape(b, 2, a, F_WIDTH), (2, 1, 0, 3)).reshape(a, 2 * b, F_WIDTH)
    o = _fourier_tail(fc.tail, t, fc.chan, wf_all, layer, g_four, 1.0 / math.sqrt(n_rows),
                      run_after=run_after)
    return jnp.transpose(o, (1, 0, 2)).reshape(n_rows, F_WIDTH)


def _ctx_fourier(pf, fc, wf_all, layer, g_four):
    n_ctx = pf.shape[0]
    d = pf.reshape(1, n_ctx, F_WIDTH)
    o = _fourier_tail(fc.ctx, d, fc.chan, wf_all, layer, g_four, 1.0 / math.sqrt(n_ctx))
    return o.reshape(n_ctx, F_WIDTH)


def _outproj_kernel(na_ref, nf_ref, w_ref, x_ref, gate_ref, o_ref):
    ka = na_ref.shape[1]
    acc = _dot(na_ref[...], w_ref[:ka, :]) + _dot(nf_ref[...], w_ref[ka:, :])
    o_ref[...] = x_ref[...] + gate_ref[...] * acc


def _outproj(na, nf, w_all, layer, x, gate, tm, tn):
    n_rows, d = x.shape
    ka, kf = na.shape[1], nf.shape[1]
    vmem = 2 * (tm * (ka + kf) * 2 + (ka + kf) * tn * 2 + 2 * tm * tn * 4) + tm * tn * 4
    return pl.pallas_call(
        _outproj_kernel,
        grid=(n_rows // tm, d // tn),
        in_specs=[
            pl.BlockSpec((tm, ka), lambda i, j: (i, 0)),
            pl.BlockSpec((tm, kf), lambda i, j: (i, 0)),
            pl.BlockSpec((None, ka + kf, tn), lambda i, j: (layer, 0, j)),
            pl.BlockSpec((tm, tn), lambda i, j: (i, j)),
            pl.BlockSpec((1, tn), lambda i, j: (0, j)),
        ],
        out_specs=pl.BlockSpec((tm, tn), lambda i, j: (i, j)),
        out_shape=jax.ShapeDtypeStruct((n_rows, d), F32),
        compiler_params=_params(("parallel", "parallel"), vmem),
        cost_estimate=pl.CostEstimate(
            flops=2 * n_rows * (ka + kf) * d, transcendentals=0,
            bytes_accessed=n_rows * (ka + kf) * 2 + (n_rows // tm) * (ka + kf) * d * 2 + 2 * n_rows * d * 4),
        name="outproj",
    )(na, nf, w_all, x, gate.reshape(1, d))


def _mlp_kernel(x_ref, gss_ref, w1_hbm, w2_hbm, *rest, final, n_cast):
    raw_refs = rest[:n_cast]
    o_ref = rest[n_cast]
    cast_refs = rest[n_cast + 1:2 * n_cast + 1]
    h_ref, ops_ref, w1_buf, w2_buf, w_sem = rest[2 * n_cast + 1:]
    i, f = pl.program_id(0), pl.program_id(1)
    n_i, n_f = pl.num_programs(0), pl.num_programs(1)
    tf = w1_buf.shape[2]
    n_slabs = n_f * MLP_SLOTS

    def slab_copies(slab, slot):
        col = pl.multiple_of(slab * tf, tf)
        return (
            pltpu.make_async_copy(w1_hbm.at[:, pl.ds(col, tf)], w1_buf.at[slot], w_sem.at[0, slot]),
            pltpu.make_async_copy(w2_hbm.at[pl.ds(col, tf), :], w2_buf.at[slot], w_sem.at[1, slot]),
        )

    @pl.when((i == 0) & (f == 0))
    def _():
        for slot in range(MLP_SLOTS):
            for cp in slab_copies(slot, slot):
                cp.start()

    @pl.when(f == 0)
    def _():
        _normmod_rows(x_ref, gss_ref, ops_ref, h_ref, copy_ref=o_ref)

    _cast_slabs(raw_refs, cast_refs)
    tn = MLP_OUT_CHUNK
    for slot in range(MLP_SLOTS):
        for cp in slab_copies(0, slot):
            cp.wait()
        u = jnp.maximum(_dot(h_ref[...], w1_buf[slot]), 0.0)
        u = (u * u).astype(BF16)
        for c in range(o_ref.shape[1] // tn):
            cs = slice(c * tn, (c + 1) * tn)
            o_ref[:, cs] += gss_ref[3:4, cs] * _dot(u, w2_buf[slot, :, cs])
        for cp in slab_copies((f * MLP_SLOTS + slot + MLP_SLOTS) % n_slabs, slot):
            cp.start()

    @pl.when((i == n_i - 1) & (f == n_f - 1))
    def _():
        for slot in range(MLP_SLOTS):
            for cp in slab_copies(0, slot):
                cp.wait()

    if final:
        @pl.when(f == n_f - 1)
        def _():
            _rms_rows(o_ref, gss_ref.at[4:5, :], o_ref)


def _mlp(x, gss, w1_all, w2_all, layer, tm, tf, final, cast_next=()):
    n_rows, d = x.shape
    d_ff = w1_all.shape[2]
    n_f = d_ff // (tf * MLP_SLOTS)
    c_in, c_out, c_shape, c_args, c_vmem = _cast_specs(
        cast_next, (n_rows // tm) * n_f, lambda i, f: i * n_f + f)
    vmem = (2 * 2 * tm * d * 4 + MLP_SLOTS * 2 * d * tf * 2 + tm * d * 2 + tm * tf * 6
            + 2 * tm * MLP_OUT_CHUNK * 4 + c_vmem)
    outs = pl.pallas_call(
        functools.partial(_mlp_kernel, final=final, n_cast=len(cast_next)),
        grid=(n_rows // tm, n_f),
        in_specs=[
            pl.BlockSpec((tm, d), lambda i, f: (i, 0)),
            pl.BlockSpec((8, d), lambda i, f: (0, 0)),
            pl.BlockSpec(memory_space=pl.ANY),
            pl.BlockSpec(memory_space=pl.ANY),
            *c_in,
        ],
        out_specs=[pl.BlockSpec((tm, d), lambda i, f: (i, 0)), *c_out],
        out_shape=[jax.ShapeDtypeStruct((n_rows, d), F32), *c_shape],
        scratch_shapes=[
            pltpu.VMEM((tm, d), BF16), pltpu.VMEM((8, d), F32),
            pltpu.VMEM((MLP_SLOTS, d, tf), BF16), pltpu.VMEM((MLP_SLOTS, tf, d), BF16),
            pltpu.SemaphoreType.DMA((2, MLP_SLOTS)),
        ],
        compiler_params=_params(("arbitrary", "arbitrary"), vmem),
        cost_estimate=pl.CostEstimate(
            flops=4 * n_rows * d * d_ff, transcendentals=n_rows,
            bytes_accessed=(2 * n_rows * d * 4 + (n_rows // tm) * 2 * d * d_ff * 2
                            + sum(6 * s.shape[1] * s.shape[2] for s in c_shape))),
        name="mlp_final" if final else "mlp",
    )(x, gss, w1_all[layer], w2_all[layer], *c_args)
    return outs[0], tuple(outs[1:])


def _rope_tables(n_rows):
    quarter = HEAD_DIM // 4
    inv = ROPE_THETA ** (-jnp.arange(quarter, dtype=F32) / quarter)
    pos = jnp.arange(n_rows)
    rows = (pos // GRID_W).astype(F32)
    cols = (pos % GRID_W).astype(F32)
    ang = jnp.concatenate([rows[:, None] * inv[None, :]] * 2 + [cols[:, None] * inv[None, :]] * 2, axis=1)
    sign = jnp.tile(jnp.concatenate([-jnp.ones(quarter, F32), jnp.ones(quarter, F32)]), 2)
    return jnp.cos(ang), jnp.sin(ang) * sign[None, :]


def _rows8(*vecs):
    d = vecs[0].shape[0]
    pad = [jnp.zeros((d,), F32)] * (8 - len(vecs))
    return jnp.stack(list(vecs) + pad, axis=0)


def _row_tile(n_rows, want):
    return min(want, n_rows)


def kernel(x, c, ctx, c_ctx, ada_a, ada_b, ada_bias, g_mix, w_in, sink, w_f, g_attn_out, g_four_out,
           w_out, g_mlp, w1, w2, g_final):
    assert x.shape[0] == 1 and ctx.shape[0] == 1
    depth = w_in.shape[0]
    d = x.shape[2]
    xs = x[0]
    cs = ctx[0]
    n_lat, n_ctx = xs.shape[0], cs.shape[0]

    cv = jnp.zeros((ADA_ROWS, d), F32).at[0].set(c[0]).at[1].set(c_ctx)
    mods = _ada(cv, ada_a, ada_b, ada_bias)
    rope = _rope_tables(n_lat)
    fconst = _FourierConsts(n_lat, n_ctx)
    wf_b = w_f.astype(BF16)
    w_in_b = w_in[0:1].astype(BF16)
    w_out_b = w1_b = w2_b = None

    for l in range(depth):
        last = l == depth - 1
        m_lat = [mods[l, 0, i * d:(i + 1) * d] for i in range(N_MOD)]
        m_ctx = [mods[l, 1, i * d:(i + 1) * d] for i in range(N_MOD)]

        hc = _normmod(cs, _rows8(g_mix[l], m_ctx[0], m_ctx[1]), _row_tile(n_ctx, 256))
        hx = _normmod(xs, _rows8(g_mix[l], m_lat[0], m_lat[1]), _row_tile(n_lat, 1024))
        pc, pcf, _ = _inproj(hc, w_in_b, 0, None, _row_tile(n_ctx, 1024), 1024)
        px, pxf, cast1 = _inproj(hx, w_in_b, 0, rope, _row_tile(n_lat, 1024), 1024,
                                 cast=((w1, 0),) if l == 0 else ())

        na, cast2 = _latent_attn(px, pc, sink[l], g_attn_out[l],
                                 cast=((w2, 0), (w_out, 0)) if l == 0 else ())
        if l == 0:
            (w1_b,), (w2_b, w_out_b) = cast1, cast2
        nf = _latent_fourier(pxf, fconst, wf_b, l, g_four_out[l], run_after=pc)
        x_mid = _outproj(na, nf, w_out_b, 0, xs, m_lat[2], _row_tile(n_lat, 1024), 1024)
        gss = _rows8(g_mlp[l], m_lat[3], m_lat[4], m_lat[5], g_final)
        cast_next = () if last else tuple((w, l + 1) for w in (w_in, w_out, w1, w2))
        xs, next_weights = _mlp(x_mid, gss, w1_b, w2_b, 0, _row_tile(n_lat, 512), 512, final=last,
                                cast_next=cast_next)

        if not last:
            nac = _ctx_attn(pc, sink[l], g_attn_out[l])
            nfc = _ctx_fourier(pcf, fconst, wf_b, l, g_four_out[l])
            c_mid = _outproj(nac, nfc, w_out_b, 0, cs, m_ctx[2], _row_tile(n_ctx, 1024), 1024)
            gss_c = _rows8(g_mlp[l], m_ctx[3], m_ctx[4], m_ctx[5])
            cs, _ = _mlp(c_mid, gss_c, w1_b, w2_b, 0, _row_tile(n_ctx, 512), 512, final=False)
            w_in_b, w_out_b, w1_b, w2_b = next_weights

    return xs[None]
```

```python
import functools
import math

import numpy as np
import jax
import jax.numpy as jnp
from jax import lax
from jax.experimental import pallas as pl
from jax.experimental.pallas import tpu as pltpu

F32 = jnp.float32
BF16 = jnp.bfloat16

N_HEADS = 16
N_KV_HEADS = 4
HEAD_DIM = 128
GROUP = N_HEADS // N_KV_HEADS
Q_WIDTH = N_HEADS * HEAD_DIM
KV_WIDTH = N_KV_HEADS * HEAD_DIM
BLOCK = 128
GRID_W = 64
N_FGROUPS = 4
F_DIM = 512
F_WIDTH = N_FGROUPS * F_DIM
N_MOD = 6
ROPE_THETA = 10000.0
EPS = 1e-6
NEG_INF = -1e30
LOG2E = 1.4426950408889634

V7X_VMEM_BYTES = 64 * 1024 * 1024
ADA_ROWS = 8
MLP_OUT_CHUNK = 512
MLP_SLOTS = 2
ROW_CHUNK = 16
ROW_UNROLL = 2
STREAM_UNROLL = 8
DOT_COLS = 256
SOFTMAX_ROWS = 32
ATTN_Q_BLOCKS = 4
HEAD_RESIDUES = 16
TAIL_BLOCKS = 4


def _params(semantics, vmem_bytes):
    limit = min(int(vmem_bytes * 1.15) + (4 << 20), V7X_VMEM_BYTES - (2 << 20))
    return pltpu.CompilerParams(dimension_semantics=semantics, vmem_limit_bytes=limit)


def _dot(a, b):
    return jnp.dot(a, b, preferred_element_type=F32)


def _dot_nt(a, b):
    return lax.dot_general(a, b, (((1,), (1,)), ((), ())), preferred_element_type=F32)


def _rms(x, g):
    return x * lax.rsqrt(jnp.mean(x * x, axis=-1, keepdims=True) + EPS) * g


def _for_row_chunks(n_rows, body, unroll=ROW_UNROLL):
    rc = min(ROW_CHUNK, n_rows)

    def step(r, carry):
        body(pl.ds(pl.multiple_of(r * rc, rc), rc))
        return carry

    lax.fori_loop(0, n_rows // rc, step, 0, unroll=min(unroll, n_rows // rc))


def _normmod_rows(x_ref, gss_ref, ops_ref, h_ref, copy_ref=None):
    ops_ref[0:1, :] = gss_ref[0:1, :] * (1.0 + gss_ref[2:3, :])

    def body(rows):
        x = x_ref[rows, :]
        y = x * lax.rsqrt(jnp.mean(x * x, axis=-1, keepdims=True) + EPS)
        h_ref[rows, :] = (y * ops_ref[0:1, :] + gss_ref[1:2, :]).astype(h_ref.dtype)
        if copy_ref is not None:
            copy_ref[rows, :] = x

    _for_row_chunks(x_ref.shape[0], body)


def _rms_rows(src_ref, g_ref, dst_ref):
    def body(rows):
        dst_ref[rows, :] = _rms(src_ref[rows, :], g_ref[...]).astype(dst_ref.dtype)

    _for_row_chunks(src_ref.shape[0], body)


def _rms_rows_from_sumsq(src_ref, ss_ref, g_ref, dst_ref):
    width = src_ref.shape[1]

    def body(rows):
        inv = lax.rsqrt(ss_ref[rows, :] * (1.0 / width) + EPS)
        dst_ref[rows, :] = (src_ref[rows, :] * inv * g_ref[...]).astype(dst_ref.dtype)

    _for_row_chunks(src_ref.shape[0], body, unroll=STREAM_UNROLL)


def _sumsq(x):
    return jnp.sum(x * x, axis=-1, keepdims=True)


def _cast_slab(shape, n_steps):
    n_r, n_c = shape
    for q in (1, 2, 4, 8, 16):
        if n_steps % q or n_c % (q * 128) or n_r % (n_steps // q):
            continue
        br = n_r // (n_steps // q)
        if br % 16 == 0:
            return br, n_c // q, q
    raise ValueError(f"no bf16-tile-aligned slab split of {shape} into {n_steps} steps")


def _cast_specs(jobs, n_steps, step_of):
    ins, outs, shapes, args, vmem = [], [], [], [], 0
    for w_raw, src in jobs:
        br, bc, q = _cast_slab(w_raw.shape[1:], n_steps)
        ins.append(pl.BlockSpec(
            (None, br, bc), lambda *g, src=src, q=q: (src, step_of(*g) // q, step_of(*g) % q)))
        outs.append(pl.BlockSpec((None, br, bc), lambda *g, q=q: (0, step_of(*g) // q, step_of(*g) % q)))
        shapes.append(jax.ShapeDtypeStruct((1,) + w_raw.shape[1:], BF16))
        args.append(w_raw)
        vmem += 2 * br * bc * 6
    return ins, outs, shapes, args, vmem


def _cast_slabs(raw_refs, cast_refs):
    for raw_ref, cast_ref in zip(raw_refs, cast_refs):
        cast_ref[...] = raw_ref[...].astype(cast_ref.dtype)


def _ada_kernel(cv_ref, a_ref, b_ref, bias_ref, o_ref, h_ref):
    @pl.when(pl.program_id(1) == 0)
    def _():
        cv = cv_ref[...]
        s = cv * jax.nn.sigmoid(cv)
        h_ref[...] = _dot(s.astype(BF16), a_ref[0].astype(BF16))

    o_ref[0] = _dot(h_ref[...].astype(BF16), b_ref[0].astype(BF16)) + bias_ref[0]


def _ada(cv, ada_a, ada_b, ada_bias):
    depth, d, rank = ada_a.shape
    n = ada_b.shape[2]
    tn = d
    vmem = 2 * (d * rank * 4 + rank * tn * 4) + 4 * ADA_ROWS * (d + tn) * 4
    return pl.pallas_call(
        _ada_kernel,
        grid=(depth, n // tn),
        in_specs=[
            pl.BlockSpec((ADA_ROWS, d), lambda l, j: (0, 0)),
            pl.BlockSpec((1, d, rank), lambda l, j: (l, 0, 0)),
            pl.BlockSpec((1, rank, tn), lambda l, j: (l, 0, j)),
            pl.BlockSpec((1, 1, tn), lambda l, j: (l, 0, j)),
        ],
        out_specs=pl.BlockSpec((1, ADA_ROWS, tn), lambda l, j: (l, 0, j)),
        out_shape=jax.ShapeDtypeStruct((depth, ADA_ROWS, n), F32),
        scratch_shapes=[pltpu.VMEM((ADA_ROWS, rank), F32)],
        compiler_params=_params(("parallel", "arbitrary"), vmem),
        name="ada",
    )(cv, ada_a, ada_b, ada_bias.reshape(depth, 1, n))


def _rope_chunk(a, cos, sin, lane):
    up = pltpu.roll(a, HEAD_DIM - 32, axis=1)
    down = pltpu.roll(a, 32, axis=1)
    partner = jnp.where((lane % 64) < 32, up, down)
    return a * cos + partner * sin


def _normmod_kernel(x_ref, gss_ref, h_ref, ops_ref):
    _normmod_rows(x_ref, gss_ref, ops_ref, h_ref)


def _normmod(x, gss, tm):
    n_rows, d = x.shape
    return pl.pallas_call(
        _normmod_kernel,
        grid=(n_rows // tm,),
        in_specs=[pl.BlockSpec((tm, d), lambda i: (i, 0)), pl.BlockSpec((8, d), lambda i: (0, 0))],
        out_specs=pl.BlockSpec((tm, d), lambda i: (i, 0)),
        out_shape=jax.ShapeDtypeStruct((n_rows, d), BF16),
        scratch_shapes=[pltpu.VMEM((8, d), F32)],
        compiler_params=_params(("parallel",), 2 * tm * d * 6 + 16 * d * 4),
        cost_estimate=pl.CostEstimate(flops=8 * n_rows * d, transcendentals=n_rows, bytes_accessed=n_rows * d * 6),
        name="normmod",
    )(x, gss)


def _inproj_kernel(h_ref, w_ref, *rest, rope_cols, tn, n_cast):
    if rope_cols:
        cos_ref, sin_ref = rest[:2]
        rest = rest[2:]
    raw_refs = rest[:n_cast]
    qkv_ref, f_ref = rest[n_cast:n_cast + 2]
    cast_refs = rest[n_cast + 2:]
    j = pl.program_id(1)
    heads_per_dot = DOT_COLS // HEAD_DIM

    def tile(n_rope, o_ref):
        _cast_slabs(raw_refs, cast_refs)
        if n_rope:
            lane = lax.broadcasted_iota(jnp.int32, (h_ref.shape[0], HEAD_DIM), 1)
        for c in range(tn // DOT_COLS):
            acc = _dot(h_ref[...], w_ref[:, c * DOT_COLS:(c + 1) * DOT_COLS])
            for k in range(heads_per_dot):
                head = c * heads_per_dot + k
                a = acc[:, k * HEAD_DIM:(k + 1) * HEAD_DIM]
                if head < n_rope:
                    a = _rope_chunk(a, cos_ref[...], sin_ref[...], lane)
                o_ref[:, head * HEAD_DIM:(head + 1) * HEAD_DIM] = a.astype(o_ref.dtype)

    qkv_tiles = (Q_WIDTH + 2 * KV_WIDTH) // tn
    full_tiles, part = divmod(rope_cols, tn)
    if full_tiles:
        pl.when(j < full_tiles)(functools.partial(tile, tn // HEAD_DIM, qkv_ref))
    if part:
        pl.when(j == full_tiles)(functools.partial(tile, part // HEAD_DIM, qkv_ref))
        full_tiles += 1
    if full_tiles < qkv_tiles:
        pl.when((j >= full_tiles) & (j < qkv_tiles))(functools.partial(tile, 0, qkv_ref))
    pl.when(j >= qkv_tiles)(functools.partial(tile, 0, f_ref))


def _inproj(h, w_all, layer, rope_tables, tm, tn, cast=()):
    n_rows, d = h.shape
    n_out = w_all.shape[2]
    qkv_cols = Q_WIDTH + 2 * KV_WIDTH
    qkv_tiles = qkv_cols // tn
    n_j = n_out // tn
    rope_cols = Q_WIDTH + KV_WIDTH if rope_tables is not None else 0
    in_specs = [
        pl.BlockSpec((tm, d), lambda i, j: (i, 0)),
        pl.BlockSpec((None, d, tn), lambda i, j: (layer, 0, j)),
    ]
    args = [h, w_all]
    if rope_tables is not None:
        in_specs += [pl.BlockSpec((tm, HEAD_DIM), lambda i, j: (i, 0))] * 2
        args += list(rope_tables)
    slabs = 1 << (n_j.bit_length() - 1)
    c_in, c_out, c_shape, c_args, c_vmem = _cast_specs(
        cast, (n_rows // tm) * slabs, lambda i, j: i * slabs + jnp.minimum(j, slabs - 1))
    vmem = (2 * (tm * d * 2 + d * tn * 2 + 2 * tm * tn * 2 + 2 * tm * HEAD_DIM * 4) + 4 * tm * DOT_COLS * 4
            + c_vmem)
    cost = pl.CostEstimate(
        flops=2 * n_rows * d * n_out, transcendentals=0,
        bytes_accessed=(n_rows * d * 2 + (n_rows // tm) * d * n_out * 2 + n_rows * n_out * 2
                        + sum(6 * s.shape[1] * s.shape[2] for s in c_shape)))
    outs = pl.pallas_call(
        functools.partial(_inproj_kernel, rope_cols=rope_cols, tn=tn, n_cast=len(cast)),
        grid=(n_rows // tm, n_j),
        in_specs=in_specs + c_in,
        out_specs=[
            pl.BlockSpec((tm, tn), lambda i, j: (i, jnp.minimum(j, qkv_tiles - 1))),
            pl.BlockSpec((tm, tn), lambda i, j: (i, jnp.maximum(j - qkv_tiles, 0))),
            *c_out,
        ],
        out_shape=[
            jax.ShapeDtypeStruct((n_rows, qkv_cols), BF16),
            jax.ShapeDtypeStruct((n_rows, n_out - qkv_cols), BF16),
            *c_shape,
        ],
        compiler_params=_params(("parallel", "arbitrary"), vmem),
        cost_estimate=cost,
        name="inproj_rope" if rope_cols else "inproj_ctx",
    )(*args, *c_args)
    return outs[0], outs[1], tuple(outs[2:])


def _stack_heads(q_ref, rows, h):
    return jnp.concatenate(
        [q_ref[rows, (h * GROUP + g) * HEAD_DIM:(h * GROUP + g + 1) * HEAD_DIM] for g in range(GROUP)], axis=0)


def _softmax_chunk(s_ref, p_ref, rows, sink, masks):
    n_tiles = s_ref.shape[1] // BLOCK
    tiles = []
    for t in range(n_tiles):
        v = s_ref[rows, t * BLOCK:(t + 1) * BLOCK]
        if masks.get(t) is not None:
            v = jnp.where(masks[t], v, NEG_INF)
        tiles.append(v)
    m_raw = jnp.max(functools.reduce(jnp.maximum, tiles), axis=-1, keepdims=True)
    m = jnp.maximum(m_raw * (HEAD_DIM ** -0.5), sink)
    mb = m * LOG2E
    es = [jnp.exp2(v * (HEAD_DIM ** -0.5 * LOG2E) - mb) for v in tiles]
    denom = jnp.sum(functools.reduce(jnp.add, es), axis=-1, keepdims=True) + jnp.exp2(sink * LOG2E - mb)
    inv = 1.0 / denom
    for t in range(n_tiles):
        p_ref[rows, t * BLOCK:(t + 1) * BLOCK] = (es[t] * inv).astype(p_ref.dtype)


def _attn_units(sink_ref, q_ref, units, acc_ref, ss_ref, s_ref, p_ref):
    def scores(u):
        q_rows, h, keys, _, _ = units[u]
        s_ref[u % 2] = _dot_nt(_stack_heads(q_ref, q_rows, h), keys())

    scores(0)
    for u, (q_rows, h, _, values, mask_fn) in enumerate(units):
        if u + 1 < len(units):
            scores(u + 1)
        n_q = q_rows.stop - q_rows.start
        chunks_per_head = n_q // SOFTMAX_ROWS
        for r in range(GROUP * chunks_per_head):
            rows = slice(r * SOFTMAX_ROWS, (r + 1) * SOFTMAX_ROWS)
            sink = sink_ref[h * GROUP + r // chunks_per_head]
            q0 = (r % chunks_per_head) * SOFTMAX_ROWS
            _softmax_chunk(s_ref.at[u % 2], p_ref.at[u % 2], rows, sink, mask_fn(q0))
        o = _dot(p_ref[u % 2], values())
        ss = None
        for g in range(GROUP):
            c0 = (h * GROUP + g) * HEAD_DIM
            og = o[g * n_q:(g + 1) * n_q, :]
            acc_ref[q_rows, c0:c0 + HEAD_DIM] = og
            ss = _sumsq(og) if ss is None else ss + _sumsq(og)
        ss_ref[q_rows, :] = ss if h == 0 else ss_ref[q_rows, :] + ss


def _latent_attn_kernel(sink_ref, q_ref, kp_ref, kc_ref, kn_ref, vp_ref, vc_ref, vn_ref,
                        kx_ref, vx_ref, g_ref, *rest, n_ctx, q_blocks, n_cast):
    raw_refs = rest[:n_cast]
    o_ref = rest[n_cast]
    cast_refs = rest[n_cast + 1:2 * n_cast + 1]
    acc_ref, ss_ref, s_ref, p_ref = rest[2 * n_cast + 1:]
    _cast_slabs(raw_refs, cast_refs)
    n = pl.program_id(0)
    nb = pl.num_programs(0)
    ctx_tiles = n_ctx // BLOCK
    prev_lo = jnp.where(n == 0, BLOCK, 0)
    next_hi = jnp.where(n == nb - 1, 0, BLOCK)

    def block_of(refs, i):
        prev_ref, cur_ref, next_ref = refs
        if i == 0:
            return prev_ref, slice(0, BLOCK)
        if i == q_blocks + 1:
            return next_ref, slice(0, BLOCK)
        return cur_ref, slice((i - 1) * BLOCK, i * BLOCK)

    def operand(ctx_ref, refs, qb, h):
        hs = slice(h * HEAD_DIM, (h + 1) * HEAD_DIM)
        parts = [ctx_ref[:, hs]]
        for i in range(qb, qb + 3):
            ref, rows = block_of(refs, i)
            parts.append(ref[rows, hs])
        return jnp.concatenate(parts, axis=0)

    def mask_fn(qb, q0):
        qi = lax.broadcasted_iota(jnp.int32, (SOFTMAX_ROWS, BLOCK), 0) + q0
        kj = lax.broadcasted_iota(jnp.int32, (SOFTMAX_ROWS, BLOCK), 1)
        lo = kj >= qi
        hi = kj <= qi
        if qb == 0:
            lo = lo & (kj >= prev_lo)
        if qb == q_blocks - 1:
            hi = hi & (kj < next_hi)
        return {ctx_tiles: lo, ctx_tiles + 2: hi}

    units = []
    for qb in range(q_blocks):
        for h in range(N_KV_HEADS):
            units.append((
                slice(qb * BLOCK, (qb + 1) * BLOCK), h,
                functools.partial(operand, kx_ref, (kp_ref, kc_ref, kn_ref), qb, h),
                functools.partial(operand, vx_ref, (vp_ref, vc_ref, vn_ref), qb, h),
                functools.partial(mask_fn, qb),
            ))
    _attn_units(sink_ref, q_ref, units, acc_ref, ss_ref, s_ref, p_ref)
    _rms_rows_from_sumsq(acc_ref, ss_ref, g_ref, o_ref)


def _latent_attn(px, pc, sink, g_attn, cast=()):
    n_rows = px.shape[0]
    n_ctx = pc.shape[0]
    nb = n_rows // BLOCK
    qb = ATTN_Q_BLOCKS if nb % ATTN_Q_BLOCKS == 0 else 1
    kcol = Q_WIDTH // KV_WIDTH
    vcol = kcol + 1
    n_keys = n_ctx + 3 * BLOCK

    def kv_specs(colblk):
        return [
            pl.BlockSpec((BLOCK, KV_WIDTH), lambda n: (jnp.maximum(n * qb - 1, 0), colblk)),
            pl.BlockSpec((qb * BLOCK, KV_WIDTH), lambda n: (n, colblk)),
            pl.BlockSpec((BLOCK, KV_WIDTH), lambda n: (jnp.minimum(n * qb + qb, nb - 1), colblk)),
        ]

    in_specs = [
        pl.BlockSpec(memory_space=pltpu.SMEM),
        pl.BlockSpec((qb * BLOCK, Q_WIDTH), lambda n: (n, 0)),
        *kv_specs(kcol), *kv_specs(vcol),
        pl.BlockSpec((n_ctx, KV_WIDTH), lambda n: (0, kcol)),
        pl.BlockSpec((n_ctx, KV_WIDTH), lambda n: (0, vcol)),
        pl.BlockSpec((1, Q_WIDTH), lambda n: (0, 0)),
    ]
    c_in, c_out, c_shape, c_args, c_vmem = _cast_specs(cast, nb // qb, lambda n: n)
    outs = pl.pallas_call(
        functools.partial(_latent_attn_kernel, n_ctx=n_ctx, q_blocks=qb, n_cast=len(cast)),
        grid=(nb // qb,),
        in_specs=in_specs + c_in,
        out_specs=[pl.BlockSpec((qb * BLOCK, Q_WIDTH), lambda n: (n, 0)), *c_out],
        out_shape=[jax.ShapeDtypeStruct((n_rows, Q_WIDTH), BF16), *c_shape],
        scratch_shapes=[
            pltpu.VMEM((qb * BLOCK, Q_WIDTH), F32),
            pltpu.VMEM((qb * BLOCK, 1), F32),
            pltpu.VMEM((2, GROUP * BLOCK, n_keys), F32),
            pltpu.VMEM((2, GROUP * BLOCK, n_keys), BF16),
        ],
        compiler_params=_params(("parallel",), (24 << 20) + c_vmem),
        cost_estimate=pl.CostEstimate(
            flops=4 * n_rows * N_HEADS * n_keys * HEAD_DIM, transcendentals=n_rows * N_HEADS * n_keys,
            bytes_accessed=(n_rows * (2 * Q_WIDTH + 6 * KV_WIDTH) * 2
                            + sum(6 * s.shape[1] * s.shape[2] for s in c_shape))),
        name="latent_attn",
    )(sink, px, px, px, px, px, px, px, pc, pc, g_attn.reshape(1, Q_WIDTH), *c_args)
    return outs[0], tuple(outs[1:])


def _ctx_attn_kernel(sink_ref, q_ref, k_ref, v_ref, g_ref, o_ref, acc_ref, ss_ref, s_ref, p_ref):
    def operand(ref, h):
        return ref[:, h * HEAD_DIM:(h + 1) * HEAD_DIM]

    units = [(slice(0, q_ref.shape[0]), h, functools.partial(operand, k_ref, h),
              functools.partial(operand, v_ref, h), lambda q0: {}) for h in range(N_KV_HEADS)]
    _attn_units(sink_ref, q_ref, units, acc_ref, ss_ref, s_ref, p_ref)
    _rms_rows_from_sumsq(acc_ref, ss_ref, g_ref, o_ref)


def _ctx_attn(pc, sink, g_attn):
    n_ctx = pc.shape[0]
    kcol = Q_WIDTH // KV_WIDTH
    return pl.pallas_call(
        _ctx_attn_kernel,
        grid=(1,),
        in_specs=[
            pl.BlockSpec(memory_space=pltpu.SMEM),
            pl.BlockSpec((n_ctx, Q_WIDTH), lambda i: (0, 0)),
            pl.BlockSpec((n_ctx, KV_WIDTH), lambda i: (0, kcol)),
            pl.BlockSpec((n_ctx, KV_WIDTH), lambda i: (0, kcol + 1)),
            pl.BlockSpec((1, Q_WIDTH), lambda i: (0, 0)),
        ],
        out_specs=pl.BlockSpec((n_ctx, Q_WIDTH), lambda i: (0, 0)),
        out_shape=jax.ShapeDtypeStruct((n_ctx, Q_WIDTH), BF16),
        scratch_shapes=[
            pltpu.VMEM((n_ctx, Q_WIDTH), F32),
            pltpu.VMEM((n_ctx, 1), F32),
            pltpu.VMEM((2, GROUP * n_ctx, n_ctx), F32),
            pltpu.VMEM((2, GROUP * n_ctx, n_ctx), BF16),
        ],
        compiler_params=_params(("arbitrary",), 16 << 20),
        name="ctx_attn",
    )(sink, pc, pc, pc, g_attn.reshape(1, Q_WIDTH))


def _dft_tables(n):
    ang = 2.0 * np.pi * (np.outer(np.arange(n), np.arange(n)) % n) / n
    return np.cos(ang), np.sin(ang)


def _fourier_head_kernel(g_ref, z_ref, o_ref):
    for r in range(z_ref.shape[0]):
        o_ref[r] = _dot(g_ref[r], z_ref[r]).astype(o_ref.dtype)


def _fourier_head(zt, gmat):
    nb, na, c = zt.shape
    rb = min(HEAD_RESIDUES, nb)
    return pl.pallas_call(
        _fourier_head_kernel,
        grid=(nb // rb,),
        in_specs=[
            pl.BlockSpec((rb, 2 * na, na), lambda b: (b, 0, 0)),
            pl.BlockSpec((rb, na, c), lambda b: (b, 0, 0)),
        ],
        out_specs=pl.BlockSpec((rb, 2 * na, c), lambda b: (b, 0, 0)),
        out_shape=jax.ShapeDtypeStruct((nb, 2 * na, c), BF16),
        compiler_params=_params(("parallel",), 6 * rb * na * c * 2),
        cost_estimate=pl.CostEstimate(
            flops=4 * nb * na * na * c, transcendentals=0,
            bytes_accessed=(3 * nb * na * c + 2 * nb * na * na) * 2),
        name="fourier_head",
    )(gmat, zt)


def _fourier_tail_kernel(m_ref, d_ref, cs_ref, wf_ref, g_ref, *rest, pos_scale):
    o_ref, acc_ref, ss_ref = rest[-3:]
    kb, p = o_ref.shape[0], o_ref.shape[1]
    xs = [(_dot(m_ref[...], d_ref[blk]) * pos_scale).astype(BF16) for blk in range(kb)]
    lhs = jnp.concatenate(
        [jnp.concatenate([x[:p, g * F_DIM:(g + 1) * F_DIM], x[p:, g * F_DIM:(g + 1) * F_DIM]], axis=1)
         for g in range(N_FGROUPS) for x in xs], axis=0)
    f = (_dot(lhs, cs_ref[...]) * (F_DIM ** -0.5)).astype(BF16)
    for g in range(N_FGROUPS):
        og = _dot(f[g * kb * p:(g + 1) * kb * p, :], wf_ref[g])
        for blk in range(kb):
            piece = og[blk * p:(blk + 1) * p, :]
            acc_ref[blk, :, g * F_DIM:(g + 1) * F_DIM] = piece
            ss_ref[blk] = _sumsq(piece) if g == 0 else ss_ref[blk] + _sumsq(piece)
    for blk in range(kb):
        _rms_rows_from_sumsq(acc_ref.at[blk], ss_ref.at[blk], g_ref, o_ref.at[blk])


def _fourier_tail(stage_mat, data, cs, wf_all, layer, g_four, pos_scale, run_after=None):
    nblk, k_in, c = data.shape
    p = stage_mat.shape[0] // 2
    kb = min(TAIL_BLOCKS, nblk)
    temps = kb * p * c * (4 + 2 + 2 + 4 + 2)
    vmem = 2 * kb * (k_in + p) * c * 2 + kb * p * c * 4 + 2 * (2 * p * k_in + 6 * F_DIM * F_DIM) * 2 + temps
    in_specs = [
        pl.BlockSpec((2 * p, k_in), lambda i: (0, 0)),
        pl.BlockSpec((kb, k_in, c), lambda i: (i, 0, 0)),
        pl.BlockSpec((2 * F_DIM, F_DIM), lambda i: (0, 0)),
        pl.BlockSpec((None, N_FGROUPS, F_DIM, F_DIM), lambda i: (layer, 0, 0, 0)),
        pl.BlockSpec((1, c), lambda i: (0, 0)),
    ]
    args = [stage_mat, data, cs, wf_all, g_four.reshape(1, c)]
    if run_after is not None:
        in_specs.append(pl.BlockSpec((16, 128), lambda i: (0, 0)))
        args.append(run_after)
    return pl.pallas_call(
        functools.partial(_fourier_tail_kernel, pos_scale=pos_scale),
        grid=(nblk // kb,),
        in_specs=in_specs,
        out_specs=pl.BlockSpec((kb, p, c), lambda i: (i, 0, 0)),
        out_shape=jax.ShapeDtypeStruct((nblk, p, c), BF16),
        scratch_shapes=[pltpu.VMEM((kb, p, c), F32), pltpu.VMEM((kb, p, 1), F32)],
        compiler_params=_params(("parallel",), vmem),
        cost_estimate=pl.CostEstimate(
            flops=2 * nblk * (2 * p * k_in * c + p * c * 2 * F_DIM + p * c * F_DIM), transcendentals=nblk * p,
            bytes_accessed=nblk * (k_in + p) * c * 2 + 2 * p * k_in * 2 + 6 * F_DIM * F_DIM * 2),
        name="fourier_tail",
    )(*args)


class _FourierConsts:
    def __init__(self, n_lat, n_ctx):
        a = b = int(round(math.sqrt(n_lat)))
        assert a * b == n_lat
        self.a, self.b = a, b
        k_lo = np.arange(a)[None, :, None]
        n = (np.arange(a)[None, None, :] * b + np.arange(b)[:, None, None])
        ang = 2.0 * np.pi * ((k_lo * n) % n_lat) / n_lat
        self.head = jnp.asarray(np.concatenate([np.cos(ang), -np.sin(ang)], axis=1), BF16)
        cb, sb = _dft_tables(b)
        self.tail = jnp.asarray(np.block([[cb, sb], [-sb, cb]]), BF16)
        cc, sc = _dft_tables(n_ctx)
        self.ctx = jnp.asarray(np.concatenate([cc, -sc], axis=0), BF16)
        cf, sf = _dft_tables(F_DIM)
        self.chan = jnp.asarray(np.concatenate([cf, sf], axis=0), BF16)


def _latent_fourier(pf, fc, wf_all, layer, g_four, run_after):
    n_rows = pf.shape[0]
    a, b = fc.a, fc.b
    zt = jnp.transpose(pf.reshape(a, b, F_WIDTH), (1, 0, 2))
    t = _fourier_head(zt, fc.head)
    t = jnp.transpose(t.reshape(b, 2, a, F_WIDTH), (2, 1, 0, 3)).reshape(a, 2 * b, F_WIDTH)
    o = _fourier_tail(fc.tail, t, fc.chan, wf_all, layer, g_four, 1.0 / math.sqrt(n_rows),
                      run_after=run_after)
    return jnp.transpose(o, (1, 0, 2)).reshape(n_rows, F_WIDTH)


def _ctx_fourier(pf, fc, wf_all, layer, g_four):
    n_ctx = pf.shape[0]
    d = pf.reshape(1, n_ctx, F_WIDTH)
    o = _fourier_tail(fc.ctx, d, fc.chan, wf_all, layer, g_four, 1.0 / math.sqrt(n_ctx))
    return o.reshape(n_ctx, F_WIDTH)


def _outproj_kernel(na_ref, nf_ref, w_ref, x_ref, gate_ref, o_ref):
    ka = na_ref.shape[1]
    acc = _dot(na_ref[...], w_ref[:ka, :]) + _dot(nf_ref[...], w_ref[ka:, :])
    o_ref[...] = x_ref[...] + gate_ref[...] * acc


def _outproj(na, nf, w_all, layer, x, gate, tm, tn):
    n_rows, d = x.shape
    ka, kf = na.shape[1], nf.shape[1]
    vmem = 2 * (tm * (ka + kf) * 2 + (ka + kf) * tn * 2 + 2 * tm * tn * 4) + tm * tn * 4
    return pl.pallas_call(
        _outproj_kernel,
        grid=(n_rows // tm, d // tn),
        in_specs=[
            pl.BlockSpec((tm, ka), lambda i, j: (i, 0)),
            pl.BlockSpec((tm, kf), lambda i, j: (i, 0)),
            pl.BlockSpec((None, ka + kf, tn), lambda i, j: (layer, 0, j)),
            pl.BlockSpec((tm, tn), lambda i, j: (i, j)),
            pl.BlockSpec((1, tn), lambda i, j: (0, j)),
        ],
        out_specs=pl.BlockSpec((tm, tn), lambda i, j: (i, j)),
        out_shape=jax.ShapeDtypeStruct((n_rows, d), F32),
        compiler_params=_params(("parallel", "parallel"), vmem),
        cost_estimate=pl.CostEstimate(
            flops=2 * n_rows * (ka + kf) * d, transcendentals=0,
            bytes_accessed=n_rows * (ka + kf) * 2 + (n_rows // tm) * (ka + kf) * d * 2 + 2 * n_rows * d * 4),
        name="outproj",
    )(na, nf, w_all, x, gate.reshape(1, d))


def _mlp_kernel(x_ref, gss_ref, w1_hbm, w2_hbm, *rest, final, n_cast):
    raw_refs = rest[:n_cast]
    o_ref = rest[n_cast]
    cast_refs = rest[n_cast + 1:2 * n_cast + 1]
    h_ref, ops_ref, w1_buf, w2_buf, w_sem = rest[2 * n_cast + 1:]
    i, f = pl.program_id(0), pl.program_id(1)
    n_i, n_f = pl.num_programs(0), pl.num_programs(1)
    tf = w1_buf.shape[2]
    n_slabs = n_f * MLP_SLOTS

    def slab_copies(slab, slot):
        col = pl.multiple_of(slab * tf, tf)
        return (
            pltpu.make_async_copy(w1_hbm.at[:, pl.ds(col, tf)], w1_buf.at[slot], w_sem.at[0, slot]),
            pltpu.make_async_copy(w2_hbm.at[pl.ds(col, tf), :], w2_buf.at[slot], w_sem.at[1, slot]),
        )

    @pl.when((i == 0) & (f == 0))
    def _():
        for slot in range(MLP_SLOTS):
            for cp in slab_copies(slot, slot):
                cp.start()

    @pl.when(f == 0)
    def _():
        _normmod_rows(x_ref, gss_ref, ops_ref, h_ref, copy_ref=o_ref)

    _cast_slabs(raw_refs, cast_refs)
    tn = MLP_OUT_CHUNK
    for slot in range(MLP_SLOTS):
        for cp in slab_copies(0, slot):
            cp.wait()
        u = jnp.maximum(_dot(h_ref[...], w1_buf[slot]), 0.0)
        u = (u * u).astype(BF16)
        for c in range(o_ref.shape[1] // tn):
            cs = slice(c * tn, (c + 1) * tn)
            o_ref[:, cs] += gss_ref[3:4, cs] * _dot(u, w2_buf[slot, :, cs])
        for cp in slab_copies((f * MLP_SLOTS + slot + MLP_SLOTS) % n_slabs, slot):
            cp.start()

    @pl.when((i == n_i - 1) & (f == n_f - 1))
    def _():
        for slot in range(MLP_SLOTS):
            for cp in slab_copies(0, slot):
                cp.wait()

    if final:
        @pl.when(f == n_f - 1)
        def _():
            _rms_rows(o_ref, gss_ref.at[4:5, :], o_ref)


def _mlp(x, gss, w1_all, w2_all, layer, tm, tf, final, cast_next=()):
    n_rows, d = x.shape
    d_ff = w1_all.shape[2]
    n_f = d_ff // (tf * MLP_SLOTS)
    c_in, c_out, c_shape, c_args, c_vmem = _cast_specs(
        cast_next, (n_rows // tm) * n_f, lambda i, f: i * n_f + f)
    vmem = (2 * 2 * tm * d * 4 + MLP_SLOTS * 2 * d * tf * 2 + tm * d * 2 + tm * tf * 6
            + 2 * tm * MLP_OUT_CHUNK * 4 + c_vmem)
    outs = pl.pallas_call(
        functools.partial(_mlp_kernel, final=final, n_cast=len(cast_next)),
        grid=(n_rows // tm, n_f),
        in_specs=[
            pl.BlockSpec((tm, d), lambda i, f: (i, 0)),
            pl.BlockSpec((8, d), lambda i, f: (0, 0)),
            pl.BlockSpec(memory_space=pl.ANY),
            pl.BlockSpec(memory_space=pl.ANY),
            *c_in,
        ],
        out_specs=[pl.BlockSpec((tm, d), lambda i, f: (i, 0)), *c_out],
        out_shape=[jax.ShapeDtypeStruct((n_rows, d), F32), *c_shape],
        scratch_shapes=[
            pltpu.VMEM((tm, d), BF16), pltpu.VMEM((8, d), F32),
            pltpu.VMEM((MLP_SLOTS, d, tf), BF16), pltpu.VMEM((MLP_SLOTS, tf, d), BF16),
            pltpu.SemaphoreType.DMA((2, MLP_SLOTS)),
        ],
        compiler_params=_params(("arbitrary", "arbitrary"), vmem),
        cost_estimate=pl.CostEstimate(
            flops=4 * n_rows * d * d_ff, transcendentals=n_rows,
            bytes_accessed=(2 * n_rows * d * 4 + (n_rows // tm) * 2 * d * d_ff * 2
                            + sum(6 * s.shape[1] * s.shape[2] for s in c_shape))),
        name="mlp_final" if final else "mlp",
    )(x, gss, w1_all[layer], w2_all[layer], *c_args)
    return outs[0], tuple(outs[1:])


def _rope_tables(n_rows):
    quarter = HEAD_DIM // 4
    inv = ROPE_THETA ** (-jnp.arange(quarter, dtype=F32) / quarter)
    pos = jnp.arange(n_rows)
    rows = (pos // GRID_W).astype(F32)
    cols = (pos % GRID_W).astype(F32)
    ang = jnp.concatenate([rows[:, None] * inv[None, :]] * 2 + [cols[:, None] * inv[None, :]] * 2, axis=1)
    sign = jnp.tile(jnp.concatenate([-jnp.ones(quarter, F32), jnp.ones(quarter, F32)]), 2)
    return jnp.cos(ang), jnp.sin(ang) * sign[None, :]


def _rows8(*vecs):
    d = vecs[0].shape[0]
    pad = [jnp.zeros((d,), F32)] * (8 - len(vecs))
    return jnp.stack(list(vecs) + pad, axis=0)


def _row_tile(n_rows, want):
    return min(want, n_rows)


def kernel(x, c, ctx, c_ctx, ada_a, ada_b, ada_bias, g_mix, w_in, sink, w_f, g_attn_out, g_four_out,
           w_out, g_mlp, w1, w2, g_final):
    assert x.shape[0] == 1 and ctx.shape[0] == 1
    depth = w_in.shape[0]
    d = x.shape[2]
    xs = x[0]
    cs = ctx[0]
    n_lat, n_ctx = xs.shape[0], cs.shape[0]

    cv = jnp.zeros((ADA_ROWS, d), F32).at[0].set(c[0]).at[1].set(c_ctx)
    mods = _ada(cv, ada_a, ada_b, ada_bias)
    rope = _rope_tables(n_lat)
    fconst = _FourierConsts(n_lat, n_ctx)
    wf_b = w_f.astype(BF16)
    w_in_b = w_in[0:1].astype(BF16)
    w_out_b = w1_b = w2_b = None

    for l in range(depth):
        last = l == depth - 1
        m_lat = [mods[l, 0, i * d:(i + 1) * d] for i in range(N_MOD)]
        m_ctx = [mods[l, 1, i * d:(i + 1) * d] for i in range(N_MOD)]

        hc = _normmod(cs, _rows8(g_mix[l], m_ctx[0], m_ctx[1]), _row_tile(n_ctx, 256))
        hx = _normmod(xs, _rows8(g_mix[l], m_lat[0], m_lat[1]), _row_tile(n_lat, 1024))
        pc, pcf, _ = _inproj(hc, w_in_b, 0, None, _row_tile(n_ctx, 1024), 1024)
        px, pxf, cast1 = _inproj(hx, w_in_b, 0, rope, _row_tile(n_lat, 1024), 1024,
                                 cast=((w1, 0),) if l == 0 else ())

        na, cast2 = _latent_attn(px, pc, sink[l], g_attn_out[l],
                                 cast=((w2, 0), (w_out, 0)) if l == 0 else ())
        if l == 0:
            (w1_b,), (w2_b, w_out_b) = cast1, cast2
        nf = _latent_fourier(pxf, fconst, wf_b, l, g_four_out[l], run_after=pc)
        x_mid = _outproj(na, nf, w_out_b, 0, xs, m_lat[2], _row_tile(n_lat, 1024), 1024)
        gss = _rows8(g_mlp[l], m_lat[3], m_lat[4], m_lat[5], g_final)
        cast_next = () if last else tuple((w, l + 1) for w in (w_in, w_out, w1, w2))
        xs, next_weights = _mlp(x_mid, gss, w1_b, w2_b, 0, _row_tile(n_lat, 512), 512, final=last,
                                cast_next=cast_next)

        if not last:
            nac = _ctx_attn(pc, sink[l], g_attn_out[l])
            nfc = _ctx_fourier(pcf, fconst, wf_b, l, g_four_out[l])
            c_mid = _outproj(nac, nfc, w_out_b, 0, cs, m_ctx[2], _row_tile(n_ctx, 1024), 1024)
            gss_c = _rows8(g_mlp[l], m_ctx[3], m_ctx[4], m_ctx[5])
            cs, _ = _mlp(c_mid, gss_c, w1_b, w2_b, 0, _row_tile(n_ctx, 512), 512, final=False)
            w_in_b, w_out_b, w1_b, w2_b = next_weights

    return xs[None]
```

```python
import functools
import math

import numpy as np
import jax
import jax.numpy as jnp
from jax import lax
from jax.experimental import pallas as pl
from jax.experimental.pallas import tpu as pltpu

F32 = jnp.float32
BF16 = jnp.bfloat16

N_HEADS = 16
N_KV_HEADS = 4
HEAD_DIM = 128
GROUP = N_HEADS // N_KV_HEADS
Q_WIDTH = N_HEADS * HEAD_DIM
KV_WIDTH = N_KV_HEADS * HEAD_DIM
BLOCK = 128
GRID_W = 64
N_FGROUPS = 4
F_DIM = 512
F_WIDTH = N_FGROUPS * F_DIM
N_MOD = 6
ROPE_THETA = 10000.0
EPS = 1e-6
NEG_INF = -1e30
LOG2E = 1.4426950408889634

V7X_VMEM_BYTES = 64 * 1024 * 1024
ADA_ROWS = 8
MLP_OUT_CHUNK = 512
MLP_SLOTS = 2
MLP_STEP_SLABS = 4
ROW_CHUNK = 16
ROW_UNROLL = 2
STREAM_UNROLL = 8
DOT_COLS = 256
SOFTMAX_ROWS = 32
ATTN_Q_BLOCKS = 4
HEAD_RESIDUES = 16
TAIL_BLOCKS = 4


def _params(semantics, vmem_bytes):
    limit = min(int(vmem_bytes * 1.15) + (4 << 20), V7X_VMEM_BYTES - (2 << 20))
    return pltpu.CompilerParams(dimension_semantics=semantics, vmem_limit_bytes=limit)


def _dot(a, b):
    return jnp.dot(a, b, preferred_element_type=F32)


def _dot_nt(a, b):
    return lax.dot_general(a, b, (((1,), (1,)), ((), ())), preferred_element_type=F32)


def _rms(x, g):
    return x * lax.rsqrt(jnp.mean(x * x, axis=-1, keepdims=True) + EPS) * g


def _for_row_chunks(n_rows, body, unroll=ROW_UNROLL):
    rc = min(ROW_CHUNK, n_rows)

    def step(r, carry):
        body(pl.ds(pl.multiple_of(r * rc, rc), rc))
        return carry

    lax.fori_loop(0, n_rows // rc, step, 0, unroll=min(unroll, n_rows // rc))


def _normmod_rows(x_ref, gss_ref, ops_ref, h_ref, copy_ref=None):
    ops_ref[0:1, :] = gss_ref[0:1, :] * (1.0 + gss_ref[2:3, :])

    def body(rows):
        x = x_ref[rows, :]
        y = x * lax.rsqrt(jnp.mean(x * x, axis=-1, keepdims=True) + EPS)
        h_ref[rows, :] = (y * ops_ref[0:1, :] + gss_ref[1:2, :]).astype(h_ref.dtype)
        if copy_ref is not None:
            copy_ref[rows, :] = x

    _for_row_chunks(x_ref.shape[0], body)


def _rms_rows(src_ref, g_ref, dst_ref):
    def body(rows):
        dst_ref[rows, :] = _rms(src_ref[rows, :], g_ref[...]).astype(dst_ref.dtype)

    _for_row_chunks(src_ref.shape[0], body)


def _rms_rows_from_sumsq(src_ref, ss_ref, g_ref, dst_ref):
    width = src_ref.shape[1]

    def body(rows):
        inv = lax.rsqrt(ss_ref[rows, :] * (1.0 / width) + EPS)
        dst_ref[rows, :] = (src_ref[rows, :] * inv * g_ref[...]).astype(dst_ref.dtype)

    _for_row_chunks(src_ref.shape[0], body, unroll=STREAM_UNROLL)


def _sumsq(x):
    return jnp.sum(x * x, axis=-1, keepdims=True)


def _cast_slab(shape, n_steps):
    n_r, n_c = shape
    for q in (1, 2, 4, 8, 16):
        if n_steps % q or n_c % (q * 128) or n_r % (n_steps // q):
            continue
        br = n_r // (n_steps // q)
        if br % 16 == 0:
            return br, n_c // q, q
    raise ValueError(f"no bf16-tile-aligned slab split of {shape} into {n_steps} steps")


def _cast_specs(jobs, n_steps, step_of):
    ins, outs, shapes, args, vmem = [], [], [], [], 0
    for w_raw, src in jobs:
        br, bc, q = _cast_slab(w_raw.shape[1:], n_steps)
        ins.append(pl.BlockSpec(
            (None, br, bc), lambda *g, src=src, q=q: (src, step_of(*g) // q, step_of(*g) % q)))
        outs.append(pl.BlockSpec((None, br, bc), lambda *g, q=q: (0, step_of(*g) // q, step_of(*g) % q)))
        shapes.append(jax.ShapeDtypeStruct((1,) + w_raw.shape[1:], BF16))
        args.append(w_raw)
        vmem += 2 * br * bc * 6
    return ins, outs, shapes, args, vmem


def _cast_slabs(raw_refs, cast_refs):
    for raw_ref, cast_ref in zip(raw_refs, cast_refs):
        cast_ref[...] = raw_ref[...].astype(cast_ref.dtype)


def _ada_kernel(cv_ref, a_ref, b_ref, bias_ref, o_ref, h_ref):
    @pl.when(pl.program_id(1) == 0)
    def _():
        cv = cv_ref[...]
        s = cv * jax.nn.sigmoid(cv)
        h_ref[...] = _dot(s.astype(BF16), a_ref[0].astype(BF16))

    o_ref[0] = _dot(h_ref[...].astype(BF16), b_ref[0].astype(BF16)) + bias_ref[0]


def _ada(cv, ada_a, ada_b, ada_bias):
    depth, d, rank = ada_a.shape
    n = ada_b.shape[2]
    tn = d
    vmem = 2 * (d * rank * 4 + rank * tn * 4) + 4 * ADA_ROWS * (d + tn) * 4
    return pl.pallas_call(
        _ada_kernel,
        grid=(depth, n // tn),
        in_specs=[
            pl.BlockSpec((ADA_ROWS, d), lambda l, j: (0, 0)),
            pl.BlockSpec((1, d, rank), lambda l, j: (l, 0, 0)),
            pl.BlockSpec((1, rank, tn), lambda l, j: (l, 0, j)),
            pl.BlockSpec((1, 1, tn), lambda l, j: (l, 0, j)),
        ],
        out_specs=pl.BlockSpec((1, ADA_ROWS, tn), lambda l, j: (l, 0, j)),
        out_shape=jax.ShapeDtypeStruct((depth, ADA_ROWS, n), F32),
        scratch_shapes=[pltpu.VMEM((ADA_ROWS, rank), F32)],
        compiler_params=_params(("parallel", "arbitrary"), vmem),
        name="ada",
    )(cv, ada_a, ada_b, ada_bias.reshape(depth, 1, n))


def _rope_chunk(a, cos, sin, lane):
    up = pltpu.roll(a, HEAD_DIM - 32, axis=1)
    down = pltpu.roll(a, 32, axis=1)
    partner = jnp.where((lane % 64) < 32, up, down)
    return a * cos + partner * sin


def _normmod_kernel(x_ref, gss_ref, h_ref, ops_ref):
    _normmod_rows(x_ref, gss_ref, ops_ref, h_ref)


def _normmod(x, gss, tm):
    n_rows, d = x.shape
    return pl.pallas_call(
        _normmod_kernel,
        grid=(n_rows // tm,),
        in_specs=[pl.BlockSpec((tm, d), lambda i: (i, 0)), pl.BlockSpec((8, d), lambda i: (0, 0))],
        out_specs=pl.BlockSpec((tm, d), lambda i: (i, 0)),
        out_shape=jax.ShapeDtypeStruct((n_rows, d), BF16),
        scratch_shapes=[pltpu.VMEM((8, d), F32)],
        compiler_params=_params(("parallel",), 2 * tm * d * 6 + 16 * d * 4),
        cost_estimate=pl.CostEstimate(flops=8 * n_rows * d, transcendentals=n_rows, bytes_accessed=n_rows * d * 6),
        name="normmod",
    )(x, gss)


def _inproj_kernel(h_ref, w_ref, *rest, rope_cols, tn, n_cast):
    if rope_cols:
        cos_ref, sin_ref = rest[:2]
        rest = rest[2:]
    raw_refs = rest[:n_cast]
    qkv_ref, f_ref = rest[n_cast:n_cast + 2]
    cast_refs = rest[n_cast + 2:]
    j = pl.program_id(1)
    heads_per_dot = DOT_COLS // HEAD_DIM

    def tile(n_rope, o_ref):
        _cast_slabs(raw_refs, cast_refs)
        if n_rope:
            lane = lax.broadcasted_iota(jnp.int32, (h_ref.shape[0], HEAD_DIM), 1)
        for c in range(tn // DOT_COLS):
            acc = _dot(h_ref[...], w_ref[:, c * DOT_COLS:(c + 1) * DOT_COLS])
            for k in range(heads_per_dot):
                head = c * heads_per_dot + k
                a = acc[:, k * HEAD_DIM:(k + 1) * HEAD_DIM]
                if head < n_rope:
                    a = _rope_chunk(a, cos_ref[...], sin_ref[...], lane)
                o_ref[:, head * HEAD_DIM:(head + 1) * HEAD_DIM] = a.astype(o_ref.dtype)

    qkv_tiles = (Q_WIDTH + 2 * KV_WIDTH) // tn
    full_tiles, part = divmod(rope_cols, tn)
    if full_tiles:
        pl.when(j < full_tiles)(functools.partial(tile, tn // HEAD_DIM, qkv_ref))
    if part:
        pl.when(j == full_tiles)(functools.partial(tile, part // HEAD_DIM, qkv_ref))
        full_tiles += 1
    if full_tiles < qkv_tiles:
        pl.when((j >= full_tiles) & (j < qkv_tiles))(functools.partial(tile, 0, qkv_ref))
    pl.when(j >= qkv_tiles)(functools.partial(tile, 0, f_ref))


def _inproj(h, w_all, layer, rope_tables, tm, tn, cast=()):
    n_rows, d = h.shape
    n_out = w_all.shape[2]
    qkv_cols = Q_WIDTH + 2 * KV_WIDTH
    qkv_tiles = qkv_cols // tn
    n_j = n_out // tn
    rope_cols = Q_WIDTH + KV_WIDTH if rope_tables is not None else 0
    in_specs = [
        pl.BlockSpec((tm, d), lambda i, j: (i, 0)),
        pl.BlockSpec((None, d, tn), lambda i, j: (layer, 0, j)),
    ]
    args = [h, w_all]
    if rope_tables is not None:
        in_specs += [pl.BlockSpec((tm, HEAD_DIM), lambda i, j: (i, 0))] * 2
        args += list(rope_tables)
    slabs = 1 << (n_j.bit_length() - 1)
    c_in, c_out, c_shape, c_args, c_vmem = _cast_specs(
        cast, (n_rows // tm) * slabs, lambda i, j: i * slabs + jnp.minimum(j, slabs - 1))
    vmem = (2 * (tm * d * 2 + d * tn * 2 + 2 * tm * tn * 2 + 2 * tm * HEAD_DIM * 4) + 4 * tm * DOT_COLS * 4
            + c_vmem)
    cost = pl.CostEstimate(
        flops=2 * n_rows * d * n_out, transcendentals=0,
        bytes_accessed=(n_rows * d * 2 + (n_rows // tm) * d * n_out * 2 + n_rows * n_out * 2
                        + sum(6 * s.shape[1] * s.shape[2] for s in c_shape)))
    outs = pl.pallas_call(
        functools.partial(_inproj_kernel, rope_cols=rope_cols, tn=tn, n_cast=len(cast)),
        grid=(n_rows // tm, n_j),
        in_specs=in_specs + c_in,
        out_specs=[
            pl.BlockSpec((tm, tn), lambda i, j: (i, jnp.minimum(j, qkv_tiles - 1))),
            pl.BlockSpec((tm, tn), lambda i, j: (i, jnp.maximum(j - qkv_tiles, 0))),
            *c_out,
        ],
        out_shape=[
            jax.ShapeDtypeStruct((n_rows, qkv_cols), BF16),
            jax.ShapeDtypeStruct((n_rows, n_out - qkv_cols), BF16),
            *c_shape,
        ],
        compiler_params=_params(("parallel", "arbitrary"), vmem),
        cost_estimate=cost,
        name="inproj_rope" if rope_cols else "inproj_ctx",
    )(*args, *c_args)
    return outs[0], outs[1], tuple(outs[2:])


def _stack_heads(q_ref, rows, h):
    return jnp.concatenate(
        [q_ref[rows, (h * GROUP + g) * HEAD_DIM:(h * GROUP + g + 1) * HEAD_DIM] for g in range(GROUP)], axis=0)


def _softmax_chunk(s_ref, p_ref, rows, sink, masks):
    n_tiles = s_ref.shape[1] // BLOCK
    tiles = []
    for t in range(n_tiles):
        v = s_ref[rows, t * BLOCK:(t + 1) * BLOCK]
        if masks.get(t) is not None:
            v = jnp.where(masks[t], v, NEG_INF)
        tiles.append(v)
    m_raw = jnp.max(functools.reduce(jnp.maximum, tiles), axis=-1, keepdims=True)
    m = jnp.maximum(m_raw * (HEAD_DIM ** -0.5), sink)
    mb = m * LOG2E
    es = [jnp.exp2(v * (HEAD_DIM ** -0.5 * LOG2E) - mb) for v in tiles]
    denom = jnp.sum(functools.reduce(jnp.add, es), axis=-1, keepdims=True) + jnp.exp2(sink * LOG2E - mb)
    inv = 1.0 / denom
    for t in range(n_tiles):
        p_ref[rows, t * BLOCK:(t + 1) * BLOCK] = (es[t] * inv).astype(p_ref.dtype)


def _attn_units(sink_ref, q_ref, units, acc_ref, ss_ref, s_ref, p_ref):
    def scores(u):
        q_rows, h, keys, _, _ = units[u]
        s_ref[u % 2] = _dot_nt(_stack_heads(q_ref, q_rows, h), keys())

    scores(0)
    for u, (q_rows, h, _, values, mask_fn) in enumerate(units):
        if u + 1 < len(units):
            scores(u + 1)
        n_q = q_rows.stop - q_rows.start
        chunks_per_head = n_q // SOFTMAX_ROWS
        for r in range(GROUP * chunks_per_head):
            rows = slice(r * SOFTMAX_ROWS, (r + 1) * SOFTMAX_ROWS)
            sink = sink_ref[h * GROUP + r // chunks_per_head]
            q0 = (r % chunks_per_head) * SOFTMAX_ROWS
            _softmax_chunk(s_ref.at[u % 2], p_ref.at[u % 2], rows, sink, mask_fn(q0))
        o = _dot(p_ref[u % 2], values())
        ss = None
        for g in range(GROUP):
            c0 = (h * GROUP + g) * HEAD_DIM
            og = o[g * n_q:(g + 1) * n_q, :]
            acc_ref[q_rows, c0:c0 + HEAD_DIM] = og
            ss = _sumsq(og) if ss is None else ss + _sumsq(og)
        ss_ref[q_rows, :] = ss if h == 0 else ss_ref[q_rows, :] + ss


def _latent_attn_kernel(sink_ref, q_ref, kp_ref, kc_ref, kn_ref, vp_ref, vc_ref, vn_ref,
                        kx_ref, vx_ref, g_ref, *rest, n_ctx, q_blocks, n_cast):
    raw_refs = rest[:n_cast]
    o_ref = rest[n_cast]
    cast_refs = rest[n_cast + 1:2 * n_cast + 1]
    acc_ref, ss_ref, s_ref, p_ref = rest[2 * n_cast + 1:]
    _cast_slabs(raw_refs, cast_refs)
    n = pl.program_id(0)
    nb = pl.num_programs(0)
    ctx_tiles = n_ctx // BLOCK
    prev_lo = jnp.where(n == 0, BLOCK, 0)
    next_hi = jnp.where(n == nb - 1, 0, BLOCK)

    def block_of(refs, i):
        prev_ref, cur_ref, next_ref = refs
        if i == 0:
            return prev_ref, slice(0, BLOCK)
        if i == q_blocks + 1:
            return next_ref, slice(0, BLOCK)
        return cur_ref, slice((i - 1) * BLOCK, i * BLOCK)

    def operand(ctx_ref, refs, qb, h):
        hs = slice(h * HEAD_DIM, (h + 1) * HEAD_DIM)
        parts = [ctx_ref[:, hs]]
        for i in range(qb, qb + 3):
            ref, rows = block_of(refs, i)
            parts.append(ref[rows, hs])
        return jnp.concatenate(parts, axis=0)

    def mask_fn(qb, q0):
        qi = lax.broadcasted_iota(jnp.int32, (SOFTMAX_ROWS, BLOCK), 0) + q0
        kj = lax.broadcasted_iota(jnp.int32, (SOFTMAX_ROWS, BLOCK), 1)
        lo = kj >= qi
        hi = kj <= qi
        if qb == 0:
            lo = lo & (kj >= prev_lo)
        if qb == q_blocks - 1:
            hi = hi & (kj < next_hi)
        return {ctx_tiles: lo, ctx_tiles + 2: hi}

    units = []
    for qb in range(q_blocks):
        for h in range(N_KV_HEADS):
            units.append((
                slice(qb * BLOCK, (qb + 1) * BLOCK), h,
                functools.partial(operand, kx_ref, (kp_ref, kc_ref, kn_ref), qb, h),
                functools.partial(operand, vx_ref, (vp_ref, vc_ref, vn_ref), qb, h),
                functools.partial(mask_fn, qb),
            ))
    _attn_units(sink_ref, q_ref, units, acc_ref, ss_ref, s_ref, p_ref)
    _rms_rows_from_sumsq(acc_ref, ss_ref, g_ref, o_ref)


def _latent_attn(px, pc, sink, g_attn, cast=()):
    n_rows = px.shape[0]
    n_ctx = pc.shape[0]
    nb = n_rows // BLOCK
    qb = ATTN_Q_BLOCKS if nb % ATTN_Q_BLOCKS == 0 else 1
    kcol = Q_WIDTH // KV_WIDTH
    vcol = kcol + 1
    n_keys = n_ctx + 3 * BLOCK

    def kv_specs(colblk):
        return [
            pl.BlockSpec((BLOCK, KV_WIDTH), lambda n: (jnp.maximum(n * qb - 1, 0), colblk)),
            pl.BlockSpec((qb * BLOCK, KV_WIDTH), lambda n: (n, colblk)),
            pl.BlockSpec((BLOCK, KV_WIDTH), lambda n: (jnp.minimum(n * qb + qb, nb - 1), colblk)),
        ]

    in_specs = [
        pl.BlockSpec(memory_space=pltpu.SMEM),
        pl.BlockSpec((qb * BLOCK, Q_WIDTH), lambda n: (n, 0)),
        *kv_specs(kcol), *kv_specs(vcol),
        pl.BlockSpec((n_ctx, KV_WIDTH), lambda n: (0, kcol)),
        pl.BlockSpec((n_ctx, KV_WIDTH), lambda n: (0, vcol)),
        pl.BlockSpec((1, Q_WIDTH), lambda n: (0, 0)),
    ]
    c_in, c_out, c_shape, c_args, c_vmem = _cast_specs(cast, nb // qb, lambda n: n)
    outs = pl.pallas_call(
        functools.partial(_latent_attn_kernel, n_ctx=n_ctx, q_blocks=qb, n_cast=len(cast)),
        grid=(nb // qb,),
        in_specs=in_specs + c_in,
        out_specs=[pl.BlockSpec((qb * BLOCK, Q_WIDTH), lambda n: (n, 0)), *c_out],
        out_shape=[jax.ShapeDtypeStruct((n_rows, Q_WIDTH), BF16), *c_shape],
        scratch_shapes=[
            pltpu.VMEM((qb * BLOCK, Q_WIDTH), F32),
            pltpu.VMEM((qb * BLOCK, 1), F32),
            pltpu.VMEM((2, GROUP * BLOCK, n_keys), F32),
            pltpu.VMEM((2, GROUP * BLOCK, n_keys), BF16),
        ],
        compiler_params=_params(("parallel",), (24 << 20) + c_vmem),
        cost_estimate=pl.CostEstimate(
            flops=4 * n_rows * N_HEADS * n_keys * HEAD_DIM, transcendentals=n_rows * N_HEADS * n_keys,
            bytes_accessed=(n_rows * (2 * Q_WIDTH + 6 * KV_WIDTH) * 2
                            + sum(6 * s.shape[1] * s.shape[2] for s in c_shape))),
        name="latent_attn",
    )(sink, px, px, px, px, px, px, px, pc, pc, g_attn.reshape(1, Q_WIDTH), *c_args)
    return outs[0], tuple(outs[1:])


def _ctx_attn_kernel(sink_ref, q_ref, k_ref, v_ref, g_ref, o_ref, acc_ref, ss_ref, s_ref, p_ref):
    def operand(ref, h):
        return ref[:, h * HEAD_DIM:(h + 1) * HEAD_DIM]

    units = [(slice(0, q_ref.shape[0]), h, functools.partial(operand, k_ref, h),
              functools.partial(operand, v_ref, h), lambda q0: {}) for h in range(N_KV_HEADS)]
    _attn_units(sink_ref, q_ref, units, acc_ref, ss_ref, s_ref, p_ref)
    _rms_rows_from_sumsq(acc_ref, ss_ref, g_ref, o_ref)


def _ctx_attn(pc, sink, g_attn):
    n_ctx = pc.shape[0]
    kcol = Q_WIDTH // KV_WIDTH
    return pl.pallas_call(
        _ctx_attn_kernel,
        grid=(1,),
        in_specs=[
            pl.BlockSpec(memory_space=pltpu.SMEM),
            pl.BlockSpec((n_ctx, Q_WIDTH), lambda i: (0, 0)),
            pl.BlockSpec((n_ctx, KV_WIDTH), lambda i: (0, kcol)),
            pl.BlockSpec((n_ctx, KV_WIDTH), lambda i: (0, kcol + 1)),
            pl.BlockSpec((1, Q_WIDTH), lambda i: (0, 0)),
        ],
        out_specs=pl.BlockSpec((n_ctx, Q_WIDTH), lambda i: (0, 0)),
        out_shape=jax.ShapeDtypeStruct((n_ctx, Q_WIDTH), BF16),
        scratch_shapes=[
            pltpu.VMEM((n_ctx, Q_WIDTH), F32),
            pltpu.VMEM((n_ctx, 1), F32),
            pltpu.VMEM((2, GROUP * n_ctx, n_ctx), F32),
            pltpu.VMEM((2, GROUP * n_ctx, n_ctx), BF16),
        ],
        compiler_params=_params(("arbitrary",), 16 << 20),
        name="ctx_attn",
    )(sink, pc, pc, pc, g_attn.reshape(1, Q_WIDTH))


def _dft_tables(n):
    ang = 2.0 * np.pi * (np.outer(np.arange(n), np.arange(n)) % n) / n
    return np.cos(ang), np.sin(ang)


def _fourier_head_kernel(g_ref, z_ref, o_ref):
    for r in range(z_ref.shape[0]):
        o_ref[r] = _dot(g_ref[r], z_ref[r]).astype(o_ref.dtype)


def _fourier_head(zt, gmat):
    nb, na, c = zt.shape
    rb = min(HEAD_RESIDUES, nb)
    return pl.pallas_call(
        _fourier_head_kernel,
        grid=(nb // rb,),
        in_specs=[
            pl.BlockSpec((rb, 2 * na, na), lambda b: (b, 0, 0)),
            pl.BlockSpec((rb, na, c), lambda b: (b, 0, 0)),
        ],
        out_specs=pl.BlockSpec((rb, 2 * na, c), lambda b: (b, 0, 0)),
        out_shape=jax.ShapeDtypeStruct((nb, 2 * na, c), BF16),
        compiler_params=_params(("parallel",), 6 * rb * na * c * 2),
        cost_estimate=pl.CostEstimate(
            flops=4 * nb * na * na * c, transcendentals=0,
            bytes_accessed=(3 * nb * na * c + 2 * nb * na * na) * 2),
        name="fourier_head",
    )(gmat, zt)


def _fourier_tail_kernel(m_ref, d_ref, cs_ref, wf_ref, g_ref, *rest, pos_scale):
    o_ref, acc_ref, ss_ref = rest[-3:]
    kb, p = o_ref.shape[0], o_ref.shape[1]
    xs = [(_dot(m_ref[...], d_ref[blk]) * pos_scale).astype(BF16) for blk in range(kb)]
    lhs = jnp.concatenate(
        [jnp.concatenate([x[:p, g * F_DIM:(g + 1) * F_DIM], x[p:, g * F_DIM:(g + 1) * F_DIM]], axis=1)
         for g in range(N_FGROUPS) for x in xs], axis=0)
    f = (_dot(lhs, cs_ref[...]) * (F_DIM ** -0.5)).astype(BF16)
    for g in range(N_FGROUPS):
        og = _dot(f[g * kb * p:(g + 1) * kb * p, :], wf_ref[g])
        for blk in range(kb):
            piece = og[blk * p:(blk + 1) * p, :]
            acc_ref[blk, :, g * F_DIM:(g + 1) * F_DIM] = piece
            ss_ref[blk] = _sumsq(piece) if g == 0 else ss_ref[blk] + _sumsq(piece)
    for blk in range(kb):
        _rms_rows_from_sumsq(acc_ref.at[blk], ss_ref.at[blk], g_ref, o_ref.at[blk])


def _fourier_tail(stage_mat, data, cs, wf_all, layer, g_four, pos_scale, run_after=None):
    nblk, k_in, c = data.shape
    p = stage_mat.shape[0] // 2
    kb = min(TAIL_BLOCKS, nblk)
    temps = kb * p * c * (4 + 2 + 2 + 4 + 2)
    vmem = 2 * kb * (k_in + p) * c * 2 + kb * p * c * 4 + 2 * (2 * p * k_in + 6 * F_DIM * F_DIM) * 2 + temps
    in_specs = [
        pl.BlockSpec((2 * p, k_in), lambda i: (0, 0)),
        pl.BlockSpec((kb, k_in, c), lambda i: (i, 0, 0)),
        pl.BlockSpec((2 * F_DIM, F_DIM), lambda i: (0, 0)),
        pl.BlockSpec((None, N_FGROUPS, F_DIM, F_DIM), lambda i: (layer, 0, 0, 0)),
        pl.BlockSpec((1, c), lambda i: (0, 0)),
    ]
    args = [stage_mat, data, cs, wf_all, g_four.reshape(1, c)]
    if run_after is not None:
        in_specs.append(pl.BlockSpec((16, 128), lambda i: (0, 0)))
        args.append(run_after)
    return pl.pallas_call(
        functools.partial(_fourier_tail_kernel, pos_scale=pos_scale),
        grid=(nblk // kb,),
        in_specs=in_specs,
        out_specs=pl.BlockSpec((kb, p, c), lambda i: (i, 0, 0)),
        out_shape=jax.ShapeDtypeStruct((nblk, p, c), BF16),
        scratch_shapes=[pltpu.VMEM((kb, p, c), F32), pltpu.VMEM((kb, p, 1), F32)],
        compiler_params=_params(("parallel",), vmem),
        cost_estimate=pl.CostEstimate(
            flops=2 * nblk * (2 * p * k_in * c + p * c * 2 * F_DIM + p * c * F_DIM), transcendentals=nblk * p,
            bytes_accessed=nblk * (k_in + p) * c * 2 + 2 * p * k_in * 2 + 6 * F_DIM * F_DIM * 2),
        name="fourier_tail",
    )(*args)


class _FourierConsts:
    def __init__(self, n_lat, n_ctx):
        a = b = int(round(math.sqrt(n_lat)))
        assert a * b == n_lat
        self.a, self.b = a, b
        k_lo = np.arange(a)[None, :, None]
        n = (np.arange(a)[None, None, :] * b + np.arange(b)[:, None, None])
        ang = 2.0 * np.pi * ((k_lo * n) % n_lat) / n_lat
        self.head = jnp.asarray(np.concatenate([np.cos(ang), -np.sin(ang)], axis=1), BF16)
        cb, sb = _dft_tables(b)
        self.tail = jnp.asarray(np.block([[cb, sb], [-sb, cb]]), BF16)
        cc, sc = _dft_tables(n_ctx)
        self.ctx = jnp.asarray(np.concatenate([cc, -sc], axis=0), BF16)
        cf, sf = _dft_tables(F_DIM)
        self.chan = jnp.asarray(np.concatenate([cf, sf], axis=0), BF16)


def _latent_fourier(pf, fc, wf_all, layer, g_four, run_after):
    n_rows = pf.shape[0]
    a, b = fc.a, fc.b
    zt = jnp.transpose(pf.reshape(a, b, F_WIDTH), (1, 0, 2))
    t = _fourier_head(zt, fc.head)
    t = jnp.transpose(t.reshape(b, 2, a, F_WIDTH), (2, 1, 0, 3)).reshape(a, 2 * b, F_WIDTH)
    o = _fourier_tail(fc.tail, t, fc.chan, wf_all, layer, g_four, 1.0 / math.sqrt(n_rows),
                      run_after=run_after)
    return jnp.transpose(o, (1, 0, 2)).reshape(n_rows, F_WIDTH)


def _ctx_fourier(pf, fc, wf_all, layer, g_four):
    n_ctx = pf.shape[0]
    d = pf.reshape(1, n_ctx, F_WIDTH)
    o = _fourier_tail(fc.ctx, d, fc.chan, wf_all, layer, g_four, 1.0 / math.sqrt(n_ctx))
    return o.reshape(n_ctx, F_WIDTH)


def _outproj_kernel(na_ref, nf_ref, w_ref, x_ref, gate_ref, o_ref):
    ka = na_ref.shape[1]
    acc = _dot(na_ref[...], w_ref[:ka, :]) + _dot(nf_ref[...], w_ref[ka:, :])
    o_ref[...] = x_ref[...] + gate_ref[...] * acc


def _outproj(na, nf, w_all, layer, x, gate, tm, tn):
    n_rows, d = x.shape
    ka, kf = na.shape[1], nf.shape[1]
    vmem = 2 * (tm * (ka + kf) * 2 + (ka + kf) * tn * 2 + 2 * tm * tn * 4) + tm * tn * 4
    return pl.pallas_call(
        _outproj_kernel,
        grid=(n_rows // tm, d // tn),
        in_specs=[
            pl.BlockSpec((tm, ka), lambda i, j: (i, 0)),
            pl.BlockSpec((tm, kf), lambda i, j: (i, 0)),
            pl.BlockSpec((None, ka + kf, tn), lambda i, j: (layer, 0, j)),
            pl.BlockSpec((tm, tn), lambda i, j: (i, j)),
            pl.BlockSpec((1, tn), lambda i, j: (0, j)),
        ],
        out_specs=pl.BlockSpec((tm, tn), lambda i, j: (i, j)),
        out_shape=jax.ShapeDtypeStruct((n_rows, d), F32),
        compiler_params=_params(("parallel", "parallel"), vmem),
        cost_estimate=pl.CostEstimate(
            flops=2 * n_rows * (ka + kf) * d, transcendentals=0,
            bytes_accessed=n_rows * (ka + kf) * 2 + (n_rows // tm) * (ka + kf) * d * 2 + 2 * n_rows * d * 4),
        name="outproj",
    )(na, nf, w_all, x, gate.reshape(1, d))


def _mlp_kernel(x_hbm, gss_ref, w1_hbm, w2_hbm, *rest, final, n_cast):
    raw_refs = rest[:n_cast]
    o_ref = rest[n_cast]
    cast_refs = rest[n_cast + 1:2 * n_cast + 1]
    h_ref, ops_ref, x_buf, w1_buf, w2_buf, x_sem, w_sem = rest[2 * n_cast + 1:]
    i, f = pl.program_id(0), pl.program_id(1)
    n_i, n_f = pl.num_programs(0), pl.num_programs(1)
    tm = x_buf.shape[0]
    tf = w1_buf.shape[2]
    n_slabs = n_f * MLP_STEP_SLABS

    def row_copy(block):
        row = pl.multiple_of(block * tm, tm)
        return pltpu.make_async_copy(x_hbm.at[pl.ds(row, tm), :], x_buf, x_sem.at[0])

    def slab_copies(slab, slot):
        col = pl.multiple_of(slab * tf, tf)
        return (
            pltpu.make_async_copy(w1_hbm.at[:, pl.ds(col, tf)], w1_buf.at[slot], w_sem.at[0, slot]),
            pltpu.make_async_copy(w2_hbm.at[pl.ds(col, tf), :], w2_buf.at[slot], w_sem.at[1, slot]),
        )

    @pl.when((i == 0) & (f == 0))
    def _():
        row_copy(0).start()
        for slot in range(MLP_SLOTS):
            for cp in slab_copies(slot, slot):
                cp.start()

    @pl.when(f == 0)
    def _():
        row_copy(0).wait()
        _normmod_rows(x_buf, gss_ref, ops_ref, h_ref, copy_ref=o_ref)
        row_copy(jnp.minimum(i + 1, n_i - 1)).start()

    _cast_slabs(raw_refs, cast_refs)
    tn = MLP_OUT_CHUNK
    for k in range(MLP_STEP_SLABS):
        slot = k % MLP_SLOTS
        for cp in slab_copies(0, slot):
            cp.wait()
        u = jnp.maximum(_dot(h_ref[...], w1_buf[slot]), 0.0)
        u = (u * u).astype(BF16)
        for c in range(o_ref.shape[1] // tn):
            cs = slice(c * tn, (c + 1) * tn)
            o_ref[:, cs] += gss_ref[3:4, cs] * _dot(u, w2_buf[slot, :, cs])
        for cp in slab_copies((f * MLP_STEP_SLABS + k + MLP_SLOTS) % n_slabs, slot):
            cp.start()

    @pl.when((i == n_i - 1) & (f == n_f - 1))
    def _():
        row_copy(0).wait()
        for slot in range(MLP_SLOTS):
            for cp in slab_copies(0, slot):
                cp.wait()

    if final:
        @pl.when(f == n_f - 1)
        def _():
            _rms_rows(o_ref, gss_ref.at[4:5, :], o_ref)


def _mlp(x, gss, w1_all, w2_all, layer, tm, tf, final, cast_next=()):
    n_rows, d = x.shape
    d_ff = w1_all.shape[2]
    n_f = d_ff // (tf * MLP_STEP_SLABS)
    c_in, c_out, c_shape, c_args, c_vmem = _cast_specs(
        cast_next, (n_rows // tm) * n_f, lambda i, f: i * n_f + f)
    vmem = (3 * tm * d * 4 + MLP_SLOTS * 2 * d * tf * 2 + tm * d * 2 + tm * tf * 6
            + 2 * tm * MLP_OUT_CHUNK * 4 + c_vmem)
    outs = pl.pallas_call(
        functools.partial(_mlp_kernel, final=final, n_cast=len(cast_next)),
        grid=(n_rows // tm, n_f),
        in_specs=[
            pl.BlockSpec(memory_space=pl.ANY),
            pl.BlockSpec((8, d), lambda i, f: (0, 0)),
            pl.BlockSpec(memory_space=pl.ANY),
            pl.BlockSpec(memory_space=pl.ANY),
            *c_in,
        ],
        out_specs=[pl.BlockSpec((tm, d), lambda i, f: (i, 0)), *c_out],
        out_shape=[jax.ShapeDtypeStruct((n_rows, d), F32), *c_shape],
        scratch_shapes=[
            pltpu.VMEM((tm, d), BF16), pltpu.VMEM((8, d), F32), pltpu.VMEM((tm, d), F32),
            pltpu.VMEM((MLP_SLOTS, d, tf), BF16), pltpu.VMEM((MLP_SLOTS, tf, d), BF16),
            pltpu.SemaphoreType.DMA((1,)), pltpu.SemaphoreType.DMA((2, MLP_SLOTS)),
        ],
        compiler_params=_params(("arbitrary", "arbitrary"), vmem),
        cost_estimate=pl.CostEstimate(
            flops=4 * n_rows * d * d_ff, transcendentals=n_rows,
            bytes_accessed=(2 * n_rows * d * 4 + (n_rows // tm) * 2 * d * d_ff * 2
                            + sum(6 * s.shape[1] * s.shape[2] for s in c_shape))),
        name="mlp_final" if final else "mlp",
    )(x, gss, w1_all[layer], w2_all[layer], *c_args)
    return outs[0], tuple(outs[1:])


def _rope_tables(n_rows):
    quarter = HEAD_DIM // 4
    inv = ROPE_THETA ** (-jnp.arange(quarter, dtype=F32) / quarter)
    pos = jnp.arange(n_rows)
    rows = (pos // GRID_W).astype(F32)
    cols = (pos % GRID_W).astype(F32)
    ang = jnp.concatenate([rows[:, None] * inv[None, :]] * 2 + [cols[:, None] * inv[None, :]] * 2, axis=1)
    sign = jnp.tile(jnp.concatenate([-jnp.ones(quarter, F32), jnp.ones(quarter, F32)]), 2)
    return jnp.cos(ang), jnp.sin(ang) * sign[None, :]


def _rows8(*vecs):
    d = vecs[0].shape[0]
    pad = [jnp.zeros((d,), F32)] * (8 - len(vecs))
    return jnp.stack(list(vecs) + pad, axis=0)


def _row_tile(n_rows, want):
    return min(want, n_rows)


def kernel(x, c, ctx, c_ctx, ada_a, ada_b, ada_bias, g_mix, w_in, sink, w_f, g_attn_out, g_four_out,
           w_out, g_mlp, w1, w2, g_final):
    assert x.shape[0] == 1 and ctx.shape[0] == 1
    depth = w_in.shape[0]
    d = x.shape[2]
    xs = x[0]
    cs = ctx[0]
    n_lat, n_ctx = xs.shape[0], cs.shape[0]

    cv = jnp.zeros((ADA_ROWS, d), F32).at[0].set(c[0]).at[1].set(c_ctx)
    mods = _ada(cv, ada_a, ada_b, ada_bias)
    rope = _rope_tables(n_lat)
    fconst = _FourierConsts(n_lat, n_ctx)
    wf_b = w_f.astype(BF16)
    w_in_b = w_in[0:1].astype(BF16)
    w_out_b = w1_b = w2_b = None

    for l in range(depth):
        last = l == depth - 1
        m_lat = [mods[l, 0, i * d:(i + 1) * d] for i in range(N_MOD)]
        m_ctx = [mods[l, 1, i * d:(i + 1) * d] for i in range(N_MOD)]

        hc = _normmod(cs, _rows8(g_mix[l], m_ctx[0], m_ctx[1]), _row_tile(n_ctx, 256))
        hx = _normmod(xs, _rows8(g_mix[l], m_lat[0], m_lat[1]), _row_tile(n_lat, 1024))
        pc, pcf, _ = _inproj(hc, w_in_b, 0, None, _row_tile(n_ctx, 1024), 1024)
        px, pxf, cast1 = _inproj(hx, w_in_b, 0, rope, _row_tile(n_lat, 1024), 1024,
                                 cast=((w1, 0),) if l == 0 else ())

        na, cast2 = _latent_attn(px, pc, sink[l], g_attn_out[l],
                                 cast=((w2, 0), (w_out, 0)) if l == 0 else ())
        if l == 0:
            (w1_b,), (w2_b, w_out_b) = cast1, cast2
        nf = _latent_fourier(pxf, fconst, wf_b, l, g_four_out[l], run_after=pc)
        x_mid = _outproj(na, nf, w_out_b, 0, xs, m_lat[2], _row_tile(n_lat, 1024), 1024)
        gss = _rows8(g_mlp[l], m_lat[3], m_lat[4], m_lat[5], g_final)
        cast_next = () if last else tuple((w, l + 1) for w in (w_in, w_out, w1, w2))
        xs, next_weights = _mlp(x_mid, gss, w1_b, w2_b, 0, _row_tile(n_lat, 512), 512, final=last,
                                cast_next=cast_next)

        if not last:
            nac = _ctx_attn(pc, sink[l], g_attn_out[l])
            nfc = _ctx_fourier(pcf, fconst, wf_b, l, g_four_out[l])
            c_mid = _outproj(nac, nfc, w_out_b, 0, cs, m_ctx[2], _row_tile(n_ctx, 1024), 1024)
            gss_c = _rows8(g_mlp[l], m_ctx[3], m_ctx[4], m_ctx[5])
            cs, _ = _mlp(c_mid, gss_c, w1_b, w2_b, 0, _row_tile(n_ctx, 512), 512, final=False)
            w_in_b, w_out_b, w1_b, w2_b = next_weights

    return xs[None]
```

```python
import functools
import math

import numpy as np
import jax
import jax.numpy as jnp
from jax import lax
from jax.experimental import pallas as pl
from jax.experimental.pallas import tpu as pltpu

F32 = jnp.float32
BF16 = jnp.bfloat16

N_HEADS = 16
N_KV_HEADS = 4
HEAD_DIM = 128
GROUP = N_HEADS // N_KV_HEADS
Q_WIDTH = N_HEADS * HEAD_DIM
KV_WIDTH = N_KV_HEADS * HEAD_DIM
BLOCK = 128
GRID_W = 64
N_FGROUPS = 4
F_DIM = 512
F_WIDTH = N_FGROUPS * F_DIM
N_MOD = 6
ROPE_THETA = 10000.0
EPS = 1e-6
NEG_INF = -1e30
LOG2E = 1.4426950408889634

V7X_VMEM_BYTES = 64 * 1024 * 1024
ADA_ROWS = 8
MLP_OUT_CHUNK = 512
MLP_SLOTS = 2
MLP_STEP_SLABS = 4
ROW_CHUNK = 16
ROW_UNROLL = 2
STREAM_UNROLL = 8
DOT_COLS = 256
SOFTMAX_ROWS = 32
ATTN_Q_BLOCKS = 4
HEAD_RESIDUES = 16
TAIL_BLOCKS = 4


def _params(semantics, vmem_bytes):
    limit = min(int(vmem_bytes * 1.15) + (4 << 20), V7X_VMEM_BYTES - (2 << 20))
    return pltpu.CompilerParams(dimension_semantics=semantics, vmem_limit_bytes=limit)


def _dot(a, b):
    return jnp.dot(a, b, preferred_element_type=F32)


def _dot_nt(a, b):
    return lax.dot_general(a, b, (((1,), (1,)), ((), ())), preferred_element_type=F32)


def _for_row_chunks(n_rows, body, unroll=ROW_UNROLL):
    rc = min(ROW_CHUNK, n_rows)

    def step(r, carry):
        body(pl.ds(pl.multiple_of(r * rc, rc), rc))
        return carry

    lax.fori_loop(0, n_rows // rc, step, 0, unroll=min(unroll, n_rows // rc))


def _normmod_rows(x_ref, gss_ref, ops_ref, h_ref, copy_ref=None):
    ops_ref[0:1, :] = gss_ref[0:1, :] * (1.0 + gss_ref[2:3, :])

    def body(rows):
        x = x_ref[rows, :]
        y = x * lax.rsqrt(jnp.mean(x * x, axis=-1, keepdims=True) + EPS)
        h_ref[rows, :] = (y * ops_ref[0:1, :] + gss_ref[1:2, :]).astype(h_ref.dtype)
        if copy_ref is not None:
            copy_ref[rows, :] = x

    _for_row_chunks(x_ref.shape[0], body)


def _rms_rows_from_sumsq(src_ref, ss_ref, g_ref, dst_ref):
    width = src_ref.shape[1]

    def body(rows):
        inv = lax.rsqrt(ss_ref[rows, :] * (1.0 / width) + EPS)
        dst_ref[rows, :] = (src_ref[rows, :] * inv * g_ref[...]).astype(dst_ref.dtype)

    _for_row_chunks(src_ref.shape[0], body, unroll=STREAM_UNROLL)


def _sumsq(x):
    return jnp.sum(x * x, axis=-1, keepdims=True)


def _cast_slab(shape, n_steps):
    n_r, n_c = shape
    for q in (1, 2, 4, 8, 16):
        if n_steps % q or n_c % (q * 128) or n_r % (n_steps // q):
            continue
        br = n_r // (n_steps // q)
        if br % 16 == 0:
            return br, n_c // q, q
    raise ValueError(f"no bf16-tile-aligned slab split of {shape} into {n_steps} steps")


def _cast_specs(jobs, n_steps, step_of):
    ins, outs, shapes, args, vmem = [], [], [], [], 0
    for w_raw, src in jobs:
        br, bc, q = _cast_slab(w_raw.shape[1:], n_steps)
        ins.append(pl.BlockSpec(
            (None, br, bc), lambda *g, src=src, q=q: (src, step_of(*g) // q, step_of(*g) % q)))
        outs.append(pl.BlockSpec((None, br, bc), lambda *g, q=q: (0, step_of(*g) // q, step_of(*g) % q)))
        shapes.append(jax.ShapeDtypeStruct((1,) + w_raw.shape[1:], BF16))
        args.append(w_raw)
        vmem += 2 * br * bc * 6
    return ins, outs, shapes, args, vmem


def _cast_slabs(raw_refs, cast_refs):
    for raw_ref, cast_ref in zip(raw_refs, cast_refs):
        cast_ref[...] = raw_ref[...].astype(cast_ref.dtype)


def _ada_kernel(cv_ref, a_ref, b_ref, bias_ref, o_ref, h_ref):
    @pl.when(pl.program_id(1) == 0)
    def _():
        cv = cv_ref[...]
        s = cv * jax.nn.sigmoid(cv)
        h_ref[...] = _dot(s.astype(BF16), a_ref[0].astype(BF16))

    o_ref[0] = _dot(h_ref[...].astype(BF16), b_ref[0].astype(BF16)) + bias_ref[0]


def _ada(cv, ada_a, ada_b, ada_bias):
    depth, d, rank = ada_a.shape
    n = ada_b.shape[2]
    tn = d
    vmem = 2 * (d * rank * 4 + rank * tn * 4) + 4 * ADA_ROWS * (d + tn) * 4
    return pl.pallas_call(
        _ada_kernel,
        grid=(depth, n // tn),
        in_specs=[
            pl.BlockSpec((ADA_ROWS, d), lambda l, j: (0, 0)),
            pl.BlockSpec((1, d, rank), lambda l, j: (l, 0, 0)),
            pl.BlockSpec((1, rank, tn), lambda l, j: (l, 0, j)),
            pl.BlockSpec((1, 1, tn), lambda l, j: (l, 0, j)),
        ],
        out_specs=pl.BlockSpec((1, ADA_ROWS, tn), lambda l, j: (l, 0, j)),
        out_shape=jax.ShapeDtypeStruct((depth, ADA_ROWS, n), F32),
        scratch_shapes=[pltpu.VMEM((ADA_ROWS, rank), F32)],
        compiler_params=_params(("parallel", "arbitrary"), vmem),
        name="ada",
    )(cv, ada_a, ada_b, ada_bias.reshape(depth, 1, n))


def _rope_chunk(a, cos, sin, lane):
    up = pltpu.roll(a, HEAD_DIM - 32, axis=1)
    down = pltpu.roll(a, 32, axis=1)
    partner = jnp.where((lane % 64) < 32, up, down)
    return a * cos + partner * sin


def _normmod_kernel(x_ref, gss_ref, h_ref, ops_ref):
    _normmod_rows(x_ref, gss_ref, ops_ref, h_ref)


def _normmod(x, gss, tm):
    n_rows, d = x.shape
    return pl.pallas_call(
        _normmod_kernel,
        grid=(n_rows // tm,),
        in_specs=[pl.BlockSpec((tm, d), lambda i: (i, 0)), pl.BlockSpec((8, d), lambda i: (0, 0))],
        out_specs=pl.BlockSpec((tm, d), lambda i: (i, 0)),
        out_shape=jax.ShapeDtypeStruct((n_rows, d), BF16),
        scratch_shapes=[pltpu.VMEM((8, d), F32)],
        compiler_params=_params(("parallel",), 2 * tm * d * 6 + 16 * d * 4),
        cost_estimate=pl.CostEstimate(flops=8 * n_rows * d, transcendentals=n_rows, bytes_accessed=n_rows * d * 6),
        name="normmod",
    )(x, gss)


def _inproj_kernel(h_ref, w_ref, *rest, rope_cols, tn, n_cast):
    if rope_cols:
        cos_ref, sin_ref = rest[:2]
        rest = rest[2:]
    raw_refs = rest[:n_cast]
    qkv_ref, f_ref = rest[n_cast:n_cast + 2]
    cast_refs = rest[n_cast + 2:]
    j = pl.program_id(1)
    heads_per_dot = DOT_COLS // HEAD_DIM

    def tile(n_rope, o_ref):
        _cast_slabs(raw_refs, cast_refs)
        if n_rope:
            lane = lax.broadcasted_iota(jnp.int32, (h_ref.shape[0], HEAD_DIM), 1)
        for c in range(tn // DOT_COLS):
            acc = _dot(h_ref[...], w_ref[:, c * DOT_COLS:(c + 1) * DOT_COLS])
            for k in range(heads_per_dot):
                head = c * heads_per_dot + k
                a = acc[:, k * HEAD_DIM:(k + 1) * HEAD_DIM]
                if head < n_rope:
                    a = _rope_chunk(a, cos_ref[...], sin_ref[...], lane)
                o_ref[:, head * HEAD_DIM:(head + 1) * HEAD_DIM] = a.astype(o_ref.dtype)

    qkv_tiles = (Q_WIDTH + 2 * KV_WIDTH) // tn
    full_tiles, part = divmod(rope_cols, tn)
    if full_tiles:
        pl.when(j < full_tiles)(functools.partial(tile, tn // HEAD_DIM, qkv_ref))
    if part:
        pl.when(j == full_tiles)(functools.partial(tile, part // HEAD_DIM, qkv_ref))
        full_tiles += 1
    if full_tiles < qkv_tiles:
        pl.when((j >= full_tiles) & (j < qkv_tiles))(functools.partial(tile, 0, qkv_ref))
    pl.when(j >= qkv_tiles)(functools.partial(tile, 0, f_ref))


def _inproj(h, w_all, layer, rope_tables, tm, tn, cast=()):
    n_rows, d = h.shape
    n_out = w_all.shape[2]
    qkv_cols = Q_WIDTH + 2 * KV_WIDTH
    qkv_tiles = qkv_cols // tn
    n_j = n_out // tn
    rope_cols = Q_WIDTH + KV_WIDTH if rope_tables is not None else 0
    in_specs = [
        pl.BlockSpec((tm, d), lambda i, j: (i, 0)),
        pl.BlockSpec((None, d, tn), lambda i, j: (layer, 0, j)),
    ]
    args = [h, w_all]
    if rope_tables is not None:
        in_specs += [pl.BlockSpec((tm, HEAD_DIM), lambda i, j: (i, 0))] * 2
        args += list(rope_tables)
    slabs = 1 << (n_j.bit_length() - 1)
    c_in, c_out, c_shape, c_args, c_vmem = _cast_specs(
        cast, (n_rows // tm) * slabs, lambda i, j: i * slabs + jnp.minimum(j, slabs - 1))
    vmem = (2 * (tm * d * 2 + d * tn * 2 + 2 * tm * tn * 2 + 2 * tm * HEAD_DIM * 4) + 4 * tm * DOT_COLS * 4
            + c_vmem)
    cost = pl.CostEstimate(
        flops=2 * n_rows * d * n_out, transcendentals=0,
        bytes_accessed=(n_rows * d * 2 + (n_rows // tm) * d * n_out * 2 + n_rows * n_out * 2
                        + sum(6 * s.shape[1] * s.shape[2] for s in c_shape)))
    outs = pl.pallas_call(
        functools.partial(_inproj_kernel, rope_cols=rope_cols, tn=tn, n_cast=len(cast)),
        grid=(n_rows // tm, n_j),
        in_specs=in_specs + c_in,
        out_specs=[
            pl.BlockSpec((tm, tn), lambda i, j: (i, jnp.minimum(j, qkv_tiles - 1))),
            pl.BlockSpec((tm, tn), lambda i, j: (i, jnp.maximum(j - qkv_tiles, 0))),
            *c_out,
        ],
        out_shape=[
            jax.ShapeDtypeStruct((n_rows, qkv_cols), BF16),
            jax.ShapeDtypeStruct((n_rows, n_out - qkv_cols), BF16),
            *c_shape,
        ],
        compiler_params=_params(("parallel", "arbitrary"), vmem),
        cost_estimate=cost,
        name="inproj_rope" if rope_cols else "inproj_ctx",
    )(*args, *c_args)
    return outs[0], outs[1], tuple(outs[2:])


def _stack_heads(q_ref, rows, h):
    return jnp.concatenate(
        [q_ref[rows, (h * GROUP + g) * HEAD_DIM:(h * GROUP + g + 1) * HEAD_DIM] for g in range(GROUP)], axis=0)


def _softmax_chunk(s_ref, p_ref, rows, sink, masks):
    n_tiles = s_ref.shape[1] // BLOCK
    tiles = []
    for t in range(n_tiles):
        v = s_ref[rows, t * BLOCK:(t + 1) * BLOCK]
        if masks.get(t) is not None:
            v = jnp.where(masks[t], v, NEG_INF)
        tiles.append(v)
    m_raw = jnp.max(functools.reduce(jnp.maximum, tiles), axis=-1, keepdims=True)
    m = jnp.maximum(m_raw * (HEAD_DIM ** -0.5), sink)
    mb = m * LOG2E
    es = [jnp.exp2(v * (HEAD_DIM ** -0.5 * LOG2E) - mb) for v in tiles]
    denom = jnp.sum(functools.reduce(jnp.add, es), axis=-1, keepdims=True) + jnp.exp2(sink * LOG2E - mb)
    inv = 1.0 / denom
    for t in range(n_tiles):
        p_ref[rows, t * BLOCK:(t + 1) * BLOCK] = (es[t] * inv).astype(p_ref.dtype)


def _attn_units(sink_ref, q_ref, units, acc_ref, ss_ref, s_ref, p_ref):
    def scores(u):
        q_rows, h, keys, _, _ = units[u]
        s_ref[u % 2] = _dot_nt(_stack_heads(q_ref, q_rows, h), keys())

    scores(0)
    for u, (q_rows, h, _, values, mask_fn) in enumerate(units):
        if u + 1 < len(units):
            scores(u + 1)
        n_q = q_rows.stop - q_rows.start
        chunks_per_head = n_q // SOFTMAX_ROWS
        for r in range(GROUP * chunks_per_head):
            rows = slice(r * SOFTMAX_ROWS, (r + 1) * SOFTMAX_ROWS)
            sink = sink_ref[h * GROUP + r // chunks_per_head]
            q0 = (r % chunks_per_head) * SOFTMAX_ROWS
            _softmax_chunk(s_ref.at[u % 2], p_ref.at[u % 2], rows, sink, mask_fn(q0))
        o = _dot(p_ref[u % 2], values())
        ss = None
        for g in range(GROUP):
            c0 = (h * GROUP + g) * HEAD_DIM
            og = o[g * n_q:(g + 1) * n_q, :]
            acc_ref[q_rows, c0:c0 + HEAD_DIM] = og
            ss = _sumsq(og) if ss is None else ss + _sumsq(og)
        ss_ref[q_rows, :] = ss if h == 0 else ss_ref[q_rows, :] + ss


def _latent_attn_kernel(sink_ref, q_ref, kp_ref, kc_ref, kn_ref, vp_ref, vc_ref, vn_ref,
                        kx_ref, vx_ref, g_ref, *rest, n_ctx, q_blocks, n_cast):
    raw_refs = rest[:n_cast]
    o_ref = rest[n_cast]
    cast_refs = rest[n_cast + 1:2 * n_cast + 1]
    acc_ref, ss_ref, s_ref, p_ref = rest[2 * n_cast + 1:]
    _cast_slabs(raw_refs, cast_refs)
    n = pl.program_id(0)
    nb = pl.num_programs(0)
    ctx_tiles = n_ctx // BLOCK
    prev_lo = jnp.where(n == 0, BLOCK, 0)
    next_hi = jnp.where(n == nb - 1, 0, BLOCK)

    def block_of(refs, i):
        prev_ref, cur_ref, next_ref = refs
        if i == 0:
            return prev_ref, slice(0, BLOCK)
        if i == q_blocks + 1:
            return next_ref, slice(0, BLOCK)
        return cur_ref, slice((i - 1) * BLOCK, i * BLOCK)

    def operand(ctx_ref, refs, qb, h):
        hs = slice(h * HEAD_DIM, (h + 1) * HEAD_DIM)
        parts = [ctx_ref[:, hs]]
        for i in range(qb, qb + 3):
            ref, rows = block_of(refs, i)
            parts.append(ref[rows, hs])
        return jnp.concatenate(parts, axis=0)

    def mask_fn(qb, q0):
        qi = lax.broadcasted_iota(jnp.int32, (SOFTMAX_ROWS, BLOCK), 0) + q0
        kj = lax.broadcasted_iota(jnp.int32, (SOFTMAX_ROWS, BLOCK), 1)
        lo = kj >= qi
        hi = kj <= qi
        if qb == 0:
            lo = lo & (kj >= prev_lo)
        if qb == q_blocks - 1:
            hi = hi & (kj < next_hi)
        return {ctx_tiles: lo, ctx_tiles + 2: hi}

    units = []
    for qb in range(q_blocks):
        for h in range(N_KV_HEADS):
            units.append((
                slice(qb * BLOCK, (qb + 1) * BLOCK), h,
                functools.partial(operand, kx_ref, (kp_ref, kc_ref, kn_ref), qb, h),
                functools.partial(operand, vx_ref, (vp_ref, vc_ref, vn_ref), qb, h),
                functools.partial(mask_fn, qb),
            ))
    _attn_units(sink_ref, q_ref, units, acc_ref, ss_ref, s_ref, p_ref)
    _rms_rows_from_sumsq(acc_ref, ss_ref, g_ref, o_ref)


def _latent_attn(px, pc, sink, g_attn, cast=()):
    n_rows = px.shape[0]
    n_ctx = pc.shape[0]
    nb = n_rows // BLOCK
    qb = ATTN_Q_BLOCKS if nb % ATTN_Q_BLOCKS == 0 else 1
    kcol = Q_WIDTH // KV_WIDTH
    vcol = kcol + 1
    n_keys = n_ctx + 3 * BLOCK

    def kv_specs(colblk):
        return [
            pl.BlockSpec((BLOCK, KV_WIDTH), lambda n: (jnp.maximum(n * qb - 1, 0), colblk)),
            pl.BlockSpec((qb * BLOCK, KV_WIDTH), lambda n: (n, colblk)),
            pl.BlockSpec((BLOCK, KV_WIDTH), lambda n: (jnp.minimum(n * qb + qb, nb - 1), colblk)),
        ]

    in_specs = [
        pl.BlockSpec(memory_space=pltpu.SMEM),
        pl.BlockSpec((qb * BLOCK, Q_WIDTH), lambda n: (n, 0)),
        *kv_specs(kcol), *kv_specs(vcol),
        pl.BlockSpec((n_ctx, KV_WIDTH), lambda n: (0, kcol)),
        pl.BlockSpec((n_ctx, KV_WIDTH), lambda n: (0, vcol)),
        pl.BlockSpec((1, Q_WIDTH), lambda n: (0, 0)),
    ]
    c_in, c_out, c_shape, c_args, c_vmem = _cast_specs(cast, nb // qb, lambda n: n)
    outs = pl.pallas_call(
        functools.partial(_latent_attn_kernel, n_ctx=n_ctx, q_blocks=qb, n_cast=len(cast)),
        grid=(nb // qb,),
        in_specs=in_specs + c_in,
        out_specs=[pl.BlockSpec((qb * BLOCK, Q_WIDTH), lambda n: (n, 0)), *c_out],
        out_shape=[jax.ShapeDtypeStruct((n_rows, Q_WIDTH), BF16), *c_shape],
        scratch_shapes=[
            pltpu.VMEM((qb * BLOCK, Q_WIDTH), F32),
            pltpu.VMEM((qb * BLOCK, 1), F32),
            pltpu.VMEM((2, GROUP * BLOCK, n_keys), F32),
            pltpu.VMEM((2, GROUP * BLOCK, n_keys), BF16),
        ],
        compiler_params=_params(("parallel",), (24 << 20) + c_vmem),
        cost_estimate=pl.CostEstimate(
            flops=4 * n_rows * N_HEADS * n_keys * HEAD_DIM, transcendentals=n_rows * N_HEADS * n_keys,
            bytes_accessed=(n_rows * (2 * Q_WIDTH + 6 * KV_WIDTH) * 2
                            + sum(6 * s.shape[1] * s.shape[2] for s in c_shape))),
        name="latent_attn",
    )(sink, px, px, px, px, px, px, px, pc, pc, g_attn.reshape(1, Q_WIDTH), *c_args)
    return outs[0], tuple(outs[1:])


def _ctx_attn_kernel(sink_ref, q_ref, k_ref, v_ref, g_ref, o_ref, acc_ref, ss_ref, s_ref, p_ref):
    def operand(ref, h):
        return ref[:, h * HEAD_DIM:(h + 1) * HEAD_DIM]

    units = [(slice(0, q_ref.shape[0]), h, functools.partial(operand, k_ref, h),
              functools.partial(operand, v_ref, h), lambda q0: {}) for h in range(N_KV_HEADS)]
    _attn_units(sink_ref, q_ref, units, acc_ref, ss_ref, s_ref, p_ref)
    _rms_rows_from_sumsq(acc_ref, ss_ref, g_ref, o_ref)


def _ctx_attn(pc, sink, g_attn):
    n_ctx = pc.shape[0]
    kcol = Q_WIDTH // KV_WIDTH
    return pl.pallas_call(
        _ctx_attn_kernel,
        grid=(1,),
        in_specs=[
            pl.BlockSpec(memory_space=pltpu.SMEM),
            pl.BlockSpec((n_ctx, Q_WIDTH), lambda i: (0, 0)),
            pl.BlockSpec((n_ctx, KV_WIDTH), lambda i: (0, kcol)),
            pl.BlockSpec((n_ctx, KV_WIDTH), lambda i: (0, kcol + 1)),
            pl.BlockSpec((1, Q_WIDTH), lambda i: (0, 0)),
        ],
        out_specs=pl.BlockSpec((n_ctx, Q_WIDTH), lambda i: (0, 0)),
        out_shape=jax.ShapeDtypeStruct((n_ctx, Q_WIDTH), BF16),
        scratch_shapes=[
            pltpu.VMEM((n_ctx, Q_WIDTH), F32),
            pltpu.VMEM((n_ctx, 1), F32),
            pltpu.VMEM((2, GROUP * n_ctx, n_ctx), F32),
            pltpu.VMEM((2, GROUP * n_ctx, n_ctx), BF16),
        ],
        compiler_params=_params(("arbitrary",), 16 << 20),
        name="ctx_attn",
    )(sink, pc, pc, pc, g_attn.reshape(1, Q_WIDTH))


def _dft_tables(n):
    ang = 2.0 * np.pi * (np.outer(np.arange(n), np.arange(n)) % n) / n
    return np.cos(ang), np.sin(ang)


def _fourier_head_kernel(g_ref, z_ref, o_ref):
    for r in range(z_ref.shape[0]):
        o_ref[r] = _dot(g_ref[r], z_ref[r]).astype(o_ref.dtype)


def _fourier_head(zt, gmat):
    nb, na, c = zt.shape
    rb = min(HEAD_RESIDUES, nb)
    return pl.pallas_call(
        _fourier_head_kernel,
        grid=(nb // rb,),
        in_specs=[
            pl.BlockSpec((rb, 2 * na, na), lambda b: (b, 0, 0)),
            pl.BlockSpec((rb, na, c), lambda b: (b, 0, 0)),
        ],
        out_specs=pl.BlockSpec((rb, 2 * na, c), lambda b: (b, 0, 0)),
        out_shape=jax.ShapeDtypeStruct((nb, 2 * na, c), BF16),
        compiler_params=_params(("parallel",), 6 * rb * na * c * 2),
        cost_estimate=pl.CostEstimate(
            flops=4 * nb * na * na * c, transcendentals=0,
            bytes_accessed=(3 * nb * na * c + 2 * nb * na * na) * 2),
        name="fourier_head",
    )(gmat, zt)


def _fourier_tail_kernel(m_ref, d_ref, cs_ref, wf_ref, g_ref, *rest, pos_scale):
    o_ref, acc_ref, ss_ref = rest[-3:]
    kb, p = o_ref.shape[0], o_ref.shape[1]
    xs = [(_dot(m_ref[...], d_ref[blk]) * pos_scale).astype(BF16) for blk in range(kb)]
    lhs = jnp.concatenate(
        [jnp.concatenate([x[:p, g * F_DIM:(g + 1) * F_DIM], x[p:, g * F_DIM:(g + 1) * F_DIM]], axis=1)
         for g in range(N_FGROUPS) for x in xs], axis=0)
    f = (_dot(lhs, cs_ref[...]) * (F_DIM ** -0.5)).astype(BF16)
    for g in range(N_FGROUPS):
        og = _dot(f[g * kb * p:(g + 1) * kb * p, :], wf_ref[g])
        for blk in range(kb):
            piece = og[blk * p:(blk + 1) * p, :]
            acc_ref[blk, :, g * F_DIM:(g + 1) * F_DIM] = piece
            ss_ref[blk] = _sumsq(piece) if g == 0 else ss_ref[blk] + _sumsq(piece)
    for blk in range(kb):
        _rms_rows_from_sumsq(acc_ref.at[blk], ss_ref.at[blk], g_ref, o_ref.at[blk])


def _fourier_tail(stage_mat, data, cs, wf_all, layer, g_four, pos_scale, run_after=None):
    nblk, k_in, c = data.shape
    p = stage_mat.shape[0] // 2
    kb = min(TAIL_BLOCKS, nblk)
    temps = kb * p * c * (4 + 2 + 2 + 4 + 2)
    vmem = 2 * kb * (k_in + p) * c * 2 + kb * p * c * 4 + 2 * (2 * p * k_in + 6 * F_DIM * F_DIM) * 2 + temps
    in_specs = [
        pl.BlockSpec((2 * p, k_in), lambda i: (0, 0)),
        pl.BlockSpec((kb, k_in, c), lambda i: (i, 0, 0)),
        pl.BlockSpec((2 * F_DIM, F_DIM), lambda i: (0, 0)),
        pl.BlockSpec((None, N_FGROUPS, F_DIM, F_DIM), lambda i: (layer, 0, 0, 0)),
        pl.BlockSpec((1, c), lambda i: (0, 0)),
    ]
    args = [stage_mat, data, cs, wf_all, g_four.reshape(1, c)]
    if run_after is not None:
        in_specs.append(pl.BlockSpec((16, 128), lambda i: (0, 0)))
        args.append(run_after)
    return pl.pallas_call(
        functools.partial(_fourier_tail_kernel, pos_scale=pos_scale),
        grid=(nblk // kb,),
        in_specs=in_specs,
        out_specs=pl.BlockSpec((kb, p, c), lambda i: (i, 0, 0)),
        out_shape=jax.ShapeDtypeStruct((nblk, p, c), BF16),
        scratch_shapes=[pltpu.VMEM((kb, p, c), F32), pltpu.VMEM((kb, p, 1), F32)],
        compiler_params=_params(("parallel",), vmem),
        cost_estimate=pl.CostEstimate(
            flops=2 * nblk * (2 * p * k_in * c + p * c * 2 * F_DIM + p * c * F_DIM), transcendentals=nblk * p,
            bytes_accessed=nblk * (k_in + p) * c * 2 + 2 * p * k_in * 2 + 6 * F_DIM * F_DIM * 2),
        name="fourier_tail",
    )(*args)


class _FourierConsts:
    def __init__(self, n_lat, n_ctx):
        a = b = int(round(math.sqrt(n_lat)))
        assert a * b == n_lat
        self.a, self.b = a, b
        k_lo = np.arange(a)[None, :, None]
        n = (np.arange(a)[None, None, :] * b + np.arange(b)[:, None, None])
        ang = 2.0 * np.pi * ((k_lo * n) % n_lat) / n_lat
        self.head = jnp.asarray(np.concatenate([np.cos(ang), -np.sin(ang)], axis=1), BF16)
        cb, sb = _dft_tables(b)
        self.tail = jnp.asarray(np.block([[cb, sb], [-sb, cb]]), BF16)
        cc, sc = _dft_tables(n_ctx)
        self.ctx = jnp.asarray(np.concatenate([cc, -sc], axis=0), BF16)
        cf, sf = _dft_tables(F_DIM)
        self.chan = jnp.asarray(np.concatenate([cf, sf], axis=0), BF16)


def _latent_fourier(pf, fc, wf_all, layer, g_four, run_after):
    n_rows = pf.shape[0]
    a, b = fc.a, fc.b
    zt = jnp.transpose(pf.reshape(a, b, F_WIDTH), (1, 0, 2))
    t = _fourier_head(zt, fc.head)
    t = jnp.transpose(t.reshape(b, 2, a, F_WIDTH), (2, 1, 0, 3)).reshape(a, 2 * b, F_WIDTH)
    o = _fourier_tail(fc.tail, t, fc.chan, wf_all, layer, g_four, 1.0 / math.sqrt(n_rows),
                      run_after=run_after)
    return jnp.transpose(o, (1, 0, 2)).reshape(n_rows, F_WIDTH)


def _ctx_fourier(pf, fc, wf_all, layer, g_four):
    n_ctx = pf.shape[0]
    d = pf.reshape(1, n_ctx, F_WIDTH)
    o = _fourier_tail(fc.ctx, d, fc.chan, wf_all, layer, g_four, 1.0 / math.sqrt(n_ctx))
    return o.reshape(n_ctx, F_WIDTH)


def _outproj_kernel(na_ref, nf_ref, w_ref, x_ref, gate_ref, o_ref):
    ka = na_ref.shape[1]
    acc = _dot(na_ref[...], w_ref[:ka, :]) + _dot(nf_ref[...], w_ref[ka:, :])
    o_ref[...] = x_ref[...] + gate_ref[...] * acc


def _outproj(na, nf, w_all, layer, x, gate, tm, tn):
    n_rows, d = x.shape
    ka, kf = na.shape[1], nf.shape[1]
    vmem = 2 * (tm * (ka + kf) * 2 + (ka + kf) * tn * 2 + 2 * tm * tn * 4) + tm * tn * 4
    return pl.pallas_call(
        _outproj_kernel,
        grid=(n_rows // tm, d // tn),
        in_specs=[
            pl.BlockSpec((tm, ka), lambda i, j: (i, 0)),
            pl.BlockSpec((tm, kf), lambda i, j: (i, 0)),
            pl.BlockSpec((None, ka + kf, tn), lambda i, j: (layer, 0, j)),
            pl.BlockSpec((tm, tn), lambda i, j: (i, j)),
            pl.BlockSpec((1, tn), lambda i, j: (0, j)),
        ],
        out_specs=pl.BlockSpec((tm, tn), lambda i, j: (i, j)),
        out_shape=jax.ShapeDtypeStruct((n_rows, d), F32),
        compiler_params=_params(("parallel", "parallel"), vmem),
        cost_estimate=pl.CostEstimate(
            flops=2 * n_rows * (ka + kf) * d, transcendentals=0,
            bytes_accessed=n_rows * (ka + kf) * 2 + (n_rows // tm) * (ka + kf) * d * 2 + 2 * n_rows * d * 4),
        name="outproj",
    )(na, nf, w_all, x, gate.reshape(1, d))


def _mlp_kernel(x_hbm, gss_ref, w1_hbm, w2_hbm, *rest, final, n_cast):
    raw_refs = rest[:n_cast]
    o_ref = rest[n_cast]
    cast_refs = rest[n_cast + 1:2 * n_cast + 1]
    h_ref, ops_ref, ss_ref, x_buf, w1_buf, w2_buf, x_sem, w_sem = rest[2 * n_cast + 1:]
    i, f = pl.program_id(0), pl.program_id(1)
    n_i, n_f = pl.num_programs(0), pl.num_programs(1)
    tm = x_buf.shape[0]
    tf = w1_buf.shape[2]
    n_slabs = n_f * MLP_STEP_SLABS

    def row_copy(block):
        row = pl.multiple_of(block * tm, tm)
        return pltpu.make_async_copy(x_hbm.at[pl.ds(row, tm), :], x_buf, x_sem.at[0])

    def slab_copies(slab, slot):
        col = pl.multiple_of(slab * tf, tf)
        return (
            pltpu.make_async_copy(w1_hbm.at[:, pl.ds(col, tf)], w1_buf.at[slot], w_sem.at[0, slot]),
            pltpu.make_async_copy(w2_hbm.at[pl.ds(col, tf), :], w2_buf.at[slot], w_sem.at[1, slot]),
        )

    @pl.when((i == 0) & (f == 0))
    def _():
        row_copy(0).start()
        for slot in range(MLP_SLOTS):
            for cp in slab_copies(slot, slot):
                cp.start()

    @pl.when(f == 0)
    def _():
        row_copy(0).wait()
        _normmod_rows(x_buf, gss_ref, ops_ref, h_ref, copy_ref=o_ref)
        row_copy(jnp.minimum(i + 1, n_i - 1)).start()

    _cast_slabs(raw_refs, cast_refs)
    tn = MLP_OUT_CHUNK
    for k in range(MLP_STEP_SLABS):
        slot = k % MLP_SLOTS
        for cp in slab_copies(0, slot):
            cp.wait()
        u = jnp.maximum(_dot(h_ref[...], w1_buf[slot]), 0.0)
        u = (u * u).astype(BF16)
        ss = None
        for c in range(o_ref.shape[1] // tn):
            cs = slice(c * tn, (c + 1) * tn)
            new = o_ref[:, cs] + gss_ref[3:4, cs] * _dot(u, w2_buf[slot, :, cs])
            o_ref[:, cs] = new
            if final and k == MLP_STEP_SLABS - 1:
                ss = _sumsq(new) if ss is None else ss + _sumsq(new)
        if ss is not None:
            ss_ref[...] = ss
        for cp in slab_copies((f * MLP_STEP_SLABS + k + MLP_SLOTS) % n_slabs, slot):
            cp.start()

    @pl.when((i == n_i - 1) & (f == n_f - 1))
    def _():
        row_copy(0).wait()
        for slot in range(MLP_SLOTS):
            for cp in slab_copies(0, slot):
                cp.wait()

    if final:
        @pl.when(f == n_f - 1)
        def _():
            _rms_rows_from_sumsq(o_ref, ss_ref, gss_ref.at[4:5, :], o_ref)


def _mlp(x, gss, w1_all, w2_all, layer, tm, tf, final, cast_next=()):
    n_rows, d = x.shape
    d_ff = w1_all.shape[2]
    n_f = d_ff // (tf * MLP_STEP_SLABS)
    c_in, c_out, c_shape, c_args, c_vmem = _cast_specs(
        cast_next, (n_rows // tm) * n_f, lambda i, f: i * n_f + f)
    vmem = (3 * tm * d * 4 + MLP_SLOTS * 2 * d * tf * 2 + tm * d * 2 + tm * tf * 6
            + 2 * tm * MLP_OUT_CHUNK * 4 + c_vmem)
    outs = pl.pallas_call(
        functools.partial(_mlp_kernel, final=final, n_cast=len(cast_next)),
        grid=(n_rows // tm, n_f),
        in_specs=[
            pl.BlockSpec(memory_space=pl.ANY),
            pl.BlockSpec((8, d), lambda i, f: (0, 0)),
            pl.BlockSpec(memory_space=pl.ANY),
            pl.BlockSpec(memory_space=pl.ANY),
            *c_in,
        ],
        out_specs=[pl.BlockSpec((tm, d), lambda i, f: (i, 0)), *c_out],
        out_shape=[jax.ShapeDtypeStruct((n_rows, d), F32), *c_shape],
        scratch_shapes=[
            pltpu.VMEM((tm, d), BF16), pltpu.VMEM((8, d), F32), pltpu.VMEM((tm, 1), F32),
            pltpu.VMEM((tm, d), F32),
            pltpu.VMEM((MLP_SLOTS, d, tf), BF16), pltpu.VMEM((MLP_SLOTS, tf, d), BF16),
            pltpu.SemaphoreType.DMA((1,)), pltpu.SemaphoreType.DMA((2, MLP_SLOTS)),
        ],
        compiler_params=_params(("arbitrary", "arbitrary"), vmem),
        cost_estimate=pl.CostEstimate(
            flops=4 * n_rows * d * d_ff, transcendentals=n_rows,
            bytes_accessed=(2 * n_rows * d * 4 + (n_rows // tm) * 2 * d * d_ff * 2
                            + sum(6 * s.shape[1] * s.shape[2] for s in c_shape))),
        name="mlp_final" if final else "mlp",
    )(x, gss, w1_all[layer], w2_all[layer], *c_args)
    return outs[0], tuple(outs[1:])


def _rope_tables(n_rows):
    quarter = HEAD_DIM // 4
    inv = ROPE_THETA ** (-jnp.arange(quarter, dtype=F32) / quarter)
    pos = jnp.arange(n_rows)
    rows = (pos // GRID_W).astype(F32)
    cols = (pos % GRID_W).astype(F32)
    ang = jnp.concatenate([rows[:, None] * inv[None, :]] * 2 + [cols[:, None] * inv[None, :]] * 2, axis=1)
    sign = jnp.tile(jnp.concatenate([-jnp.ones(quarter, F32), jnp.ones(quarter, F32)]), 2)
    return jnp.cos(ang), jnp.sin(ang) * sign[None, :]


def _rows8(*vecs):
    d = vecs[0].shape[0]
    pad = [jnp.zeros((d,), F32)] * (8 - len(vecs))
    return jnp.stack(list(vecs) + pad, axis=0)


def _row_tile(n_rows, want):
    return min(want, n_rows)


def kernel(x, c, ctx, c_ctx, ada_a, ada_b, ada_bias, g_mix, w_in, sink, w_f, g_attn_out, g_four_out,
           w_out, g_mlp, w1, w2, g_final):
    assert x.shape[0] == 1 and ctx.shape[0] == 1
    depth = w_in.shape[0]
    d = x.shape[2]
    xs = x[0]
    cs = ctx[0]
    n_lat, n_ctx = xs.shape[0], cs.shape[0]

    cv = jnp.zeros((ADA_ROWS, d), F32).at[0].set(c[0]).at[1].set(c_ctx)
    mods = _ada(cv, ada_a, ada_b, ada_bias)
    rope = _rope_tables(n_lat)
    fconst = _FourierConsts(n_lat, n_ctx)
    wf_b = w_f.astype(BF16)
    w_in_b = w_in[0:1].astype(BF16)
    w_out_b = w1_b = w2_b = None

    for l in range(depth):
        last = l == depth - 1
        m_lat = [mods[l, 0, i * d:(i + 1) * d] for i in range(N_MOD)]
        m_ctx = [mods[l, 1, i * d:(i + 1) * d] for i in range(N_MOD)]

        hc = _normmod(cs, _rows8(g_mix[l], m_ctx[0], m_ctx[1]), _row_tile(n_ctx, 256))
        hx = _normmod(xs, _rows8(g_mix[l], m_lat[0], m_lat[1]), _row_tile(n_lat, 1024))
        pc, pcf, _ = _inproj(hc, w_in_b, 0, None, _row_tile(n_ctx, 1024), 1024)
        px, pxf, cast1 = _inproj(hx, w_in_b, 0, rope, _row_tile(n_lat, 1024), 1024,
                                 cast=((w1, 0),) if l == 0 else ())

        na, cast2 = _latent_attn(px, pc, sink[l], g_attn_out[l],
                                 cast=((w2, 0), (w_out, 0)) if l == 0 else ())
        if l == 0:
            (w1_b,), (w2_b, w_out_b) = cast1, cast2
        nf = _latent_fourier(pxf, fconst, wf_b, l, g_four_out[l], run_after=pc)
        x_mid = _outproj(na, nf, w_out_b, 0, xs, m_lat[2], _row_tile(n_lat, 1024), 1024)
        gss = _rows8(g_mlp[l], m_lat[3], m_lat[4], m_lat[5], g_final)
        cast_next = () if last else tuple((w, l + 1) for w in (w_in, w_out, w1, w2))
        xs, next_weights = _mlp(x_mid, gss, w1_b, w2_b, 0, _row_tile(n_lat, 512), 512, final=last,
                                cast_next=cast_next)

        if not last:
            nac = _ctx_attn(pc, sink[l], g_attn_out[l])
            nfc = _ctx_fourier(pcf, fconst, wf_b, l, g_four_out[l])
            c_mid = _outproj(nac, nfc, w_out_b, 0, cs, m_ctx[2], _row_tile(n_ctx, 1024), 1024)
            gss_c = _rows8(g_mlp[l], m_ctx[3], m_ctx[4], m_ctx[5])
            cs, _ = _mlp(c_mid, gss_c, w1_b, w2_b, 0, _row_tile(n_ctx, 512), 512, final=False)
            w_in_b, w_out_b, w1_b, w2_b = next_weights

    return xs[None]
```
